```python
import jax, jax.numpy as jnp
from jax import lax
import numpy as np

D_MODEL = 1024
BATCH = 4
SEQ = 8192
DEPTH = 2

CHUNK = 64
D_MIX = 1024
M_HEADS = 4
M_HEAD_DIM = 128
M_WIDTH = M_HEADS * M_HEAD_DIM
CONV_W = 4
G_GROUPS = 4
G_GROUP_CH = 128
G_WIDTH = G_GROUPS * G_GROUP_CH
G_BLOCK = 128
D_IN = 2 * M_WIDTH + 2 * G_WIDTH
D_FF = 2752
N_EXPERTS = 8
TOP_K = 2
D_FF_EXPERT = 3584
N_DENSE = (DEPTH + 1) // 2
N_MOE = DEPTH // 2
EPS = 1e-6

kernel_name = "hybrid_mlstm_gmlp_moe_trunk"


def rmsnorm(x, g):
    xf = x.astype(jnp.float32)
    y = xf * lax.rsqrt(jnp.mean(xf * xf, axis=-1, keepdims=True) + EPS)
    return (y * g.astype(jnp.float32)).astype(x.dtype)


def layernorm_f32(xf, g):
    mu = jnp.mean(xf, axis=-1, keepdims=True)
    var = jnp.mean(jnp.square(xf - mu), axis=-1, keepdims=True)
    return (xf - mu) * lax.rsqrt(var + EPS) * g.astype(jnp.float32)


def causal_depthwise_conv(x, w, b):
    y = lax.conv_general_dilated(
        x, w[:, None, :].astype(x.dtype), window_strides=(1,),
        padding=[(w.shape[0] - 1, 0)],
        dimension_numbers=("NWC", "WIO", "NWC"),
        feature_group_count=x.shape[-1])
    return y + b.astype(x.dtype)


def mlstm_chunkwise(q, k, v, ig, lf):
    B, S, H, dh = q.shape
    nc = S // CHUNK

    def to_chunks(a):
        a = a.reshape((B, nc, CHUNK, H) + a.shape[3:])
        return jnp.moveaxis(jnp.moveaxis(a, 1, 0), 3, 2)

    tril = jnp.tril(jnp.ones((CHUNK, CHUNK), dtype=bool))

    def step(carry, xs):
        C, n, m = carry
        qc, kc, vc, igc, lfc = xs
        b = jnp.cumsum(lfc, axis=-1)
        d = b[..., :, None] - b[..., None, :] + igc[..., None, :]
        d = jnp.where(tril, d, -jnp.inf)
        inter = b + m[..., None]
        m_t = jnp.maximum(inter, jnp.max(d, axis=-1))
        w = jnp.exp(d - m_t[..., None])
        a = jnp.exp(inter - m_t)
        s = jnp.einsum("bhtk,bhsk->bhts", qc, kc) * w
        num = (a[..., None] * jnp.einsum("bhvk,bhtk->bhtv", C, qc)
               + jnp.einsum("bhts,bhsv->bhtv", s, vc))
        den = a * jnp.einsum("bhk,bhtk->bht", n, qc) + jnp.sum(s, axis=-1)
        den = jnp.maximum(jnp.abs(den), jnp.exp(-m_t))
        h = num / den[..., None]
        b_end = b[..., -1]
        log_prev = b_end + m
        log_s = b_end[..., None] - b + igc
        m_new = jnp.maximum(log_prev, jnp.max(log_s, axis=-1))
        a_prev = jnp.exp(log_prev - m_new)
        w_s = jnp.exp(log_s - m_new[..., None])
        C_new = a_prev[..., None, None] * C + jnp.einsum("bhs,bhsv,bhsk->bhvk", w_s, vc, kc)
        n_new = a_prev[..., None] * n + jnp.einsum("bhs,bhsk->bhk", w_s, kc)
        return (C_new, n_new, m_new), h

    init = (jnp.zeros((B, H, dh, dh), jnp.float32),
            jnp.zeros((B, H, dh), jnp.float32),
            jnp.zeros((B, H), jnp.float32))
    xs = (to_chunks(q), to_chunks(k), to_chunks(v), to_chunks(ig), to_chunks(lf))
    _, hs = lax.scan(step, init, xs)
    return jnp.transpose(hs, (1, 0, 3, 2, 4)).reshape(B, S, H, dh)


def hybrid_mixer(h, w_in, conv_w, conv_b, w_q, w_k, w_v, w_if, b_if, skip_m,
                 mh_norm_g, gm_v_g, w_sp, b_sp, gm_out_g, w_out):
    B, S, _ = h.shape
    proj = h @ w_in.astype(h.dtype)
    x_m, o_pre, u, v = jnp.split(proj, [M_WIDTH, 2 * M_WIDTH, 2 * M_WIDTH + G_WIDTH], axis=-1)

    x_c = jax.nn.silu(causal_depthwise_conv(x_m, conv_w, conv_b))
    xc_h = x_c.reshape(B, S, M_HEADS, M_HEAD_DIM)
    xm_h = x_m.reshape(B, S, M_HEADS, M_HEAD_DIM)
    q = jnp.einsum("bshd,hde->bshe", xc_h, w_q.astype(h.dtype))
    k = jnp.einsum("bshd,hde->bshe", xc_h, w_k.astype(h.dtype))
    vm = jnp.einsum("bshd,hde->bshe", xm_h, w_v.astype(h.dtype))
    qkv = jnp.concatenate([q.reshape(B, S, -1), k.reshape(B, S, -1), vm.reshape(B, S, -1)], axis=-1)
    gates = (qkv @ w_if.astype(h.dtype) + b_if.astype(h.dtype)).astype(jnp.float32)
    ig = gates[..., :M_HEADS]
    lf = jax.nn.log_sigmoid(gates[..., M_HEADS:])
    hm = mlstm_chunkwise((q * (M_HEAD_DIM ** -0.5)).astype(jnp.float32),
                         k.astype(jnp.float32), vm.astype(jnp.float32), ig, lf)
    o = jax.nn.sigmoid(o_pre.astype(jnp.float32)).reshape(B, S, M_HEADS, M_HEAD_DIM)
    hm = o * hm
    mu = jnp.mean(hm, axis=-1, keepdims=True)
    var = jnp.mean(jnp.square(hm - mu), axis=-1, keepdims=True)
    hm = ((hm - mu) * lax.rsqrt(var + EPS)).reshape(B, S, M_WIDTH)
    y_m = (hm * mh_norm_g.astype(jnp.float32)
           + skip_m.astype(jnp.float32) * x_c.astype(jnp.float32)).astype(h.dtype)

    u = jax.nn.gelu(u)
    vg = layernorm_f32(jax.nn.gelu(v).astype(jnp.float32), gm_v_g)
    vb = vg.reshape(B, S // G_BLOCK, G_BLOCK, G_GROUPS, G_GROUP_CH)
    pos = jnp.arange(G_BLOCK)
    chunk_mask = (pos[:, None] // CHUNK) >= (pos[None, :] // CHUNK)
    ws = jnp.where(chunk_mask[None], w_sp.astype(jnp.float32), 0.0)
    sv = jnp.einsum("gij,bnjgc->bnigc", ws, vb) + b_sp.astype(jnp.float32).T[:, :, None]
    y_g = u.astype(jnp.float32) * sv.reshape(B, S, G_WIDTH)
    y_g = rmsnorm(y_g, gm_out_g).astype(h.dtype)

    y = jnp.concatenate([y_m, y_g], axis=-1)
    return y @ w_out.astype(h.dtype)


def swiglu(t, w1, w3, w2):
    return (jax.nn.silu(t @ w1.astype(t.dtype)) * (t @ w3.astype(t.dtype))) @ w2.astype(t.dtype)


def moe_ffn(h, w_router, w1, w3, w2):
    B, S, D = h.shape
    t = h.reshape(-1, D)
    logits = (t @ w_router.astype(t.dtype)).astype(jnp.float32)
    top_v, top_i = lax.top_k(logits, TOP_K)
    gate = jax.nn.softmax(top_v, axis=-1)
    combine = jnp.sum(jax.nn.one_hot(top_i, N_EXPERTS, dtype=jnp.float32) * gate[..., None], axis=1)
    out = jnp.zeros_like(t)
    for e in range(N_EXPERTS):
        out = out + combine[:, e:e + 1].astype(t.dtype) * swiglu(t, w1[e], w3[e], w2[e])
    return out.reshape(B, S, D)


def setup_inputs(seed: int = 0) -> dict:
    key = jax.random.key(seed)
    ks = jax.random.split(key, 32)
    f32 = jnp.float32

    def nrm(k, shape, scale):
        return jax.random.normal(k, shape, f32) * scale

    def gain(k, shape):
        return 1.0 + 0.02 * jax.random.normal(k, shape, f32)

    b_i = 0.1 * jax.random.normal(ks[9], (DEPTH, M_HEADS), f32)
    b_f = jnp.linspace(3.0, 6.0, M_HEADS, dtype=f32)[None, :] + 0.1 * jax.random.normal(ks[10], (DEPTH, M_HEADS), f32)
    return {
        "x": nrm(ks[0], (BATCH, SEQ, D_MODEL), 1.0),
        "mix_norm_g": gain(ks[1], (DEPTH, D_MODEL)),
        "w_in": nrm(ks[2], (DEPTH, D_MODEL, D_IN), D_MODEL ** -0.5),
        "conv_w": nrm(ks[3], (DEPTH, CONV_W, M_WIDTH), CONV_W ** -0.5),
        "conv_b": nrm(ks[4], (DEPTH, M_WIDTH), 0.02),
        "w_q": nrm(ks[5], (DEPTH, M_HEADS, M_HEAD_DIM, M_HEAD_DIM), M_HEAD_DIM ** -0.5),
        "w_k": nrm(ks[6], (DEPTH, M_HEADS, M_HEAD_DIM, M_HEAD_DIM), M_HEAD_DIM ** -0.5),
        "w_v": nrm(ks[7], (DEPTH, M_HEADS, M_HEAD_DIM, M_HEAD_DIM), M_HEAD_DIM ** -0.5),
        "w_if": nrm(ks[8], (DEPTH, 3 * M_WIDTH, 2 * M_HEADS), 0.5 * (3 * M_WIDTH) ** -0.5),
        "b_if": jnp.concatenate([b_i, b_f], axis=-1),
        "skip_m": gain(ks[11], (DEPTH, M_WIDTH)),
        "mh_norm_g": gain(ks[12], (DEPTH, M_WIDTH)),
        "gm_v_g": gain(ks[13], (DEPTH, G_WIDTH)),
        "w_sp": nrm(ks[14], (DEPTH, G_GROUPS, G_BLOCK, G_BLOCK), 0.5 * G_BLOCK ** -0.5),
        "b_sp": gain(ks[15], (DEPTH, G_GROUPS, G_BLOCK)),
        "gm_out_g": gain(ks[16], (DEPTH, G_WIDTH)),
        "w_out": nrm(ks[17], (DEPTH, D_MIX, D_MODEL), D_MIX ** -0.5),
        "ffn_norm_g": gain(ks[18], (DEPTH, D_MODEL)),
        "dense_w1": nrm(ks[19], (N_DENSE, D_MODEL, D_FF), D_MODEL ** -0.5),
        "dense_w3": nrm(ks[20], (N_DENSE, D_MODEL, D_FF), D_MODEL ** -0.5),
        "dense_w2": nrm(ks[21], (N_DENSE, D_FF, D_MODEL), D_FF ** -0.5),
        "moe_router": nrm(ks[22], (N_MOE, D_MODEL, N_EXPERTS), D_MODEL ** -0.5),
        "moe_w1": nrm(ks[23], (N_MOE, N_EXPERTS, D_MODEL, D_FF_EXPERT), D_MODEL ** -0.5),
        "moe_w3": nrm(ks[24], (N_MOE, N_EXPERTS, D_MODEL, D_FF_EXPERT), D_MODEL ** -0.5),
        "moe_w2": nrm(ks[25], (N_MOE, N_EXPERTS, D_FF_EXPERT, D_MODEL), D_FF_EXPERT ** -0.5),
        "final_norm_g": gain(ks[26], (D_MODEL,)),
    }


def reference(x, mix_norm_g, w_in, conv_w, conv_b, w_q, w_k, w_v, w_if, b_if, skip_m,
              mh_norm_g, gm_v_g, w_sp, b_sp, gm_out_g, w_out, ffn_norm_g,
              dense_w1, dense_w3, dense_w2, moe_router, moe_w1, moe_w3, moe_w2,
              final_norm_g):
    for l in range(DEPTH):
        h = rmsnorm(x, mix_norm_g[l])
        x = x + hybrid_mixer(h, w_in[l], conv_w[l], conv_b[l], w_q[l], w_k[l], w_v[l],
                             w_if[l], b_if[l], skip_m[l], mh_norm_g[l], gm_v_g[l],
                             w_sp[l], b_sp[l], gm_out_g[l], w_out[l])
        h = rmsnorm(x, ffn_norm_g[l])
        if l % 2 == 0:
            i = l // 2
            x = x + swiglu(h, dense_w1[i], dense_w3[i], dense_w2[i])
        else:
            i = l // 2
            x = x + moe_ffn(h, moe_router[i], moe_w1[i], moe_w3[i], moe_w2[i])
    return rmsnorm(x, final_norm_g)
```

```python
import functools

import numpy as np
import jax
import jax.numpy as jnp
from jax import lax
from jax.experimental import pallas as pl
from jax.experimental.pallas import tpu as pltpu

F32 = jnp.float32
BF16 = jnp.bfloat16
EPS = 1e-6
NEG = -1e30

D_MODEL = 1024
M_HEADS = 4
HEAD_DIM = 128
M_WIDTH = M_HEADS * HEAD_DIM
G_GROUPS = 4
G_CH = 128
G_WIDTH = G_GROUPS * G_CH
G_BLOCK = 128
G_CHUNK = 64
CONV_W = 4
N_EXPERTS = 8
LANES = 128
GATE_ROWS = 16

MLSTM_CHUNK = 128
MIX_BLOCK = 512
FFN_BLOCK = 512
FFN_CHUNK = 256
ROUTER_BLOCK = 256
MOE_TILE = 512
MOE_FF_TILE = 512
VMEM_LIMIT = 56 * 1024 * 1024


def _dot(a, b):
    return jnp.dot(a, b, preferred_element_type=F32)


def _dot_nt(a, b):
    return lax.dot_general(a, b, (((1,), (1,)), ((), ())), preferred_element_type=F32)


def _dot_tn(a, b):
    return lax.dot_general(a, b, (((0,), (0,)), ((), ())), preferred_element_type=F32)


def _rms_scale(x):
    return lax.rsqrt(jnp.mean(x * x, axis=-1, keepdims=True) + EPS)


def _sigmoid(x):
    return 1.0 / (1.0 + jnp.exp(-x))


def _silu(x):
    return x * _sigmoid(x)


def _gelu_tanh(x):
    c = np.float32(np.sqrt(2.0 / np.pi))
    return 0.5 * x * (1.0 + jnp.tanh(c * (x + 0.044715 * (x * x * x))))


def _log_sigmoid(x):
    return jnp.minimum(x, 0.0) - jnp.log1p(jnp.exp(-jnp.abs(x)))


def _split3(x):
    hi = x.astype(BF16)
    r1 = x - hi.astype(F32)
    mid = r1.astype(BF16)
    lo = (r1 - mid.astype(F32)).astype(BF16)
    return hi, mid, lo


def _mlstm_chunk(q, k, v, b_col, ig_col, c_row, ct, n, m_prev):
    L = q.shape[0]
    row = lax.broadcasted_iota(jnp.int32, (L, L), 0)
    col = lax.broadcasted_iota(jnp.int32, (L, L), 1)
    d = jnp.where(col <= row, b_col - c_row, NEG)
    inter = b_col + m_prev
    m_t = jnp.maximum(inter, jnp.max(d, axis=1, keepdims=True))
    w = jnp.exp(d - m_t)
    a = jnp.exp(inter - m_t)
    qb = q.astype(BF16)
    kb = k.astype(BF16)
    vb = v.astype(BF16)
    s = _dot_nt(qb, kb) * w
    num = a * _dot(qb, ct.astype(BF16)) + _dot(s.astype(BF16), vb)
    den = a * jnp.sum(q * n, axis=1, keepdims=True) + jnp.sum(s, axis=1, keepdims=True)
    den = jnp.maximum(jnp.abs(den), jnp.exp(-m_t))
    h = num / den
    b_end = b_col[L - 1:L, :]
    m_new = b_end + jnp.maximum(m_prev, jnp.max(-c_row, axis=1, keepdims=True))
    a_prev = jnp.exp(b_end + m_prev - m_new)
    w_s = jnp.exp(b_end - m_new - (b_col - ig_col))
    kw = k * w_s
    ct_new = a_prev * ct + _dot_tn(kw.astype(BF16), vb)
    n_new = a_prev * n + jnp.sum(kw, axis=0, keepdims=True)
    return h, ct_new, n_new, m_new


def _mixer_kernel(x_ref, g_ref, win_ref, convw_ref, convb_ref, wq_ref, wk_ref, wv_ref,
                  wif_ref, wift_ref, bif_ref, bift_ref, tcol_ref, trow_ref,
                  skip_ref, mhg_ref, gvg_ref, wsp_ref, bsp_ref, gog_ref, wout_ref,
                  o_ref,
                  xm_ext, ct_ref, n_ref, m_ref, y_ref, yg_ref):
    sb = x_ref.shape[1]
    L = MLSTM_CHUNK

    @pl.when(pl.program_id(1) == 0)
    def _():
        xm_ext[0:8, :] = jnp.zeros((8, M_WIDTH), F32)
        ct_ref[...] = jnp.zeros(ct_ref.shape, F32)
        n_ref[...] = jnp.zeros(n_ref.shape, F32)
        m_ref[...] = jnp.zeros(m_ref.shape, F32)

    x = x_ref[0]
    h = (x * _rms_scale(x) * g_ref[...]).astype(BF16)
    proj = _dot(h, win_ref[...])
    x_m = proj[:, 0:M_WIDTH]
    o_pre = proj[:, M_WIDTH:2 * M_WIDTH]
    u = proj[:, 2 * M_WIDTH:2 * M_WIDTH + G_WIDTH]
    v = proj[:, 2 * M_WIDTH + G_WIDTH:]

    xm_ext[8:8 + sb, :] = x_m
    acc = jnp.zeros((sb, M_WIDTH), F32) + convb_ref[...]
    for j in range(CONV_W):
        off = 8 - (CONV_W - 1) + j
        acc = acc + convw_ref[j:j + 1, :] * xm_ext[off:off + sb, :]
    x_c = _silu(acc)
    xm_ext[0:8, :] = xm_ext[sb:sb + 8, :]

    xc_b = x_c.astype(BF16)
    xm_b = x_m.astype(BF16)
    qs, ks, vs = [], [], []
    for hd in range(M_HEADS):
        sl = slice(hd * HEAD_DIM, (hd + 1) * HEAD_DIM)
        qs.append(_dot(xc_b[:, sl], wq_ref[hd]))
        ks.append(_dot(xc_b[:, sl], wk_ref[hd]))
        vs.append(_dot(xm_b[:, sl], wv_ref[hd]))
    qkv_b = jnp.concatenate([t.astype(BF16) for t in qs + ks + vs], axis=1)

    gcol = _dot(qkv_b, wif_ref[...]) + bif_ref[...]
    grow = _dot_nt(wift_ref[...], qkv_b) + bift_ref[...]
    lf_col = _log_sigmoid(gcol)
    lf_row = _log_sigmoid(grow)
    ch, cm, cl = _split3(lf_col)
    bcol3 = _dot(tcol_ref[...], jnp.concatenate([ch, cm, cl], axis=1))
    bcol = bcol3[:, 0:LANES] + bcol3[:, LANES:2 * LANES] + bcol3[:, 2 * LANES:]
    rh, rm, rl = _split3(lf_row)
    brow = _dot(rh, trow_ref[...]) + _dot(rm, trow_ref[...]) + _dot(rl, trow_ref[...])

    scale = np.float32(HEAD_DIM ** -0.5)
    o_gate = _sigmoid(o_pre)
    for hd in range(M_HEADS):
        sl = slice(hd * HEAD_DIM, (hd + 1) * HEAD_DIM)
        ct = ct_ref[hd]
        n = n_ref[hd]
        m_prev = m_ref[hd][:, 0:1]
        for j in range(sb // L):
            rs = slice(j * L, (j + 1) * L)
            b_col = bcol[rs, M_HEADS + hd:M_HEADS + hd + 1]
            ig_col = gcol[rs, hd:hd + 1]
            c_row = brow[M_HEADS + hd:M_HEADS + hd + 1, rs] - grow[hd:hd + 1, rs]
            hh, ct, n, m_prev = _mlstm_chunk(qs[hd][rs] * scale, ks[hd][rs], vs[hd][rs],
                                             b_col, ig_col, c_row, ct, n, m_prev)
            hh = o_gate[rs, sl] * hh
            mu = jnp.mean(hh, axis=1, keepdims=True)
            dev = hh - mu
            var = jnp.mean(dev * dev, axis=1, keepdims=True)
            y_m = dev * lax.rsqrt(var + EPS) * mhg_ref[:, sl] + skip_ref[:, sl] * x_c[rs, sl]
            y_ref[rs, sl] = y_m.astype(BF16)
        ct_ref[hd] = ct
        n_ref[hd] = n
        m_ref[hd] = jnp.broadcast_to(m_prev, (1, LANES))

    ug = _gelu_tanh(u)
    vg = _gelu_tanh(v)
    mu = jnp.mean(vg, axis=1, keepdims=True)
    dev = vg - mu
    var = jnp.mean(dev * dev, axis=1, keepdims=True)
    vn = (dev * lax.rsqrt(var + EPS) * gvg_ref[...]).astype(BF16)
    for i in range(sb // G_BLOCK):
        rs = slice(i * G_BLOCK, (i + 1) * G_BLOCK)
        for gi in range(G_GROUPS):
            sl = slice(gi * G_CH, (gi + 1) * G_CH)
            sv = _dot(wsp_ref[gi], vn[rs, sl]) + bsp_ref[:, sl]
            yg_ref[rs, sl] = ug[rs, sl] * sv
    yg = yg_ref[...]
    y_ref[:, M_WIDTH:] = (yg * _rms_scale(yg) * gog_ref[...]).astype(BF16)

    o_ref[0] = x + _dot(y_ref[...], wout_ref[...])


def _const_spec(shape):
    nd = len(shape)
    return pl.BlockSpec(shape, lambda *_: (0,) * nd, pipeline_mode=pl.Buffered(1))


def _chunk_cumsum_matrix(n, chunk):
    i = np.arange(n)[:, None]
    j = np.arange(n)[None, :]
    return ((j <= i) & (i // chunk == j // chunk)).astype(np.float32)


def _mixer(x, g, w_in, conv_w, conv_b, w_q, w_k, w_v, w_if, b_if, skip_m, mh_norm_g,
           gm_v_g, w_sp, b_sp, gm_out_g, w_out):
    B, S, D = x.shape
    sb = min(MIX_BLOCK, S)
    assert S % sb == 0 and sb % MLSTM_CHUNK == 0 and sb % G_BLOCK == 0
    tcol = _chunk_cumsum_matrix(sb, MLSTM_CHUNK)
    wif = jnp.zeros((3 * M_WIDTH, LANES), F32).at[:, :2 * M_HEADS].set(w_if)
    wift = jnp.zeros((GATE_ROWS, 3 * M_WIDTH), F32).at[:2 * M_HEADS, :].set(w_if.T)
    bif = jnp.zeros((1, LANES), F32).at[0, :2 * M_HEADS].set(b_if)
    bift = jnp.zeros((GATE_ROWS, 1), F32).at[:2 * M_HEADS, 0].set(b_if)
    pos = np.arange(G_BLOCK)
    chunk_mask = (pos[:, None] // G_CHUNK) >= (pos[None, :] // G_CHUNK)
    wsp = jnp.where(chunk_mask[None], w_sp, 0.0).astype(BF16)
    bsp = jnp.repeat(b_sp.T, G_CH, axis=1)
    row = lambda a: a.reshape(1, -1).astype(F32)
    args = (x, row(g), w_in.astype(BF16), conv_w.astype(F32), row(conv_b),
            w_q.astype(BF16), w_k.astype(BF16), w_v.astype(BF16),
            wif.astype(BF16), wift.astype(BF16), bif, bift,
            jnp.asarray(tcol, BF16), jnp.asarray(tcol.T, BF16),
            row(skip_m), row(mh_norm_g), row(gm_v_g), wsp, bsp.astype(F32), row(gm_out_g),
            w_out.astype(BF16))
    in_specs = [pl.BlockSpec((1, sb, D), lambda b, s: (b, s, 0))]
    in_specs += [_const_spec(a.shape) for a in args[1:]]
    return pl.pallas_call(
        _mixer_kernel,
        grid=(B, S // sb),
        in_specs=in_specs,
        out_specs=pl.BlockSpec((1, sb, D), lambda b, s: (b, s, 0)),
        out_shape=jax.ShapeDtypeStruct((B, S, D), F32),
        scratch_shapes=[
            pltpu.VMEM((sb + 8, M_WIDTH), F32),
            pltpu.VMEM((M_HEADS, HEAD_DIM, HEAD_DIM), F32),
            pltpu.VMEM((M_HEADS, 1, HEAD_DIM), F32),
            pltpu.VMEM((M_HEADS, 1, LANES), F32),
            pltpu.VMEM((sb, M_WIDTH + G_WIDTH), BF16),
            pltpu.VMEM((sb, G_WIDTH), F32),
        ],
        compiler_params=pltpu.CompilerParams(
            dimension_semantics=("arbitrary", "arbitrary"), vmem_limit_bytes=VMEM_LIMIT),
        name="mixer",
    )(*args)


def _ffn_kernel(x_ref, g_ref, w1_ref, w3_ref, w2_ref, o_ref):
    x = x_ref[...]
    hn = (x * _rms_scale(x) * g_ref[...]).astype(BF16)
    acc = x
    for c in range(w1_ref.shape[1] // FFN_CHUNK):
        sl = slice(c * FFN_CHUNK, (c + 1) * FFN_CHUNK)
        h1 = _dot(hn, w1_ref[:, sl])
        h3 = _dot(hn, w3_ref[:, sl])
        acc = acc + _dot((_silu(h1) * h3).astype(BF16), w2_ref[sl, :])
    o_ref[...] = acc


def _ffn_dense(x2, g, w1, w3, w2):
    T, D = x2.shape
    ff = w1.shape[1]
    ffp = -(-ff // FFN_CHUNK) * FFN_CHUNK
    w1p = jnp.pad(w1, ((0, 0), (0, ffp - ff))).astype(BF16)
    w3p = jnp.pad(w3, ((0, 0), (0, ffp - ff))).astype(BF16)
    w2p = jnp.pad(w2, ((0, ffp - ff), (0, 0))).astype(BF16)
    tm = min(FFN_BLOCK, T)
    assert T % tm == 0
    return pl.pallas_call(
        _ffn_kernel,
        grid=(T // tm,),
        in_specs=[pl.BlockSpec((tm, D), lambda i: (i, 0)),
                  _const_spec((1, D)), _const_spec((D, ffp)), _const_spec((D, ffp)), _const_spec((ffp, D))],
        out_specs=pl.BlockSpec((tm, D), lambda i: (i, 0)),
        out_shape=jax.ShapeDtypeStruct((T, D), F32),
        compiler_params=pltpu.CompilerParams(
            dimension_semantics=("arbitrary",), vmem_limit_bytes=VMEM_LIMIT),
        name="ffn_dense",
    )(x2, g.reshape(1, D).astype(F32), w1p, w3p, w2p)


def _router_kernel(x_ref, g_ref, wrt_ref, tri_ref, hn_ref, rank_ref, gate_ref, cnt_ref, carry_ref):
    @pl.when(pl.program_id(0) == 0)
    def _():
        carry_ref[...] = jnp.zeros(carry_ref.shape, F32)

    x = x_ref[...]
    hn = x * _rms_scale(x) * g_ref[...]
    hn_ref[...] = hn.astype(BF16)
    tb = x.shape[0]
    lane = lax.broadcasted_iota(jnp.int32, (tb, LANES), 1).astype(F32)
    logits = jnp.full((tb, LANES), NEG, F32)
    for e in range(N_EXPERTS):
        col = jnp.sum(hn * wrt_ref[e:e + 1, :], axis=1, keepdims=True)
        logits = jnp.where(lane == e, col, logits)
    m1 = jnp.max(logits, axis=1, keepdims=True)
    i1 = jnp.min(jnp.where(logits == m1, lane, float(LANES)), axis=1, keepdims=True)
    rest = jnp.where(lane == i1, NEG, logits)
    m2 = jnp.max(rest, axis=1, keepdims=True)
    i2 = jnp.min(jnp.where(rest == m2, lane, float(LANES)), axis=1, keepdims=True)
    r = jnp.exp(m2 - m1)
    g1 = 1.0 / (1.0 + r)
    g2 = r * g1
    sel1 = lane == i1
    sel2 = lane == i2
    gates = jnp.where(sel1, g1, jnp.where(sel2, g2, 0.0))
    sel = jnp.where(sel1, 1.0, jnp.where(sel2, 1.0, 0.0))
    before = _dot(tri_ref[...], sel.astype(BF16))
    rank = jnp.where(sel > 0.0, carry_ref[...] + before, -1.0)
    carry = carry_ref[...] + jnp.sum(sel, axis=0, keepdims=True)
    carry_ref[...] = carry
    rank_ref[...] = rank[:, 0:N_EXPERTS].astype(jnp.int32)
    gate_ref[...] = gates[:, 0:N_EXPERTS]
    cnt_ref[0] = carry


def _router(x2, g, w_router):
    T, D = x2.shape
    tb = min(ROUTER_BLOCK, T)
    assert T % tb == 0
    nb = T // tb
    tri = np.tril(np.ones((tb, tb), np.float32), -1)
    return pl.pallas_call(
        _router_kernel,
        grid=(nb,),
        in_specs=[pl.BlockSpec((tb, D), lambda i: (i, 0)),
                  _const_spec((1, D)), _const_spec((N_EXPERTS, D)), _const_spec((tb, tb))],
        out_specs=[pl.BlockSpec((tb, D), lambda i: (i, 0)),
                   pl.BlockSpec((tb, N_EXPERTS), lambda i: (i, 0)),
                   pl.BlockSpec((tb, N_EXPERTS), lambda i: (i, 0)),
                   pl.BlockSpec((1, 1, LANES), lambda i: (i, 0, 0))],
        out_shape=[jax.ShapeDtypeStruct((T, D), BF16),
                   jax.ShapeDtypeStruct((T, N_EXPERTS), jnp.int32),
                   jax.ShapeDtypeStruct((T, N_EXPERTS), F32),
                   jax.ShapeDtypeStruct((nb, 1, LANES), F32)],
        scratch_shapes=[pltpu.VMEM((1, LANES), F32)],
        compiler_params=pltpu.CompilerParams(
            dimension_semantics=("arbitrary",), vmem_limit_bytes=VMEM_LIMIT),
        name="router",
    )(x2, g.reshape(1, D).astype(F32), w_router.T.astype(F32), jnp.asarray(tri, BF16))


def _moe_kernel(te_ref, nused_ref, x_ref, gate_ref, w1_ref, w3_ref, w2_ref, o_ref, acc_ref):
    i = pl.program_id(0)
    f = pl.program_id(1)
    used = i < nused_ref[0]

    @pl.when(used)
    def _():
        x = x_ref[...]
        h1 = _dot(x, w1_ref[0])
        h3 = _dot(x, w3_ref[0])
        part = _dot((_silu(h1) * h3).astype(BF16), w2_ref[0])

        @pl.when(f == 0)
        def _():
            acc_ref[...] = part

        @pl.when(f > 0)
        def _():
            acc_ref[...] += part

    @pl.when(f == pl.num_programs(1) - 1)
    def _():
        @pl.when(used)
        def _():
            o_ref[...] = (acc_ref[...] * gate_ref[...]).astype(BF16)

        @pl.when(jnp.logical_not(used))
        def _():
            o_ref[...] = jnp.zeros(o_ref.shape, BF16)


def _moe_grouped(xs, slot_gate, tile_expert, n_used, w1, w3, w2):
    nslot, D = xs.shape
    E, _, ff = w1.shape
    tm, tf = MOE_TILE, MOE_FF_TILE
    assert nslot % tm == 0 and ff % tf == 0
    nt, nf = nslot // tm, ff // tf

    def fsel(i, f, nu):
        return jnp.where(i < nu[0], f, nf - 1)

    grid_spec = pltpu.PrefetchScalarGridSpec(
        num_scalar_prefetch=2,
        grid=(nt, nf),
        in_specs=[pl.BlockSpec((tm, D), lambda i, f, te, nu: (i, 0)),
                  pl.BlockSpec((tm, 1), lambda i, f, te, nu: (i, 0)),
                  pl.BlockSpec((1, D, tf), lambda i, f, te, nu: (te[i], 0, fsel(i, f, nu))),
                  pl.BlockSpec((1, D, tf), lambda i, f, te, nu: (te[i], 0, fsel(i, f, nu))),
                  pl.BlockSpec((1, tf, D), lambda i, f, te, nu: (te[i], fsel(i, f, nu), 0))],
        out_specs=pl.BlockSpec((tm, D), lambda i, f, te, nu: (i, 0)),
        scratch_shapes=[pltpu.VMEM((tm, D), F32)],
    )
    return pl.pallas_call(
        _moe_kernel,
        grid_spec=grid_spec,
        out_shape=jax.ShapeDtypeStruct((nslot, D), BF16),
        compiler_params=pltpu.CompilerParams(
            dimension_semantics=("arbitrary", "arbitrary"), vmem_limit_bytes=VMEM_LIMIT),
        name="moe_ffn",
    )(tile_expert, n_used, xs, slot_gate, w1, w3, w2)


def _combine_kernel(x_ref, ya_ref, yb_ref, g_ref, o_ref):
    x = x_ref[...] + ya_ref[...].astype(F32) + yb_ref[...].astype(F32)
    o_ref[...] = x * _rms_scale(x) * g_ref[...]


def _combine(x2, ya, yb, g):
    T, D = x2.shape
    tm = min(1024, T)
    spec = pl.BlockSpec((tm, D), lambda i: (i, 0))
    return pl.pallas_call(
        _combine_kernel,
        grid=(T // tm,),
        in_specs=[spec, spec, spec, _const_spec((1, D))],
        out_specs=spec,
        out_shape=jax.ShapeDtypeStruct((T, D), F32),
        compiler_params=pltpu.CompilerParams(dimension_semantics=("arbitrary",)),
        name="combine",
    )(x2, ya, yb, g.reshape(1, D).astype(F32))


def _moe_layer(x2, g, w_router, w1, w3, w2, final_g):
    T, D = x2.shape
    tm = MOE_TILE
    hn, rank, gate, cnt = _router(x2, g, w_router)
    counts = cnt[-1, 0, :N_EXPERTS].astype(jnp.int32)
    padded = (counts + tm - 1) // tm * tm
    start = jnp.cumsum(padded) - padded
    nt = (2 * T) // tm + N_EXPERTS + 1
    nslot = nt * tm
    n_used = (jnp.sum(padded) // tm).astype(jnp.int32)
    tile_lo = jnp.arange(nt, dtype=jnp.int32) * tm
    tile_expert = jnp.sum((tile_lo[:, None] >= (start + padded)[None, :]).astype(jnp.int32), axis=1)
    tile_expert = jnp.minimum(tile_expert, N_EXPERTS - 1).astype(jnp.int32)
    pos = jnp.where(rank >= 0, start[None, :] + rank, nslot)
    tok = jnp.broadcast_to(jnp.arange(T, dtype=jnp.int32)[:, None], pos.shape)
    src_tok = jnp.zeros((nslot,), jnp.int32).at[pos.ravel()].set(tok.ravel(), mode="drop")
    slot_gate = jnp.zeros((nslot,), F32).at[pos.ravel()].set(gate.ravel(), mode="drop")
    xs = jnp.take(hn, src_tok, axis=0)
    ys = _moe_grouped(xs, slot_gate[:, None], tile_expert, n_used.reshape(1),
                      w1.astype(BF16), w3.astype(BF16), w2.astype(BF16))
    two = jnp.sort(pos, axis=1)[:, :2]
    ya = jnp.take(ys, two[:, 0], axis=0)
    yb = jnp.take(ys, two[:, 1], axis=0)
    return _combine(x2, ya, yb, final_g)


def kernel(x, mix_norm_g, w_in, conv_w, conv_b, w_q, w_k, w_v, w_if, b_if, skip_m, mh_norm_g, gm_v_g, w_sp, b_sp, gm_out_g, w_out, ffn_norm_g, dense_w1, dense_w3, dense_w2, moe_router, moe_w1, moe_w3, moe_w2, final_norm_g):
    B, S, D = x.shape
    depth = w_in.shape[0]
    assert depth == 2 and dense_w1.shape[0] == 1 and moe_w1.shape[0] == 1
    for l in range(depth):
        x = _mixer(x, mix_norm_g[l], w_in[l], conv_w[l], conv_b[l], w_q[l], w_k[l], w_v[l],
                   w_if[l], b_if[l], skip_m[l], mh_norm_g[l], gm_v_g[l], w_sp[l], b_sp[l],
                   gm_out_g[l], w_out[l])
        x2 = x.reshape(B * S, D)
        if l % 2 == 0:
            x = _ffn_dense(x2, ffn_norm_g[l], dense_w1[l // 2], dense_w3[l // 2],
                           dense_w2[l // 2]).reshape(B, S, D)
        else:
            x = _moe_layer(x2, ffn_norm_g[l], moe_router[l // 2], moe_w1[l // 2], moe_w3[l // 2],
                           moe_w2[l // 2], final_norm_g).reshape(B, S, D)
    return x
```

```python
import functools

import numpy as np
import jax
import jax.numpy as jnp
from jax import lax
from jax.experimental import pallas as pl
from jax.experimental.pallas import tpu as pltpu

F32 = jnp.float32
BF16 = jnp.bfloat16
EPS = 1e-6
NEG = -1e30

D_MODEL = 1024
M_HEADS = 4
HEAD_DIM = 128
M_WIDTH = M_HEADS * HEAD_DIM
G_GROUPS = 4
G_CH = 128
G_WIDTH = G_GROUPS * G_CH
G_BLOCK = 128
G_CHUNK = 64
CONV_W = 4
N_EXPERTS = 8
LANES = 128
GATE_ROWS = 16

MLSTM_CHUNK = 128
MIX_BLOCK = 512
FFN_BLOCK = 512
FFN_CHUNK = 256
ROUTER_BLOCK = 256
MOE_TILE = 512
MOE_FF_TILE = 1792
MOE_FF_CHUNK = 256
MOE_ROW_BLOCK = 128
MOE_WIN_BUFS = 4
COMBINE_WIN = 128
VMEM_LIMIT = 56 * 1024 * 1024


def _dot(a, b):
    return jnp.dot(a, b, preferred_element_type=F32)


def _dot_nt(a, b):
    return lax.dot_general(a, b, (((1,), (1,)), ((), ())), preferred_element_type=F32)


def _dot_tn(a, b):
    return lax.dot_general(a, b, (((0,), (0,)), ((), ())), preferred_element_type=F32)


def _rms_scale(x):
    return lax.rsqrt(jnp.mean(x * x, axis=-1, keepdims=True) + EPS)


def _sigmoid(x):
    return 1.0 / (1.0 + jnp.exp(-x))


def _silu(x):
    return x * _sigmoid(x)


def _gelu_tanh(x):
    c = np.float32(np.sqrt(2.0 / np.pi))
    return 0.5 * x * (1.0 + jnp.tanh(c * (x + 0.044715 * (x * x * x))))


def _log_sigmoid(x):
    return jnp.minimum(x, 0.0) - jnp.log1p(jnp.exp(-jnp.abs(x)))


def _split3(x):
    hi = x.astype(BF16)
    r1 = x - hi.astype(F32)
    mid = r1.astype(BF16)
    lo = (r1 - mid.astype(F32)).astype(BF16)
    return hi, mid, lo


def _mlstm_chunk(q, k, v, b_col, ig_col, c_row, ct, n, m_prev):
    L = q.shape[0]
    row = lax.broadcasted_iota(jnp.int32, (L, L), 0)
    col = lax.broadcasted_iota(jnp.int32, (L, L), 1)
    d = jnp.where(col <= row, b_col - c_row, NEG)
    inter = b_col + m_prev
    m_t = jnp.maximum(inter, jnp.max(d, axis=1, keepdims=True))
    w = jnp.exp(d - m_t)
    a = jnp.exp(inter - m_t)
    qb = q.astype(BF16)
    kb = k.astype(BF16)
    vb = v.astype(BF16)
    s = _dot_nt(qb, kb) * w
    num = a * _dot(qb, ct.astype(BF16)) + _dot(s.astype(BF16), vb)
    den = a * jnp.sum(q * n, axis=1, keepdims=True) + jnp.sum(s, axis=1, keepdims=True)
    den = jnp.maximum(jnp.abs(den), jnp.exp(-m_t))
    h = num / den
    b_end = b_col[L - 1:L, :]
    m_new = b_end + jnp.maximum(m_prev, jnp.max(-c_row, axis=1, keepdims=True))
    a_prev = jnp.exp(b_end + m_prev - m_new)
    w_s = jnp.exp(b_end - m_new - (b_col - ig_col))
    kw = k * w_s
    ct_new = a_prev * ct + _dot_tn(kw.astype(BF16), vb)
    n_new = a_prev * n + jnp.sum(kw, axis=0, keepdims=True)
    return h, ct_new, n_new, m_new


def _mixer_kernel(x_ref, g_ref, win_ref, convw_ref, convb_ref, wq_ref, wk_ref, wv_ref,
                  wif_ref, wift_ref, bif_ref, bift_ref, tcol_ref, trow_ref,
                  skip_ref, mhg_ref, gvg_ref, wsp_ref, bsp_ref, gog_ref, wout_ref,
                  o_ref,
                  xm_ext, ct_ref, n_ref, m_ref, y_ref, yg_ref):
    sb = x_ref.shape[1]
    L = MLSTM_CHUNK

    @pl.when(pl.program_id(1) == 0)
    def _():
        xm_ext[0:8, :] = jnp.zeros((8, M_WIDTH), F32)
        ct_ref[...] = jnp.zeros(ct_ref.shape, F32)
        n_ref[...] = jnp.zeros(n_ref.shape, F32)
        m_ref[...] = jnp.zeros(m_ref.shape, F32)

    x = x_ref[0]
    h = (x * _rms_scale(x) * g_ref[...]).astype(BF16)
    proj = _dot(h, win_ref[...])
    x_m = proj[:, 0:M_WIDTH]
    o_pre = proj[:, M_WIDTH:2 * M_WIDTH]
    u = proj[:, 2 * M_WIDTH:2 * M_WIDTH + G_WIDTH]
    v = proj[:, 2 * M_WIDTH + G_WIDTH:]

    xm_ext[8:8 + sb, :] = x_m
    acc = jnp.zeros((sb, M_WIDTH), F32) + convb_ref[...]
    for j in range(CONV_W):
        off = 8 - (CONV_W - 1) + j
        acc = acc + convw_ref[j:j + 1, :] * xm_ext[off:off + sb, :]
    x_c = _silu(acc)
    xm_ext[0:8, :] = xm_ext[sb:sb + 8, :]

    xc_b = x_c.astype(BF16)
    xm_b = x_m.astype(BF16)
    qs, ks, vs = [], [], []
    for hd in range(M_HEADS):
        sl = slice(hd * HEAD_DIM, (hd + 1) * HEAD_DIM)
        qs.append(_dot(xc_b[:, sl], wq_ref[hd]))
        ks.append(_dot(xc_b[:, sl], wk_ref[hd]))
        vs.append(_dot(xm_b[:, sl], wv_ref[hd]))
    qkv_b = jnp.concatenate([t.astype(BF16) for t in qs + ks + vs], axis=1)

    gcol = _dot(qkv_b, wif_ref[...]) + bif_ref[...]
    grow = _dot_nt(wift_ref[...], qkv_b) + bift_ref[...]
    lf_col = _log_sigmoid(gcol)
    lf_row = _log_sigmoid(grow)
    ch, cm, cl = _split3(lf_col)
    bcol3 = _dot(tcol_ref[...], jnp.concatenate([ch, cm, cl], axis=1))
    bcol = bcol3[:, 0:LANES] + bcol3[:, LANES:2 * LANES] + bcol3[:, 2 * LANES:]
    rh, rm, rl = _split3(lf_row)
    brow = _dot(rh, trow_ref[...]) + _dot(rm, trow_ref[...]) + _dot(rl, trow_ref[...])

    scale = np.float32(HEAD_DIM ** -0.5)
    o_gate = _sigmoid(o_pre)
    for hd in range(M_HEADS):
        sl = slice(hd * HEAD_DIM, (hd + 1) * HEAD_DIM)
        ct = ct_ref[hd]
        n = n_ref[hd]
        m_prev = m_ref[hd][:, 0:1]
        for j in range(sb // L):
            rs = slice(j * L, (j + 1) * L)
            b_col = bcol[rs, M_HEADS + hd:M_HEADS + hd + 1]
            ig_col = gcol[rs, hd:hd + 1]
            c_row = brow[M_HEADS + hd:M_HEADS + hd + 1, rs] - grow[hd:hd + 1, rs]
            hh, ct, n, m_prev = _mlstm_chunk(qs[hd][rs] * scale, ks[hd][rs], vs[hd][rs],
                                             b_col, ig_col, c_row, ct, n, m_prev)
            hh = o_gate[rs, sl] * hh
            mu = jnp.mean(hh, axis=1, keepdims=True)
            dev = hh - mu
            var = jnp.mean(dev * dev, axis=1, keepdims=True)
            y_m = dev * lax.rsqrt(var + EPS) * mhg_ref[:, sl] + skip_ref[:, sl] * x_c[rs, sl]
            y_ref[rs, sl] = y_m.astype(BF16)
        ct_ref[hd] = ct
        n_ref[hd] = n
        m_ref[hd] = jnp.broadcast_to(m_prev, (1, LANES))

    ug = _gelu_tanh(u)
    vg = _gelu_tanh(v)
    mu = jnp.mean(vg, axis=1, keepdims=True)
    dev = vg - mu
    var = jnp.mean(dev * dev, axis=1, keepdims=True)
    vn = (dev * lax.rsqrt(var + EPS) * gvg_ref[...]).astype(BF16)
    for i in range(sb // G_BLOCK):
        rs = slice(i * G_BLOCK, (i + 1) * G_BLOCK)
        for gi in range(G_GROUPS):
            sl = slice(gi * G_CH, (gi + 1) * G_CH)
            sv = _dot(wsp_ref[gi], vn[rs, sl]) + bsp_ref[:, sl]
            yg_ref[rs, sl] = ug[rs, sl] * sv
    yg = yg_ref[...]
    y_ref[:, M_WIDTH:] = (yg * _rms_scale(yg) * gog_ref[...]).astype(BF16)

    o_ref[0] = x + _dot(y_ref[...], wout_ref[...])


def _const_spec(shape):
    nd = len(shape)
    return pl.BlockSpec(shape, lambda *_: (0,) * nd, pipeline_mode=pl.Buffered(1))


def _chunk_cumsum_matrix(n, chunk):
    i = np.arange(n)[:, None]
    j = np.arange(n)[None, :]
    return ((j <= i) & (i // chunk == j // chunk)).astype(np.float32)


def _mixer(x, g, w_in, conv_w, conv_b, w_q, w_k, w_v, w_if, b_if, skip_m, mh_norm_g,
           gm_v_g, w_sp, b_sp, gm_out_g, w_out):
    B, S, D = x.shape
    sb = min(MIX_BLOCK, S)
    assert S % sb == 0 and sb % MLSTM_CHUNK == 0 and sb % G_BLOCK == 0
    tcol = _chunk_cumsum_matrix(sb, MLSTM_CHUNK)
    wif = jnp.zeros((3 * M_WIDTH, LANES), F32).at[:, :2 * M_HEADS].set(w_if)
    wift = jnp.zeros((GATE_ROWS, 3 * M_WIDTH), F32).at[:2 * M_HEADS, :].set(w_if.T)
    bif = jnp.zeros((1, LANES), F32).at[0, :2 * M_HEADS].set(b_if)
    bift = jnp.zeros((GATE_ROWS, 1), F32).at[:2 * M_HEADS, 0].set(b_if)
    pos = np.arange(G_BLOCK)
    chunk_mask = (pos[:, None] // G_CHUNK) >= (pos[None, :] // G_CHUNK)
    wsp = jnp.where(chunk_mask[None], w_sp, 0.0).astype(BF16)
    bsp = jnp.repeat(b_sp.T, G_CH, axis=1)
    row = lambda a: a.reshape(1, -1).astype(F32)
    args = (x, row(g), w_in.astype(BF16), conv_w.astype(F32), row(conv_b),
            w_q.astype(BF16), w_k.astype(BF16), w_v.astype(BF16),
            wif.astype(BF16), wift.astype(BF16), bif, bift,
            jnp.asarray(tcol, BF16), jnp.asarray(tcol.T, BF16),
            row(skip_m), row(mh_norm_g), row(gm_v_g), wsp, bsp.astype(F32), row(gm_out_g),
            w_out.astype(BF16))
    in_specs = [pl.BlockSpec((1, sb, D), lambda b, s: (b, s, 0))]
    in_specs += [_const_spec(a.shape) for a in args[1:]]
    return pl.pallas_call(
        _mixer_kernel,
        grid=(B, S // sb),
        in_specs=in_specs,
        out_specs=pl.BlockSpec((1, sb, D), lambda b, s: (b, s, 0)),
        out_shape=jax.ShapeDtypeStruct((B, S, D), F32),
        scratch_shapes=[
            pltpu.VMEM((sb + 8, M_WIDTH), F32),
            pltpu.VMEM((M_HEADS, HEAD_DIM, HEAD_DIM), F32),
            pltpu.VMEM((M_HEADS, 1, HEAD_DIM), F32),
            pltpu.VMEM((M_HEADS, 1, LANES), F32),
            pltpu.VMEM((sb, M_WIDTH + G_WIDTH), BF16),
            pltpu.VMEM((sb, G_WIDTH), F32),
        ],
        compiler_params=pltpu.CompilerParams(
            dimension_semantics=("arbitrary", "arbitrary"), vmem_limit_bytes=VMEM_LIMIT),
        name="mixer",
    )(*args)


def _ffn_kernel(x_ref, g_ref, w1_ref, w3_ref, w2_ref, o_ref):
    x = x_ref[...]
    hn = (x * _rms_scale(x) * g_ref[...]).astype(BF16)
    acc = x
    for c in range(w1_ref.shape[1] // FFN_CHUNK):
        sl = slice(c * FFN_CHUNK, (c + 1) * FFN_CHUNK)
        h1 = _dot(hn, w1_ref[:, sl])
        h3 = _dot(hn, w3_ref[:, sl])
        acc = acc + _dot((_silu(h1) * h3).astype(BF16), w2_ref[sl, :])
    o_ref[...] = acc


def _ffn_dense(x2, g, w1, w3, w2):
    T, D = x2.shape
    ff = w1.shape[1]
    ffp = -(-ff // FFN_CHUNK) * FFN_CHUNK
    w1p = jnp.pad(w1, ((0, 0), (0, ffp - ff))).astype(BF16)
    w3p = jnp.pad(w3, ((0, 0), (0, ffp - ff))).astype(BF16)
    w2p = jnp.pad(w2, ((0, ffp - ff), (0, 0))).astype(BF16)
    tm = min(FFN_BLOCK, T)
    assert T % tm == 0
    return pl.pallas_call(
        _ffn_kernel,
        grid=(T // tm,),
        in_specs=[pl.BlockSpec((tm, D), lambda i: (i, 0)),
                  _const_spec((1, D)), _const_spec((D, ffp)), _const_spec((D, ffp)), _const_spec((ffp, D))],
        out_specs=pl.BlockSpec((tm, D), lambda i: (i, 0)),
        out_shape=jax.ShapeDtypeStruct((T, D), F32),
        compiler_params=pltpu.CompilerParams(
            dimension_semantics=("arbitrary",), vmem_limit_bytes=VMEM_LIMIT),
        name="ffn_dense",
    )(x2, g.reshape(1, D).astype(F32), w1p, w3p, w2p)


def _router_kernel(x_ref, g_ref, wrt_ref, tri_ref, hn_ref, rank_ref, rankt_ref, gatet_ref, cnt_ref,
                   carry_ref):
    @pl.when(pl.program_id(0) == 0)
    def _():
        carry_ref[...] = jnp.zeros(carry_ref.shape, F32)

    x = x_ref[...]
    hn = x * _rms_scale(x) * g_ref[...]
    hn_ref[...] = hn.astype(BF16)
    tb = x.shape[0]
    lane = lax.broadcasted_iota(jnp.int32, (tb, LANES), 1).astype(F32)
    logits = jnp.full((tb, LANES), NEG, F32)
    for e in range(N_EXPERTS):
        col = jnp.sum(hn * wrt_ref[e:e + 1, :], axis=1, keepdims=True)
        logits = jnp.where(lane == e, col, logits)
    m1 = jnp.max(logits, axis=1, keepdims=True)
    i1 = jnp.min(jnp.where(logits == m1, lane, float(LANES)), axis=1, keepdims=True)
    rest = jnp.where(lane == i1, NEG, logits)
    m2 = jnp.max(rest, axis=1, keepdims=True)
    i2 = jnp.min(jnp.where(rest == m2, lane, float(LANES)), axis=1, keepdims=True)
    r = jnp.exp(m2 - m1)
    g1 = 1.0 / (1.0 + r)
    g2 = r * g1
    sel1 = lane == i1
    sel2 = lane == i2
    gates = jnp.where(sel1, g1, jnp.where(sel2, g2, 0.0))
    sel = jnp.where(sel1, 1.0, jnp.where(sel2, 1.0, 0.0))
    before = _dot(tri_ref[...], sel.astype(BF16))
    rank = jnp.where(sel > 0.0, carry_ref[...] + before, -1.0)
    carry = carry_ref[...] + jnp.sum(sel, axis=0, keepdims=True)
    carry_ref[...] = carry
    rank_ref[...] = rank[:, 0:N_EXPERTS].astype(jnp.int32)
    rankt_ref[0] = rank.T[0:N_EXPERTS, :]
    gatet_ref[0] = gates.T[0:N_EXPERTS, :]
    cnt_ref[0] = carry


def _router(x2, g, w_router):
    T, D = x2.shape
    tb = min(ROUTER_BLOCK, T)
    assert T % tb == 0
    nb = T // tb
    tri = np.tril(np.ones((tb, tb), np.float32), -1)
    return pl.pallas_call(
        _router_kernel,
        grid=(nb,),
        in_specs=[pl.BlockSpec((tb, D), lambda i: (i, 0)),
                  _const_spec((1, D)), _const_spec((N_EXPERTS, D)), _const_spec((tb, tb))],
        out_specs=[pl.BlockSpec((tb, D), lambda i: (i, 0)),
                   pl.BlockSpec((tb, N_EXPERTS), lambda i: (i, 0)),
                   pl.BlockSpec((1, N_EXPERTS, tb), lambda i: (i, 0, 0)),
                   pl.BlockSpec((1, N_EXPERTS, tb), lambda i: (i, 0, 0)),
                   pl.BlockSpec((1, 1, LANES), lambda i: (i, 0, 0))],
        out_shape=[jax.ShapeDtypeStruct((T, D), BF16),
                   jax.ShapeDtypeStruct((T, N_EXPERTS), jnp.int32),
                   jax.ShapeDtypeStruct((nb, N_EXPERTS, tb), F32),
                   jax.ShapeDtypeStruct((nb, N_EXPERTS, tb), F32),
                   jax.ShapeDtypeStruct((nb, 1, LANES), F32)],
        scratch_shapes=[pltpu.VMEM((1, LANES), F32)],
        compiler_params=pltpu.CompilerParams(
            dimension_semantics=("arbitrary",), vmem_limit_bytes=VMEM_LIMIT),
        name="router",
    )(x2, g.reshape(1, D).astype(F32), w_router.T.astype(F32), jnp.asarray(tri, BF16))


def _moe_kernel(te_ref, nused_ref, wlo_ref, nwin_ref, rbase_ref, cb_ref,
                hn_hbm, rankt_ref, gatet_ref, w1_ref, w3_ref, w2_ref,
                o_ref,
                hbuf, sem, xacc, gacc, xs_ref, acc_ref):
    i = pl.program_id(0)
    f = pl.program_id(1)
    used = i < nused_ref[0]
    tm = o_ref.shape[0]
    gw = hbuf.shape[1]
    nbuf = hbuf.shape[0]
    rblk = MOE_ROW_BLOCK

    def window_copy(tile, k, slot):
        w = wlo_ref[tile] + k
        return pltpu.make_async_copy(hn_hbm.at[pl.ds(pl.multiple_of(w * gw, gw), gw), :],
                                     hbuf.at[slot], sem.at[slot])

    def start_head(tile):
        for k in range(nbuf - 1):
            @pl.when(k < nwin_ref[tile])
            def _():
                window_copy(tile, k, k).start()

    @pl.when(jnp.logical_and(used, f == 0))
    def _dispatch():
        @pl.when(i == 0)
        def _():
            start_head(0)

        e = te_ref[i]
        rb = rbase_ref[i]
        nwin = nwin_ref[i]
        xacc[...] = jnp.zeros(xacc.shape, F32)
        gacc[...] = jnp.zeros(gacc.shape, F32)

        def win_body(k, carry):
            slot = k % nbuf
            window_copy(i, k, slot).wait()

            @pl.when(k + nbuf - 1 < nwin)
            def _():
                window_copy(i, k + nbuf - 1, (k + nbuf - 1) % nbuf).start()

            w = wlo_ref[i] + k
            ra = jnp.maximum(cb_ref[w * N_EXPERTS + e] - rb, 0)
            re = jnp.minimum(cb_ref[(w + 1) * N_EXPERTS + e] - rb, tm)
            r8 = (ra // 8) * 8
            nblk = jnp.where(re > ra, (re - r8 + rblk - 1) // rblk, 0)
            rel = rankt_ref[w, pl.ds(e, 1), :] - rb.astype(F32)
            gate_row = gatet_ref[w, pl.ds(e, 1), :]

            def blk_body(jb, c):
                st = pl.multiple_of(r8 + jb * rblk, 8)
                rows = lax.broadcasted_iota(jnp.int32, (rblk, gw), 0).astype(F32) + st.astype(F32)
                hit = rel == rows
                onehot = jnp.where(hit, 1.0, 0.0).astype(BF16)
                xacc[pl.ds(st, rblk), :] += _dot(onehot, hbuf[slot])
                gsum = jnp.sum(jnp.where(hit, gate_row, 0.0), axis=1, keepdims=True)
                gacc[pl.ds(st, rblk), :] += jnp.broadcast_to(gsum, (rblk, LANES))
                return c

            lax.fori_loop(0, nblk, blk_body, 0)
            return carry

        lax.fori_loop(0, nwin, win_body, 0)
        xs_ref[...] = xacc[0:tm, :].astype(BF16)

        @pl.when(i + 1 < nused_ref[0])
        def _():
            start_head(i + 1)

    @pl.when(used)
    def _ffn():
        x = xs_ref[...]
        part = None
        for c in range(w1_ref.shape[2] // MOE_FF_CHUNK):
            sl = slice(c * MOE_FF_CHUNK, (c + 1) * MOE_FF_CHUNK)
            h1 = _dot(x, w1_ref[0, :, sl])
            h3 = _dot(x, w3_ref[0, :, sl])
            p = _dot((_silu(h1) * h3).astype(BF16), w2_ref[0, sl, :])
            part = p if part is None else part + p

        @pl.when(f == 0)
        def _():
            acc_ref[...] = part

        @pl.when(f > 0)
        def _():
            acc_ref[...] += part

    @pl.when(f == pl.num_programs(1) - 1)
    def _():
        @pl.when(used)
        def _():
            o_ref[...] = (acc_ref[...] * gacc[0:tm, 0:1]).astype(BF16)

        @pl.when(jnp.logical_not(used))
        def _():
            o_ref[...] = jnp.zeros(o_ref.shape, BF16)


def _moe_grouped(hn, rankt, gatet, tile_expert, n_used, wlo, nwin, rbase, cb_flat, nt, w1, w3, w2):
    T, D = hn.shape
    nb, _, gw = rankt.shape
    E, _, ff = w1.shape
    tm, tf = MOE_TILE, MOE_FF_TILE
    assert ff % tf == 0 and tf % MOE_FF_CHUNK == 0 and tm % MOE_ROW_BLOCK == 0
    nf = ff // tf

    def fsel(i, f, nu):
        return jnp.where(i < nu[0], f, nf - 1)

    grid_spec = pltpu.PrefetchScalarGridSpec(
        num_scalar_prefetch=6,
        grid=(nt, nf),
        in_specs=[pl.BlockSpec(memory_space=pl.ANY),
                  pl.BlockSpec((nb, N_EXPERTS, gw), lambda i, f, *_: (0, 0, 0), pipeline_mode=pl.Buffered(1)),
                  pl.BlockSpec((nb, N_EXPERTS, gw), lambda i, f, *_: (0, 0, 0), pipeline_mode=pl.Buffered(1)),
                  pl.BlockSpec((1, D, tf), lambda i, f, te, nu, *_: (te[i], 0, fsel(i, f, nu))),
                  pl.BlockSpec((1, D, tf), lambda i, f, te, nu, *_: (te[i], 0, fsel(i, f, nu))),
                  pl.BlockSpec((1, tf, D), lambda i, f, te, nu, *_: (te[i], fsel(i, f, nu), 0))],
        out_specs=pl.BlockSpec((tm, D), lambda i, f, *_: (i, 0)),
        scratch_shapes=[pltpu.VMEM((MOE_WIN_BUFS, gw, D), BF16),
                        pltpu.SemaphoreType.DMA((MOE_WIN_BUFS,)),
                        pltpu.VMEM((tm + MOE_ROW_BLOCK, D), F32),
                        pltpu.VMEM((tm + MOE_ROW_BLOCK, LANES), F32),
                        pltpu.VMEM((tm, D), BF16),
                        pltpu.VMEM((tm, D), F32)],
    )
    return pl.pallas_call(
        _moe_kernel,
        grid_spec=grid_spec,
        out_shape=jax.ShapeDtypeStruct((nt * tm, D), BF16),
        compiler_params=pltpu.CompilerParams(
            dimension_semantics=("arbitrary", "arbitrary"), vmem_limit_bytes=VMEM_LIMIT),
        name="moe_ffn",
    )(tile_expert, n_used, wlo, nwin, rbase, cb_flat, hn, rankt, gatet, w1, w3, w2)


def _combine_kernel(ws_ref, nblk_ref, nround_ref,
                    x_ref, pos_ref, g_ref, ys_hbm, o_ref, ybuf, sem):
    j = pl.program_id(0)
    nj = pl.num_programs(0)
    tt = x_ref.shape[0]
    wc = COMBINE_WIN
    extra_slot = 2

    def window_copies(tile, rnd, slot):
        cps = []
        for e in range(N_EXPERTS):
            t = tile * N_EXPERTS + e
            s = ws_ref[t] + jnp.where(rnd < nblk_ref[t], rnd, 0) * wc
            cps.append(pltpu.make_async_copy(ys_hbm.at[pl.ds(pl.multiple_of(s, 16), wc), :],
                                             ybuf.at[slot, pl.ds(e * wc, wc), :], sem.at[slot]))
        return cps

    def select(rnd, slot):
        pos = pos_ref[...]
        lane = lax.broadcasted_iota(jnp.int32, (tt, wc), 1)
        parts = []
        for e in range(N_EXPERTS):
            t = j * N_EXPERTS + e
            base = jnp.where(rnd < nblk_ref[t], ws_ref[t] + rnd * wc, -2 * wc)
            parts.append(jnp.where(pos[:, e:e + 1] == base + lane, 1.0, 0.0).astype(BF16))
        return _dot(jnp.concatenate(parts, axis=1), ybuf[slot])

    @pl.when(j == 0)
    def _():
        for cp in window_copies(0, 0, 0):
            cp.start()

    @pl.when(j + 1 < nj)
    def _():
        for cp in window_copies(j + 1, 0, (j + 1) % 2):
            cp.start()

    for cp in window_copies(j, 0, j % 2):
        cp.wait()
    acc = select(0, j % 2)

    def round_body(rnd, acc):
        for cp in window_copies(j, rnd, extra_slot):
            cp.start()
        for cp in window_copies(j, rnd, extra_slot):
            cp.wait()
        return acc + select(rnd, extra_slot)

    acc = lax.fori_loop(1, nround_ref[j], round_body, acc)
    x = x_ref[...] + acc
    o_ref[...] = x * _rms_scale(x) * g_ref[...]


def _combine(x2, pos, ys, ws, nblk, nround, g, tt):
    T, D = x2.shape
    grid_spec = pltpu.PrefetchScalarGridSpec(
        num_scalar_prefetch=3,
        grid=(T // tt,),
        in_specs=[pl.BlockSpec((tt, D), lambda j, *_: (j, 0)),
                  pl.BlockSpec((tt, N_EXPERTS), lambda j, *_: (j, 0)),
                  pl.BlockSpec((1, D), lambda j, *_: (0, 0)),
                  pl.BlockSpec(memory_space=pl.ANY)],
        out_specs=pl.BlockSpec((tt, D), lambda j, *_: (j, 0)),
        scratch_shapes=[pltpu.VMEM((3, N_EXPERTS * COMBINE_WIN, D), BF16),
                        pltpu.SemaphoreType.DMA((3,))],
    )
    return pl.pallas_call(
        _combine_kernel,
        grid_spec=grid_spec,
        out_shape=jax.ShapeDtypeStruct((T, D), F32),
        compiler_params=pltpu.CompilerParams(
            dimension_semantics=("arbitrary",), vmem_limit_bytes=VMEM_LIMIT),
        name="combine",
    )(ws, nblk, nround, x2, pos, g.reshape(1, D).astype(F32), ys)


def _moe_layer(x2, g, w_router, w1, w3, w2, final_g):
    T, D = x2.shape
    tm = MOE_TILE
    hn, rank, rankt, gatet, cnt = _router(x2, g, w_router)
    nb, _, gw = rankt.shape
    cb = jnp.concatenate([jnp.zeros((1, N_EXPERTS), jnp.int32),
                          cnt[:, 0, :N_EXPERTS].astype(jnp.int32)], axis=0)
    counts = cb[-1]
    padded = (counts + tm - 1) // tm * tm
    start = jnp.cumsum(padded) - padded
    nt = (2 * T) // tm + N_EXPERTS + 1
    n_used = (jnp.sum(padded) // tm).astype(jnp.int32)
    tile_lo = jnp.arange(nt, dtype=jnp.int32) * tm
    te = jnp.sum((tile_lo[:, None] >= (start + padded)[None, :]).astype(jnp.int32), axis=1)
    te = jnp.minimum(te, N_EXPERTS - 1).astype(jnp.int32)
    rbase = tile_lo - start[te]
    nvalid = jnp.clip(counts[te] - rbase, 0, tm)
    cb_after = cb[1:, :][:, te]
    wlo = jnp.sum((cb_after <= rbase[None, :]).astype(jnp.int32), axis=0)
    whi = jnp.sum((cb_after <= (rbase + nvalid - 1)[None, :]).astype(jnp.int32), axis=0)
    nwin = jnp.where(nvalid > 0, whi - wlo + 1, 0).astype(jnp.int32)
    wlo = jnp.minimum(wlo, nb - 1).astype(jnp.int32)
    ys = _moe_grouped(hn, rankt, gatet, te, n_used.reshape(1), wlo, nwin, rbase.astype(jnp.int32),
                      cb.reshape(-1), nt, w1.astype(BF16), w3.astype(BF16), w2.astype(BF16))
    pos = jnp.where(rank >= 0, start[None, :] + rank, -1).astype(jnp.int32)
    first = start[None, :] + cb[:-1, :]
    need = cb[1:, :] - cb[:-1, :]
    ws = first // 16 * 16
    nblk = jnp.where(need > 0, (first - ws + need + COMBINE_WIN - 1) // COMBINE_WIN, 0)
    nround = jnp.maximum(jnp.max(nblk, axis=1), 1)
    return _combine(x2, pos, ys, ws.reshape(-1).astype(jnp.int32), nblk.reshape(-1).astype(jnp.int32),
                    nround.astype(jnp.int32), final_g, gw)


def kernel(x, mix_norm_g, w_in, conv_w, conv_b, w_q, w_k, w_v, w_if, b_if, skip_m, mh_norm_g, gm_v_g, w_sp, b_sp, gm_out_g, w_out, ffn_norm_g, dense_w1, dense_w3, dense_w2, moe_router, moe_w1, moe_w3, moe_w2, final_norm_g):
    B, S, D = x.shape
    depth = w_in.shape[0]
    assert depth == 2 and dense_w1.shape[0] == 1 and moe_w1.shape[0] == 1
    for l in range(depth):
        x = _mixer(x, mix_norm_g[l], w_in[l], conv_w[l], conv_b[l], w_q[l], w_k[l], w_v[l],
                   w_if[l], b_if[l], skip_m[l], mh_norm_g[l], gm_v_g[l], w_sp[l], b_sp[l],
                   gm_out_g[l], w_out[l])
        x2 = x.reshape(B * S, D)
        if l % 2 == 0:
            x = _ffn_dense(x2, ffn_norm_g[l], dense_w1[l // 2], dense_w3[l // 2],
                           dense_w2[l // 2]).reshape(B, S, D)
        else:
            x = _moe_layer(x2, ffn_norm_g[l], moe_router[l // 2], moe_w1[l // 2], moe_w3[l // 2],
                           moe_w2[l // 2], final_norm_g).reshape(B, S, D)
    return x
```

```python
import numpy as np
import jax
import jax.numpy as jnp
from jax import lax
from jax.experimental import pallas as pl
from jax.experimental.pallas import tpu as pltpu

F32 = jnp.float32
BF16 = jnp.bfloat16
EPS = 1e-6
NEG = -1e30

D_MODEL = 1024
M_HEADS = 4
HEAD_DIM = 128
M_WIDTH = M_HEADS * HEAD_DIM
G_GROUPS = 4
G_CH = 128
G_WIDTH = G_GROUPS * G_CH
G_BLOCK = 128
G_CHUNK = 64
CONV_W = 4
N_EXPERTS = 8
LANES = 128

MLSTM_CHUNK = 128
MIX_BLOCK = 512
MIX_BATCH = 2
MIX_STAGGER = 5
FFN_BLOCK = 512
FFN_CHUNK = 256
ROUTER_BLOCK = 256
MOE_TILE = 512
MOE_FF_TILE = 1792
MOE_FF_CHUNK = 256
MOE_ROW_BLOCK = 128
MOE_GATHER_WINS = 4
MOE_WIN_BUFS = 12
COMBINE_WIN = 128
VMEM_LIMIT = 56 * 1024 * 1024


def _dot(a, b):
    return jnp.dot(a, b, preferred_element_type=F32)


def _dot_nt(a, b):
    return lax.dot_general(a, b, (((1,), (1,)), ((), ())), preferred_element_type=F32)


def _dot_tn(a, b):
    return lax.dot_general(a, b, (((0,), (0,)), ((), ())), preferred_element_type=F32)


def _rms_scale(x):
    return lax.rsqrt(jnp.mean(x * x, axis=-1, keepdims=True) + EPS)


def _sigmoid(x):
    return 1.0 / (1.0 + jnp.exp(-x))


def _silu(x):
    return x * _sigmoid(x)


def _gelu_tanh(x):
    c = np.float32(np.sqrt(2.0 / np.pi))
    return 0.5 * x * (1.0 + jnp.tanh(c * (x + 0.044715 * (x * x * x))))


def _log_sigmoid(x):
    return jnp.minimum(x, 0.0) - jnp.log1p(jnp.exp(-jnp.abs(x)))


def _split3(x):
    hi = x.astype(BF16)
    r1 = x - hi.astype(F32)
    mid = r1.astype(BF16)
    lo = (r1 - mid.astype(F32)).astype(BF16)
    return hi, mid, lo


def _mixer_kernel(x_ref, *refs):
    weights = refs[:16]
    o_ref = refs[16]
    xm_ext, ct_ref, n_ref, m_ref = refs[17:21]
    scratch = refs[17:]

    @pl.when(pl.program_id(1) == 0)
    def _():
        xm_ext[:, 0:8, :] = jnp.zeros((xm_ext.shape[0], 8, M_WIDTH), F32)
        ct_ref[...] = jnp.zeros(ct_ref.shape, F32)
        n_ref[...] = jnp.zeros(n_ref.shape, F32)
        m_ref[...] = jnp.zeros(m_ref.shape, F32)

    rows = [_mixer_block(x_ref.at[bb], *weights, o_ref.at[bb], *[r.at[bb] for r in scratch])
            for bb in range(x_ref.shape[0])]
    live = list(range(len(rows)))
    tick = 0
    while live:
        for bb in list(live):
            if tick >= bb * MIX_STAGGER and next(rows[bb], "done") == "done":
                live.remove(bb)
        tick += 1


def _mixer_block(x_ref, g_ref, win_ref, convw_ref, convb_ref, wqk_ref, wv_ref,
                 wif_ref, bif_ref, tcol_ref,
                 skip_ref, mhg_ref, gvg_ref, wsp_ref, bsp_ref, gog_ref, wout_ref,
                 o_ref,
                 xm_ext, ct_ref, n_ref, m_ref, y_ref, yg_ref):
    sb = x_ref.shape[0]
    L = MLSTM_CHUNK

    x = x_ref[...]
    h = (x * _rms_scale(x) * g_ref[...]).astype(BF16)
    yield
    proj = _dot(h, win_ref[...])
    yield
    x_m = proj[:, 0:M_WIDTH]
    o_pre = proj[:, M_WIDTH:2 * M_WIDTH]
    gm_u = proj[:, 2 * M_WIDTH:2 * M_WIDTH + G_WIDTH]
    gm_v = proj[:, 2 * M_WIDTH + G_WIDTH:]

    xm_ext[8:8 + sb, :] = x_m
    acc = jnp.zeros((sb, M_WIDTH), F32) + convb_ref[...]
    for j in range(CONV_W):
        off = 8 - (CONV_W - 1) + j
        acc = acc + convw_ref[j:j + 1, :] * xm_ext[off:off + sb, :]
    x_c = _silu(acc)
    xm_ext[0:8, :] = xm_ext[sb:sb + 8, :]
    yield

    xc_b = x_c.astype(BF16)
    xm_b = x_m.astype(BF16)
    qs, ks, vs = [], [], []
    for hd in range(M_HEADS):
        sl = slice(hd * HEAD_DIM, (hd + 1) * HEAD_DIM)
        qk = _dot(xc_b[:, sl], wqk_ref[hd])
        qs.append(qk[:, 0:HEAD_DIM])
        ks.append(qk[:, HEAD_DIM:])
    for pr in range(M_HEADS // 2):
        sl = slice(2 * pr * HEAD_DIM, 2 * (pr + 1) * HEAD_DIM)
        vv = _dot(xm_b[:, sl], wv_ref[pr])
        vs.append(vv[:, 0:HEAD_DIM])
        vs.append(vv[:, HEAD_DIM:])
    qkv_b = jnp.concatenate([t.astype(BF16) for t in qs + ks + vs], axis=1)
    yield

    gcol = _dot(qkv_b, wif_ref[...]) + bif_ref[...]
    lf_col = _log_sigmoid(gcol)
    ch, cm, cl = _split3(lf_col)
    bcol3 = _dot(tcol_ref[...], jnp.concatenate([ch, cm, cl], axis=1))
    bcol = bcol3[:, 0:LANES] + bcol3[:, LANES:2 * LANES] + bcol3[:, 2 * LANES:]
    grow = gcol.T
    brow = bcol.T

    yield

    scale = np.float32(HEAD_DIM ** -0.5)
    o_gate = _sigmoid(o_pre)
    units = [(j, hd) for j in range(sb // L) for hd in range(M_HEADS)]
    rows = {u: slice(u[0] * L, (u[0] + 1) * L) for u in units}
    lanes = {u: slice(u[1] * HEAD_DIM, (u[1] + 1) * HEAD_DIM) for u in units}
    tril = (lax.broadcasted_iota(jnp.int32, (L, L), 1) <= lax.broadcasted_iota(jnp.int32, (L, L), 0))
    b_col = {u: bcol[rows[u], M_HEADS + u[1]:M_HEADS + u[1] + 1] for u in units}
    ig_col = {u: gcol[rows[u], u[1]:u[1] + 1] for u in units}
    c_row = {u: brow[M_HEADS + u[1]:M_HEADS + u[1] + 1, rows[u]] - grow[u[1]:u[1] + 1, rows[u]] for u in units}
    b_end = {u: b_col[u][L - 1:L, :] for u in units}
    c_max = {u: jnp.max(-c_row[u], axis=1, keepdims=True) for u in units}
    m_in, m_out = {}, {}
    for hd in range(M_HEADS):
        m = m_ref[hd][:, 0:1]
        for j in range(sb // L):
            m_in[(j, hd)] = m
            m = b_end[(j, hd)] + jnp.maximum(m, c_max[(j, hd)])
            m_out[(j, hd)] = m
        m_ref[hd] = jnp.broadcast_to(m, (1, LANES))
    yield

    w, a, m_t, a_prev, w_s = {}, {}, {}, {}, {}
    for u in units:
        d = jnp.where(tril, b_col[u] - c_row[u], NEG)
        inter = b_col[u] + m_in[u]
        m_t[u] = jnp.maximum(inter, jnp.max(d, axis=1, keepdims=True))
        w[u] = jnp.exp(d - m_t[u])
        a[u] = jnp.exp(inter - m_t[u])
        a_prev[u] = jnp.exp(b_end[u] + m_in[u] - m_out[u])
        w_s[u] = jnp.exp(b_end[u] - m_out[u] - (b_col[u] - ig_col[u]))
    yield

    qf = {u: qs[u[1]][rows[u]] * scale for u in units}
    qb = {u: qf[u].astype(BF16) for u in units}
    kb = {u: ks[u[1]][rows[u]].astype(BF16) for u in units}
    vb = {u: vs[u[1]][rows[u]].astype(BF16) for u in units}
    s = {u: _dot_nt(qb[u], kb[u]) * w[u] for u in units}
    yield
    sv = {u: _dot(s[u].astype(BF16), vb[u]) for u in units}
    s_sum = {u: jnp.sum(s[u], axis=1, keepdims=True) for u in units}
    yield
    kw = {u: ks[u[1]][rows[u]] * w_s[u] for u in units}
    upd = {u: _dot_tn(kw[u].astype(BF16), vb[u]) for u in units}
    k_sum = {u: jnp.sum(kw[u], axis=0, keepdims=True) for u in units}
    yield
    ct_in, n_in = {}, {}
    for hd in range(M_HEADS):
        ct = ct_ref[hd]
        n = n_ref[hd]
        for j in range(sb // L):
            u = (j, hd)
            ct_in[u] = ct
            n_in[u] = n
            ct = a_prev[u] * ct + upd[u]
            n = a_prev[u] * n + k_sum[u]
        ct_ref[hd] = ct
        n_ref[hd] = n
    qc = {u: _dot(qb[u], ct_in[u].astype(BF16)) for u in units}
    qn = {u: jnp.sum(qf[u] * n_in[u], axis=1, keepdims=True) for u in units}
    yield
    for u in units:
        num = a[u] * qc[u] + sv[u]
        den = jnp.maximum(jnp.abs(a[u] * qn[u] + s_sum[u]), jnp.exp(-m_t[u]))
        hh = o_gate[rows[u], lanes[u]] * (num / den)
        mu = jnp.mean(hh, axis=1, keepdims=True)
        dev = hh - mu
        var = jnp.mean(dev * dev, axis=1, keepdims=True)
        y_m = dev * lax.rsqrt(var + EPS) * mhg_ref[:, lanes[u]] + skip_ref[:, lanes[u]] * x_c[rows[u], lanes[u]]
        y_ref[rows[u], lanes[u]] = y_m.astype(BF16)
    yield

    ug = _gelu_tanh(gm_u)
    vg = _gelu_tanh(gm_v)
    mu = jnp.mean(vg, axis=1, keepdims=True)
    dev = vg - mu
    var = jnp.mean(dev * dev, axis=1, keepdims=True)
    vn = (dev * lax.rsqrt(var + EPS) * gvg_ref[...]).astype(BF16)
    yield
    for i in range(sb // G_BLOCK):
        rs = slice(i * G_BLOCK, (i + 1) * G_BLOCK)
        for gi in range(G_GROUPS):
            sl = slice(gi * G_CH, (gi + 1) * G_CH)
            sv = _dot(wsp_ref[gi], vn[rs, sl]) + bsp_ref[:, sl]
            yg_ref[rs, sl] = ug[rs, sl] * sv
        yield
    yg = yg_ref[...]
    y_ref[:, M_WIDTH:] = (yg * _rms_scale(yg) * gog_ref[...]).astype(BF16)
    yield

    o_ref[...] = x + _dot(y_ref[...], wout_ref[...])


def _const_spec(shape):
    nd = len(shape)
    return pl.BlockSpec(shape, lambda *_: (0,) * nd, pipeline_mode=pl.Buffered(1))


def _chunk_cumsum_matrix(n, chunk):
    i = np.arange(n)[:, None]
    j = np.arange(n)[None, :]
    return ((j <= i) & (i // chunk == j // chunk)).astype(np.float32)


def _mixer(x, g, w_in, conv_w, conv_b, w_q, w_k, w_v, w_if, b_if, skip_m, mh_norm_g,
           gm_v_g, w_sp, b_sp, gm_out_g, w_out):
    B, S, D = x.shape
    sb = min(MIX_BLOCK, S)
    assert S % sb == 0 and sb % MLSTM_CHUNK == 0 and sb % G_BLOCK == 0
    tcol = _chunk_cumsum_matrix(sb, MLSTM_CHUNK)
    wif = jnp.pad(w_if, ((0, 0), (0, LANES - 2 * M_HEADS)))
    bif = jnp.pad(b_if, (0, LANES - 2 * M_HEADS)).reshape(1, LANES)
    wqk = jnp.concatenate([w_q, w_k], axis=2)
    zero = jnp.zeros((HEAD_DIM, HEAD_DIM), w_v.dtype)
    wv2 = jnp.stack([jnp.block([[w_v[2 * p], zero], [zero, w_v[2 * p + 1]]]) for p in range(M_HEADS // 2)])
    pos = np.arange(G_BLOCK)
    chunk_mask = (pos[:, None] // G_CHUNK) >= (pos[None, :] // G_CHUNK)
    wsp = jnp.where(chunk_mask[None], w_sp, 0.0).astype(BF16)
    bsp = jnp.repeat(b_sp.T, G_CH, axis=1)
    row = lambda a: a.reshape(1, -1).astype(F32)
    args = (x, row(g), w_in.astype(BF16), conv_w.astype(F32), row(conv_b),
            wqk.astype(BF16), wv2.astype(BF16),
            wif.astype(BF16), bif.astype(F32),
            jnp.asarray(tcol, BF16),
            row(skip_m), row(mh_norm_g), row(gm_v_g), wsp, bsp.astype(F32), row(gm_out_g),
            w_out.astype(BF16))
    nbm = MIX_BATCH if B % MIX_BATCH == 0 else 1
    in_specs = [pl.BlockSpec((nbm, sb, D), lambda b, s: (b, s, 0))]
    in_specs += [_const_spec(a.shape) for a in args[1:]]
    return pl.pallas_call(
        _mixer_kernel,
        grid=(B // nbm, S // sb),
        in_specs=in_specs,
        out_specs=pl.BlockSpec((nbm, sb, D), lambda b, s: (b, s, 0)),
        out_shape=jax.ShapeDtypeStruct((B, S, D), F32),
        scratch_shapes=[
            pltpu.VMEM((nbm, sb + 8, M_WIDTH), F32),
            pltpu.VMEM((nbm, M_HEADS, HEAD_DIM, HEAD_DIM), F32),
            pltpu.VMEM((nbm, M_HEADS, 1, HEAD_DIM), F32),
            pltpu.VMEM((nbm, M_HEADS, 1, LANES), F32),
            pltpu.VMEM((nbm, sb, M_WIDTH + G_WIDTH), BF16),
            pltpu.VMEM((nbm, sb, G_WIDTH), F32),
        ],
        compiler_params=pltpu.CompilerParams(
            dimension_semantics=("arbitrary", "arbitrary"), vmem_limit_bytes=VMEM_LIMIT),
        name="mixer",
    )(*args)


def _ffn_kernel(x_ref, g_ref, w1_ref, w3_ref, w2_ref, o_ref):
    x = x_ref[...]
    hn = (x * _rms_scale(x) * g_ref[...]).astype(BF16)
    acc = x
    for c in range(w1_ref.shape[1] // FFN_CHUNK):
        sl = slice(c * FFN_CHUNK, (c + 1) * FFN_CHUNK)
        h1 = _dot(hn, w1_ref[:, sl])
        h3 = _dot(hn, w3_ref[:, sl])
        acc = acc + _dot((_silu(h1) * h3).astype(BF16), w2_ref[sl, :])
    o_ref[...] = acc


def _ffn_dense(x2, g, w1, w3, w2):
    T, D = x2.shape
    ff = w1.shape[1]
    ffp = -(-ff // FFN_CHUNK) * FFN_CHUNK
    w1p = jnp.pad(w1, ((0, 0), (0, ffp - ff))).astype(BF16)
    w3p = jnp.pad(w3, ((0, 0), (0, ffp - ff))).astype(BF16)
    w2p = jnp.pad(w2, ((0, ffp - ff), (0, 0))).astype(BF16)
    tm = min(FFN_BLOCK, T)
    assert T % tm == 0
    return pl.pallas_call(
        _ffn_kernel,
        grid=(T // tm,),
        in_specs=[pl.BlockSpec((tm, D), lambda i: (i, 0)),
                  _const_spec((1, D)), _const_spec((D, ffp)), _const_spec((D, ffp)), _const_spec((ffp, D))],
        out_specs=pl.BlockSpec((tm, D), lambda i: (i, 0)),
        out_shape=jax.ShapeDtypeStruct((T, D), F32),
        compiler_params=pltpu.CompilerParams(
            dimension_semantics=("arbitrary",), vmem_limit_bytes=VMEM_LIMIT),
        name="ffn_dense",
    )(x2, g.reshape(1, D).astype(F32), w1p, w3p, w2p)


def _router_kernel(x_ref, g_ref, wrt_ref, tri_ref, hn_ref, rank_ref, rankt_ref, gatet_ref, cnt_ref,
                   carry_ref):
    @pl.when(pl.program_id(0) == 0)
    def _():
        carry_ref[...] = jnp.zeros(carry_ref.shape, F32)

    x = x_ref[...]
    hn = x * _rms_scale(x) * g_ref[...]
    hn_ref[...] = hn.astype(BF16)
    tb = x.shape[0]
    lane = lax.broadcasted_iota(jnp.int32, (tb, LANES), 1).astype(F32)
    logits = jnp.full((tb, LANES), NEG, F32)
    for e in range(N_EXPERTS):
        col = jnp.sum(hn * wrt_ref[e:e + 1, :], axis=1, keepdims=True)
        logits = jnp.where(lane == e, col, logits)
    m1 = jnp.max(logits, axis=1, keepdims=True)
    i1 = jnp.min(jnp.where(logits == m1, lane, float(LANES)), axis=1, keepdims=True)
    rest = jnp.where(lane == i1, NEG, logits)
    m2 = jnp.max(rest, axis=1, keepdims=True)
    i2 = jnp.min(jnp.where(rest == m2, lane, float(LANES)), axis=1, keepdims=True)
    r = jnp.exp(m2 - m1)
    g1 = 1.0 / (1.0 + r)
    g2 = r * g1
    sel1 = lane == i1
    sel2 = lane == i2
    gates = jnp.where(sel1, g1, jnp.where(sel2, g2, 0.0))
    sel = jnp.where(sel1, 1.0, jnp.where(sel2, 1.0, 0.0))
    before = _dot(tri_ref[...], sel.astype(BF16))
    rank = jnp.where(sel > 0.0, carry_ref[...] + before, -1.0)
    carry = carry_ref[...] + jnp.sum(sel, axis=0, keepdims=True)
    carry_ref[...] = carry
    rank_ref[...] = rank[:, 0:N_EXPERTS].astype(jnp.int32)
    rankt_ref[0] = rank.T[0:N_EXPERTS, :]
    gatet_ref[0] = gates.T[0:N_EXPERTS, :]
    cnt_ref[0] = carry


def _router(x2, g, w_router):
    T, D = x2.shape
    tb = min(ROUTER_BLOCK, T)
    assert T % tb == 0
    nb = T // tb
    tri = np.tril(np.ones((tb, tb), np.float32), -1)
    return pl.pallas_call(
        _router_kernel,
        grid=(nb,),
        in_specs=[pl.BlockSpec((tb, D), lambda i: (i, 0)),
                  _const_spec((1, D)), _const_spec((N_EXPERTS, D)), _const_spec((tb, tb))],
        out_specs=[pl.BlockSpec((tb, D), lambda i: (i, 0)),
                   pl.BlockSpec((tb, N_EXPERTS), lambda i: (i, 0)),
                   pl.BlockSpec((1, N_EXPERTS, tb), lambda i: (i, 0, 0)),
                   pl.BlockSpec((1, N_EXPERTS, tb), lambda i: (i, 0, 0)),
                   pl.BlockSpec((1, 1, LANES), lambda i: (i, 0, 0))],
        out_shape=[jax.ShapeDtypeStruct((T, D), BF16),
                   jax.ShapeDtypeStruct((T, N_EXPERTS), jnp.int32),
                   jax.ShapeDtypeStruct((nb, N_EXPERTS, tb), F32),
                   jax.ShapeDtypeStruct((nb, N_EXPERTS, tb), F32),
                   jax.ShapeDtypeStruct((nb, 1, LANES), F32)],
        scratch_shapes=[pltpu.VMEM((1, LANES), F32)],
        compiler_params=pltpu.CompilerParams(
            dimension_semantics=("arbitrary",), vmem_limit_bytes=VMEM_LIMIT),
        name="router",
    )(x2, g.reshape(1, D).astype(F32), w_router.T.astype(F32), jnp.asarray(tri, BF16))


def _moe_kernel(te_ref, nused_ref, wlo_ref, nwin_ref, rbase_ref, sublo_ref, subhi_ref,
                hn_hbm, rankt_ref, gatet_ref, w1_ref, w3_ref, w2_ref,
                o_ref,
                hbuf, sem, xacc, gacc, xs_ref, acc_ref):
    i = pl.program_id(0)
    f = pl.program_id(1)
    used = i < nused_ref[0]
    tm = o_ref.shape[0]
    gw = rankt_ref.shape[2]
    nbuf = MOE_WIN_BUFS
    kwin = MOE_GATHER_WINS
    rblk = MOE_ROW_BLOCK
    nsub = tm // rblk

    def window_copy(tile, k, slot):
        w = wlo_ref[tile] + k
        return pltpu.make_async_copy(hn_hbm.at[pl.ds(pl.multiple_of(w * gw, gw), gw), :],
                                     hbuf.at[pl.ds(slot * gw, gw), :], sem.at[slot])

    def start_windows(tile, k_lo):
        for k in range(nbuf):
            @pl.when(k_lo + k < nwin_ref[tile])
            def _():
                window_copy(tile, k_lo + k, k).start()

    @pl.when(jnp.logical_and(used, f == 0))
    def _dispatch():
        @pl.when(i == 0)
        def _():
            hbuf[...] = jnp.zeros(hbuf.shape, BF16)
            start_windows(0, 0)

        e = te_ref[i]
        rb = rbase_ref[i].astype(F32)
        nwin = nwin_ref[i]
        w0 = wlo_ref[i]
        xacc[...] = jnp.zeros(xacc.shape, F32)
        gacc[...] = jnp.zeros(gacc.shape, F32)

        def phase_body(p, carry):
            k_lo = p * nbuf

            @pl.when(p > 0)
            def _():
                start_windows(i, k_lo)

            for k in range(nbuf):
                @pl.when(k_lo + k < nwin)
                def _():
                    window_copy(i, k_lo + k, k).wait()

            for s in range(nsub):
                rs = slice(s * rblk, (s + 1) * rblk)
                k0 = jnp.maximum(sublo_ref[i * nsub + s], k_lo)
                k1 = jnp.minimum(subhi_ref[i * nsub + s], k_lo + nbuf - 1)
                nch = jnp.maximum((k1 - k0 + kwin) // kwin, 0)

                def chunk_body(c, cc):
                    kk = k0 + c * kwin
                    rel, gate = [], []
                    for j in range(kwin):
                        live = kk + j <= k1
                        rel.append(jnp.where(live, rankt_ref[w0 + kk + j, pl.ds(e, 1), :] - rb, -1.0))
                        gate.append(gatet_ref[w0 + kk + j, pl.ds(e, 1), :])
                    rel = jnp.concatenate(rel, axis=1)
                    gate = jnp.concatenate(gate, axis=1)
                    rows = lax.broadcasted_iota(jnp.int32, (rblk, kwin * gw), 0).astype(F32) + float(s * rblk)
                    hit = rel == rows
                    onehot = jnp.where(hit, 1.0, 0.0).astype(BF16)
                    src = hbuf[pl.ds(pl.multiple_of((kk - k_lo) * gw, gw), kwin * gw), :]
                    xacc[rs, :] += _dot(onehot, src)
                    gsum = jnp.sum(jnp.where(hit, gate, 0.0), axis=1, keepdims=True)
                    gacc[rs, :] += jnp.broadcast_to(gsum, (rblk, LANES))
                    return cc

                lax.fori_loop(0, nch, chunk_body, 0)
            return carry

        lax.fori_loop(0, (nwin + nbuf - 1) // nbuf, phase_body, 0)
        xs_ref[...] = xacc[...].astype(BF16)

        @pl.when(i + 1 < nused_ref[0])
        def _():
            start_windows(i + 1, 0)

    @pl.when(used)
    def _ffn():
        x = xs_ref[...]
        part = None
        for c in range(w1_ref.shape[2] // MOE_FF_CHUNK):
            sl = slice(c * MOE_FF_CHUNK, (c + 1) * MOE_FF_CHUNK)
            h1 = _dot(x, w1_ref[0, :, sl])
            h3 = _dot(x, w3_ref[0, :, sl])
            p = _dot((_silu(h1) * h3).astype(BF16), w2_ref[0, sl, :])
            part = p if part is None else part + p

        @pl.when(f == 0)
        def _():
            acc_ref[...] = part

        @pl.when(f > 0)
        def _():
            acc_ref[...] += part

    @pl.when(f == pl.num_programs(1) - 1)
    def _():
        @pl.when(used)
        def _():
            o_ref[...] = (acc_ref[...] * gacc[:, 0:1]).astype(BF16)

        @pl.when(jnp.logical_not(used))
        def _():
            o_ref[...] = jnp.zeros(o_ref.shape, BF16)


def _moe_grouped(hn, rankt, gatet, tile_expert, n_used, wlo, nwin, rbase, sublo, subhi, nt, w1, w3, w2):
    T, D = hn.shape
    nbp, _, gw = rankt.shape
    E, _, ff = w1.shape
    tm, tf = MOE_TILE, MOE_FF_TILE
    assert ff % tf == 0 and tf % MOE_FF_CHUNK == 0 and tm % MOE_ROW_BLOCK == 0
    nf = ff // tf

    def fsel(i, f, nu):
        return jnp.where(i < nu[0], f, nf - 1)

    grid_spec = pltpu.PrefetchScalarGridSpec(
        num_scalar_prefetch=7,
        grid=(nt, nf),
        in_specs=[pl.BlockSpec(memory_space=pl.ANY),
                  pl.BlockSpec((nbp, N_EXPERTS, gw), lambda i, f, *_: (0, 0, 0), pipeline_mode=pl.Buffered(1)),
                  pl.BlockSpec((nbp, N_EXPERTS, gw), lambda i, f, *_: (0, 0, 0), pipeline_mode=pl.Buffered(1)),
                  pl.BlockSpec((1, D, tf), lambda i, f, te, nu, *_: (te[i], 0, fsel(i, f, nu))),
                  pl.BlockSpec((1, D, tf), lambda i, f, te, nu, *_: (te[i], 0, fsel(i, f, nu))),
                  pl.BlockSpec((1, tf, D), lambda i, f, te, nu, *_: (te[i], fsel(i, f, nu), 0))],
        out_specs=pl.BlockSpec((tm, D), lambda i, f, *_: (i, 0)),
        scratch_shapes=[pltpu.VMEM(((MOE_WIN_BUFS + MOE_GATHER_WINS - 1) * gw, D), BF16),
                        pltpu.SemaphoreType.DMA((MOE_WIN_BUFS,)),
                        pltpu.VMEM((tm, D), F32),
                        pltpu.VMEM((tm, LANES), F32),
                        pltpu.VMEM((tm, D), BF16),
                        pltpu.VMEM((tm, D), F32)],
    )
    return pl.pallas_call(
        _moe_kernel,
        grid_spec=grid_spec,
        out_shape=jax.ShapeDtypeStruct((nt * tm, D), BF16),
        compiler_params=pltpu.CompilerParams(
            dimension_semantics=("arbitrary", "arbitrary"), vmem_limit_bytes=VMEM_LIMIT),
        name="moe_ffn",
    )(tile_expert, n_used, wlo, nwin, rbase, sublo, subhi, hn, rankt, gatet, w1, w3, w2)


def _combine_kernel(ws_ref, nblk_ref, nround_ref,
                    x_ref, pos_ref, g_ref, ys_hbm, o_ref, ybuf, sem):
    j = pl.program_id(0)
    nj = pl.num_programs(0)
    tt = x_ref.shape[0]
    wc = COMBINE_WIN
    extra_slot = 2

    def window_copies(tile, rnd, slot):
        cps = []
        for e in range(N_EXPERTS):
            t = tile * N_EXPERTS + e
            s = ws_ref[t] + jnp.where(rnd < nblk_ref[t], rnd, 0) * wc
            cps.append(pltpu.make_async_copy(ys_hbm.at[pl.ds(pl.multiple_of(s, 16), wc), :],
                                             ybuf.at[slot, pl.ds(e * wc, wc), :], sem.at[slot]))
        return cps

    def select(rnd, slot):
        pos = pos_ref[...]
        lane = lax.broadcasted_iota(jnp.int32, (tt, wc), 1)
        parts = []
        for e in range(N_EXPERTS):
            t = j * N_EXPERTS + e
            base = jnp.where(rnd < nblk_ref[t], ws_ref[t] + rnd * wc, -2 * wc)
            parts.append(jnp.where(pos[:, e:e + 1] == base + lane, 1.0, 0.0).astype(BF16))
        return _dot(jnp.concatenate(parts, axis=1), ybuf[slot])

    @pl.when(j == 0)
    def _():
        for cp in window_copies(0, 0, 0):
            cp.start()

    @pl.when(j + 1 < nj)
    def _():
        for cp in window_copies(j + 1, 0, (j + 1) % 2):
            cp.start()

    for cp in window_copies(j, 0, j % 2):
        cp.wait()
    acc = select(0, j % 2)

    def round_body(rnd, acc):
        for cp in window_copies(j, rnd, extra_slot):
            cp.start()
        for cp in window_copies(j, rnd, extra_slot):
            cp.wait()
        return acc + select(rnd, extra_slot)

    acc = lax.fori_loop(1, nround_ref[j], round_body, acc)
    x = x_ref[...] + acc
    o_ref[...] = x * _rms_scale(x) * g_ref[...]


def _combine(x2, pos, ys, ws, nblk, nround, g, tt):
    T, D = x2.shape
    grid_spec = pltpu.PrefetchScalarGridSpec(
        num_scalar_prefetch=3,
        grid=(T // tt,),
        in_specs=[pl.BlockSpec((tt, D), lambda j, *_: (j, 0)),
                  pl.BlockSpec((tt, N_EXPERTS), lambda j, *_: (j, 0)),
                  pl.BlockSpec((1, D), lambda j, *_: (0, 0)),
                  pl.BlockSpec(memory_space=pl.ANY)],
        out_specs=pl.BlockSpec((tt, D), lambda j, *_: (j, 0)),
        scratch_shapes=[pltpu.VMEM((3, N_EXPERTS * COMBINE_WIN, D), BF16),
                        pltpu.SemaphoreType.DMA((3,))],
    )
    return pl.pallas_call(
        _combine_kernel,
        grid_spec=grid_spec,
        out_shape=jax.ShapeDtypeStruct((T, D), F32),
        compiler_params=pltpu.CompilerParams(
            dimension_semantics=("arbitrary",), vmem_limit_bytes=VMEM_LIMIT),
        name="combine",
    )(ws, nblk, nround, x2, pos, g.reshape(1, D).astype(F32), ys)


def _moe_layer(x2, g, w_router, w1, w3, w2, final_g):
    T, D = x2.shape
    tm = MOE_TILE
    hn, rank, rankt, gatet, cnt = _router(x2, g, w_router)
    nb, _, gw = rankt.shape
    cb = jnp.concatenate([jnp.zeros((1, N_EXPERTS), jnp.int32),
                          cnt[:, 0, :N_EXPERTS].astype(jnp.int32)], axis=0)
    counts = cb[-1]
    padded = (counts + tm - 1) // tm * tm
    start = jnp.cumsum(padded) - padded
    nt = (2 * T) // tm + N_EXPERTS + 1
    n_used = (jnp.sum(padded) // tm).astype(jnp.int32)
    tile_lo = jnp.arange(nt, dtype=jnp.int32) * tm
    te = jnp.sum((tile_lo[:, None] >= (start + padded)[None, :]).astype(jnp.int32), axis=1)
    te = jnp.minimum(te, N_EXPERTS - 1).astype(jnp.int32)
    rbase = tile_lo - start[te]
    nvalid = jnp.clip(counts[te] - rbase, 0, tm)
    cb_after = cb[1:, :][:, te]
    wlo = jnp.sum((cb_after <= rbase[None, :]).astype(jnp.int32), axis=0)
    whi = jnp.sum((cb_after <= (rbase + nvalid - 1)[None, :]).astype(jnp.int32), axis=0)
    nwin = jnp.where(nvalid > 0, whi - wlo + 1, 0).astype(jnp.int32)
    wlo = jnp.minimum(wlo, nb - 1).astype(jnp.int32)
    nsub = tm // MOE_ROW_BLOCK
    r0 = rbase[:, None] + jnp.arange(nsub, dtype=jnp.int32)[None, :] * MOE_ROW_BLOCK
    nv = jnp.clip(counts[te][:, None] - r0, 0, MOE_ROW_BLOCK)
    sub_a = jnp.sum((cb_after[:, :, None] <= r0[None]).astype(jnp.int32), axis=0)
    sub_b = jnp.sum((cb_after[:, :, None] <= (r0 + nv - 1)[None]).astype(jnp.int32), axis=0)
    sublo = jnp.where(nv > 0, sub_a - wlo[:, None], 0).astype(jnp.int32)
    subhi = jnp.where(nv > 0, sub_b - wlo[:, None], -1).astype(jnp.int32)
    pad = MOE_GATHER_WINS - 1
    rankt_p = jnp.concatenate([rankt, jnp.full((pad, N_EXPERTS, gw), -1.0, F32)], axis=0)
    gatet_p = jnp.concatenate([gatet, jnp.zeros((pad, N_EXPERTS, gw), F32)], axis=0)
    ys = _moe_grouped(hn, rankt_p, gatet_p, te, n_used.reshape(1), wlo, nwin, rbase.astype(jnp.int32),
                      sublo.reshape(-1), subhi.reshape(-1), nt,
                      w1.astype(BF16), w3.astype(BF16), w2.astype(BF16))
    pos = jnp.where(rank >= 0, start[None, :] + rank, -1).astype(jnp.int32)
    first = start[None, :] + cb[:-1, :]
    need = cb[1:, :] - cb[:-1, :]
    ws = first // 16 * 16
    nblk = jnp.where(need > 0, (first - ws + need + COMBINE_WIN - 1) // COMBINE_WIN, 0)
    nround = jnp.maximum(jnp.max(nblk, axis=1), 1)
    return _combine(x2, pos, ys, ws.reshape(-1).astype(jnp.int32), nblk.reshape(-1).astype(jnp.int32),
                    nround.astype(jnp.int32), final_g, gw)


def kernel(x, mix_norm_g, w_in, conv_w, conv_b, w_q, w_k, w_v, w_if, b_if, skip_m, mh_norm_g, gm_v_g, w_sp, b_sp, gm_out_g, w_out, ffn_norm_g, dense_w1, dense_w3, dense_w2, moe_router, moe_w1, moe_w3, moe_w2, final_norm_g):
    B, S, D = x.shape
    depth = w_in.shape[0]
    assert depth == 2 and dense_w1.shape[0] == 1 and moe_w1.shape[0] == 1
    for l in range(depth):
        x = _mixer(x, mix_norm_g[l], w_in[l], conv_w[l], conv_b[l], w_q[l], w_k[l], w_v[l],
                   w_if[l], b_if[l], skip_m[l], mh_norm_g[l], gm_v_g[l], w_sp[l], b_sp[l],
                   gm_out_g[l], w_out[l])
        x2 = x.reshape(B * S, D)
        if l % 2 == 0:
            x = _ffn_dense(x2, ffn_norm_g[l], dense_w1[l // 2], dense_w3[l // 2],
                           dense_w2[l // 2]).reshape(B, S, D)
        else:
            x = _moe_layer(x2, ffn_norm_g[l], moe_router[l // 2], moe_w1[l // 2], moe_w3[l // 2],
                           moe_w2[l // 2], final_norm_g).reshape(B, S, D)
    return x
```

```python
import numpy as np
import jax
import jax.numpy as jnp
from jax import lax
from jax.experimental import pallas as pl
from jax.experimental.pallas import tpu as pltpu

F32 = jnp.float32
BF16 = jnp.bfloat16
EPS = 1e-6
NEG = -1e30

D_MODEL = 1024
M_HEADS = 4
HEAD_DIM = 128
M_WIDTH = M_HEADS * HEAD_DIM
G_GROUPS = 4
G_CH = 128
G_WIDTH = G_GROUPS * G_CH
G_BLOCK = 128
G_CHUNK = 64
CONV_W = 4
N_EXPERTS = 8
LANES = 128

MLSTM_CHUNK = 128
MIX_BLOCK = 512
MIX_BATCH = 2
MIX_STAGGER = 5
FFN_BLOCK = 512
FFN_CHUNK = 256
ROUTER_BLOCK = 256
MOE_TILE = 512
MOE_FF_TILE = 1792
MOE_FF_CHUNK = 256
MOE_ROW_BLOCK = 128
MOE_GATHER_WINS = 4
MOE_WIN_BUFS = 12
COMBINE_WIN = 128
VMEM_LIMIT = 56 * 1024 * 1024


def _dot(a, b):
    return jnp.dot(a, b, preferred_element_type=F32)


def _dot_nt(a, b):
    return lax.dot_general(a, b, (((1,), (1,)), ((), ())), preferred_element_type=F32)


def _dot_tn(a, b):
    return lax.dot_general(a, b, (((0,), (0,)), ((), ())), preferred_element_type=F32)


def _rms_scale(x):
    return lax.rsqrt(jnp.mean(x * x, axis=-1, keepdims=True) + EPS)


def _sigmoid(x):
    return 1.0 / (1.0 + jnp.exp(-x))


def _silu(x):
    return x * _sigmoid(x)


def _gelu_tanh(x):
    c = np.float32(np.sqrt(2.0 / np.pi))
    return 0.5 * x * (1.0 + jnp.tanh(c * (x + 0.044715 * (x * x * x))))


def _log_sigmoid(x):
    return jnp.minimum(x, 0.0) - jnp.log1p(jnp.exp(-jnp.abs(x)))


def _split3(x):
    hi = x.astype(BF16)
    r1 = x - hi.astype(F32)
    mid = r1.astype(BF16)
    lo = (r1 - mid.astype(F32)).astype(BF16)
    return hi, mid, lo


def _mixer_kernel(x_ref, *refs):
    weights = refs[:16]
    o_ref = refs[16]
    xm_ext, ct_ref, n_ref, m_ref = refs[17:21]
    scratch = refs[17:]

    @pl.when(pl.program_id(1) == 0)
    def _():
        xm_ext[:, 0:8, :] = jnp.zeros((xm_ext.shape[0], 8, M_WIDTH), F32)
        ct_ref[...] = jnp.zeros(ct_ref.shape, F32)
        n_ref[...] = jnp.zeros(n_ref.shape, F32)
        m_ref[...] = jnp.zeros(m_ref.shape, F32)

    rows = [_mixer_block(x_ref.at[bb], *weights, o_ref.at[bb], *[r.at[bb] for r in scratch])
            for bb in range(x_ref.shape[0])]
    live = list(range(len(rows)))
    tick = 0
    while live:
        for bb in list(live):
            if tick >= bb * MIX_STAGGER and next(rows[bb], "done") == "done":
                live.remove(bb)
        tick += 1


def _mixer_block(x_ref, g_ref, win_ref, convw_ref, convb_ref, wqk_ref, wv_ref,
                 wif_ref, bif_ref, tcol_ref,
                 skip_ref, mhg_ref, gvg_ref, wsp_ref, bsp_ref, gog_ref, wout_ref,
                 o_ref,
                 xm_ext, ct_ref, n_ref, m_ref, y_ref, yg_ref):
    sb = x_ref.shape[0]
    L = MLSTM_CHUNK

    x = x_ref[...]
    h = (x * _rms_scale(x) * g_ref[...]).astype(BF16)
    yield
    x_m = _dot(h, win_ref[:, 0:M_WIDTH])
    yield

    xm_ext[8:8 + sb, :] = x_m
    acc = jnp.zeros((sb, M_WIDTH), F32) + convb_ref[...]
    for j in range(CONV_W):
        off = 8 - (CONV_W - 1) + j
        acc = acc + convw_ref[j:j + 1, :] * xm_ext[off:off + sb, :]
    x_c = _silu(acc)
    xm_ext[0:8, :] = xm_ext[sb:sb + 8, :]
    xc_b = x_c.astype(BF16)
    xm_b = x_m.astype(BF16)
    yield
    gm_v = _dot(h, win_ref[:, 2 * M_WIDTH + G_WIDTH:])
    yield

    qs, ks, vs = [], [], []
    for hd in range(M_HEADS):
        sl = slice(hd * HEAD_DIM, (hd + 1) * HEAD_DIM)
        qk = _dot(xc_b[:, sl], wqk_ref[hd])
        qs.append(qk[:, 0:HEAD_DIM])
        ks.append(qk[:, HEAD_DIM:])
    for pr in range(M_HEADS // 2):
        sl = slice(2 * pr * HEAD_DIM, 2 * (pr + 1) * HEAD_DIM)
        vv = _dot(xm_b[:, sl], wv_ref[pr])
        vs.append(vv[:, 0:HEAD_DIM])
        vs.append(vv[:, HEAD_DIM:])
    yield
    qkv_b = jnp.concatenate([t.astype(BF16) for t in qs + ks + vs], axis=1)

    vg = _gelu_tanh(gm_v)
    mu = jnp.mean(vg, axis=1, keepdims=True)
    dev = vg - mu
    var = jnp.mean(dev * dev, axis=1, keepdims=True)
    vn = (dev * lax.rsqrt(var + EPS) * gvg_ref[...]).astype(BF16)
    yield

    gcol = _dot(qkv_b, wif_ref[...]) + bif_ref[...]
    gm_u = _dot(h, win_ref[:, 2 * M_WIDTH:2 * M_WIDTH + G_WIDTH])
    yield
    lf_col = _log_sigmoid(gcol)
    ch, cm, cl = _split3(lf_col)
    ug = _gelu_tanh(gm_u)
    yield
    bcol3 = _dot(tcol_ref[...], jnp.concatenate([ch, cm, cl], axis=1))
    bcol = bcol3[:, 0:LANES] + bcol3[:, LANES:2 * LANES] + bcol3[:, 2 * LANES:]
    grow = gcol.T
    brow = bcol.T
    yield

    for i in range(sb // G_BLOCK):
        rs = slice(i * G_BLOCK, (i + 1) * G_BLOCK)
        for gi in range(G_GROUPS):
            sl = slice(gi * G_CH, (gi + 1) * G_CH)
            sv = _dot(wsp_ref[gi], vn[rs, sl]) + bsp_ref[:, sl]
            yg_ref[rs, sl] = ug[rs, sl] * sv
    yield
    yg = yg_ref[...]
    y_ref[:, M_WIDTH:] = (yg * _rms_scale(yg) * gog_ref[...]).astype(BF16)
    o_pre = _dot(h, win_ref[:, M_WIDTH:2 * M_WIDTH])
    yield

    scale = np.float32(HEAD_DIM ** -0.5)
    o_gate = _sigmoid(o_pre)
    units = [(j, hd) for j in range(sb // L) for hd in range(M_HEADS)]
    rows = {u: slice(u[0] * L, (u[0] + 1) * L) for u in units}
    lanes = {u: slice(u[1] * HEAD_DIM, (u[1] + 1) * HEAD_DIM) for u in units}
    tril = (lax.broadcasted_iota(jnp.int32, (L, L), 1) <= lax.broadcasted_iota(jnp.int32, (L, L), 0))
    b_col = {u: bcol[rows[u], M_HEADS + u[1]:M_HEADS + u[1] + 1] for u in units}
    ig_col = {u: gcol[rows[u], u[1]:u[1] + 1] for u in units}
    c_row = {u: brow[M_HEADS + u[1]:M_HEADS + u[1] + 1, rows[u]] - grow[u[1]:u[1] + 1, rows[u]] for u in units}
    b_end = {u: b_col[u][L - 1:L, :] for u in units}
    c_max = {u: jnp.max(-c_row[u], axis=1, keepdims=True) for u in units}
    m_in, m_out = {}, {}
    for hd in range(M_HEADS):
        m = m_ref[hd][:, 0:1]
        for j in range(sb // L):
            m_in[(j, hd)] = m
            m = b_end[(j, hd)] + jnp.maximum(m, c_max[(j, hd)])
            m_out[(j, hd)] = m
        m_ref[hd] = jnp.broadcast_to(m, (1, LANES))
    yield

    w, a, m_t, a_prev, w_s = {}, {}, {}, {}, {}
    for u in units:
        d = jnp.where(tril, b_col[u] - c_row[u], NEG)
        inter = b_col[u] + m_in[u]
        m_t[u] = jnp.maximum(inter, jnp.max(d, axis=1, keepdims=True))
        w[u] = jnp.exp(d - m_t[u])
        a[u] = jnp.exp(inter - m_t[u])
        a_prev[u] = jnp.exp(b_end[u] + m_in[u] - m_out[u])
        w_s[u] = jnp.exp(b_end[u] - m_out[u] - (b_col[u] - ig_col[u]))
    yield

    qf = {u: qs[u[1]][rows[u]] * scale for u in units}
    qb = {u: qf[u].astype(BF16) for u in units}
    kb = {u: ks[u[1]][rows[u]].astype(BF16) for u in units}
    vb = {u: vs[u[1]][rows[u]].astype(BF16) for u in units}
    s = {u: _dot_nt(qb[u], kb[u]) * w[u] for u in units}
    yield
    sv = {u: _dot(s[u].astype(BF16), vb[u]) for u in units}
    s_sum = {u: jnp.sum(s[u], axis=1, keepdims=True) for u in units}
    yield
    kw = {u: ks[u[1]][rows[u]] * w_s[u] for u in units}
    upd = {u: _dot_tn(kw[u].astype(BF16), vb[u]) for u in units}
    k_sum = {u: jnp.sum(kw[u], axis=0, keepdims=True) for u in units}
    yield
    ct_in, n_in = {}, {}
    for hd in range(M_HEADS):
        ct = ct_ref[hd]
        n = n_ref[hd]
        for j in range(sb // L):
            u = (j, hd)
            ct_in[u] = ct
            n_in[u] = n
            ct = a_prev[u] * ct + upd[u]
            n = a_prev[u] * n + k_sum[u]
        ct_ref[hd] = ct
        n_ref[hd] = n
    qc = {u: _dot(qb[u], ct_in[u].astype(BF16)) for u in units}
    qn = {u: jnp.sum(qf[u] * n_in[u], axis=1, keepdims=True) for u in units}
    yield
    for u in units:
        num = a[u] * qc[u] + sv[u]
        den = jnp.maximum(jnp.abs(a[u] * qn[u] + s_sum[u]), jnp.exp(-m_t[u]))
        hh = o_gate[rows[u], lanes[u]] * (num / den)
        mu = jnp.mean(hh, axis=1, keepdims=True)
        dev = hh - mu
        var = jnp.mean(dev * dev, axis=1, keepdims=True)
        y_m = dev * lax.rsqrt(var + EPS) * mhg_ref[:, lanes[u]] + skip_ref[:, lanes[u]] * x_c[rows[u], lanes[u]]
        y_ref[rows[u], lanes[u]] = y_m.astype(BF16)
    yield

    o_ref[...] = x + _dot(y_ref[...], wout_ref[...])


def _const_spec(shape):
    nd = len(shape)
    return pl.BlockSpec(shape, lambda *_: (0,) * nd, pipeline_mode=pl.Buffered(1))


def _chunk_cumsum_matrix(n, chunk):
    i = np.arange(n)[:, None]
    j = np.arange(n)[None, :]
    return ((j <= i) & (i // chunk == j // chunk)).astype(np.float32)


def _mixer(x, g, w_in, conv_w, conv_b, w_q, w_k, w_v, w_if, b_if, skip_m, mh_norm_g,
           gm_v_g, w_sp, b_sp, gm_out_g, w_out):
    B, S, D = x.shape
    sb = min(MIX_BLOCK, S)
    assert S % sb == 0 and sb % MLSTM_CHUNK == 0 and sb % G_BLOCK == 0
    tcol = _chunk_cumsum_matrix(sb, MLSTM_CHUNK)
    wif = jnp.pad(w_if, ((0, 0), (0, LANES - 2 * M_HEADS)))
    bif = jnp.pad(b_if, (0, LANES - 2 * M_HEADS)).reshape(1, LANES)
    wqk = jnp.concatenate([w_q, w_k], axis=2)
    zero = jnp.zeros((HEAD_DIM, HEAD_DIM), w_v.dtype)
    wv2 = jnp.stack([jnp.block([[w_v[2 * p], zero], [zero, w_v[2 * p + 1]]]) for p in range(M_HEADS // 2)])
    pos = np.arange(G_BLOCK)
    chunk_mask = (pos[:, None] // G_CHUNK) >= (pos[None, :] // G_CHUNK)
    wsp = jnp.where(chunk_mask[None], w_sp, 0.0).astype(BF16)
    bsp = jnp.repeat(b_sp.T, G_CH, axis=1)
    row = lambda a: a.reshape(1, -1).astype(F32)
    args = (x, row(g), w_in.astype(BF16), conv_w.astype(F32), row(conv_b),
            wqk.astype(BF16), wv2.astype(BF16),
            wif.astype(BF16), bif.astype(F32),
            jnp.asarray(tcol, BF16),
            row(skip_m), row(mh_norm_g), row(gm_v_g), wsp, bsp.astype(F32), row(gm_out_g),
            w_out.astype(BF16))
    nbm = MIX_BATCH if B % MIX_BATCH == 0 else 1
    in_specs = [pl.BlockSpec((nbm, sb, D), lambda b, s: (b, s, 0))]
    in_specs += [_const_spec(a.shape) for a in args[1:]]
    return pl.pallas_call(
        _mixer_kernel,
        grid=(B // nbm, S // sb),
        in_specs=in_specs,
        out_specs=pl.BlockSpec((nbm, sb, D), lambda b, s: (b, s, 0)),
        out_shape=jax.ShapeDtypeStruct((B, S, D), F32),
        scratch_shapes=[
            pltpu.VMEM((nbm, sb + 8, M_WIDTH), F32),
            pltpu.VMEM((nbm, M_HEADS, HEAD_DIM, HEAD_DIM), F32),
            pltpu.VMEM((nbm, M_HEADS, 1, HEAD_DIM), F32),
            pltpu.VMEM((nbm, M_HEADS, 1, LANES), F32),
            pltpu.VMEM((nbm, sb, M_WIDTH + G_WIDTH), BF16),
            pltpu.VMEM((nbm, sb, G_WIDTH), F32),
        ],
        compiler_params=pltpu.CompilerParams(
            dimension_semantics=("arbitrary", "arbitrary"), vmem_limit_bytes=VMEM_LIMIT),
        name="mixer",
    )(*args)


def _ffn_kernel(x_ref, g_ref, w1_ref, w3_ref, w2_ref, o_ref):
    x = x_ref[...]
    hn = (x * _rms_scale(x) * g_ref[...]).astype(BF16)
    acc = x
    for c in range(w1_ref.shape[1] // FFN_CHUNK):
        sl = slice(c * FFN_CHUNK, (c + 1) * FFN_CHUNK)
        h1 = _dot(hn, w1_ref[:, sl])
        h3 = _dot(hn, w3_ref[:, sl])
        acc = acc + _dot((_silu(h1) * h3).astype(BF16), w2_ref[sl, :])
    o_ref[...] = acc


def _ffn_dense(x2, g, w1, w3, w2):
    T, D = x2.shape
    ff = w1.shape[1]
    ffp = -(-ff // FFN_CHUNK) * FFN_CHUNK
    w1p = jnp.pad(w1, ((0, 0), (0, ffp - ff))).astype(BF16)
    w3p = jnp.pad(w3, ((0, 0), (0, ffp - ff))).astype(BF16)
    w2p = jnp.pad(w2, ((0, ffp - ff), (0, 0))).astype(BF16)
    tm = min(FFN_BLOCK, T)
    assert T % tm == 0
    return pl.pallas_call(
        _ffn_kernel,
        grid=(T // tm,),
        in_specs=[pl.BlockSpec((tm, D), lambda i: (i, 0)),
                  _const_spec((1, D)), _const_spec((D, ffp)), _const_spec((D, ffp)), _const_spec((ffp, D))],
        out_specs=pl.BlockSpec((tm, D), lambda i: (i, 0)),
        out_shape=jax.ShapeDtypeStruct((T, D), F32),
        compiler_params=pltpu.CompilerParams(
            dimension_semantics=("arbitrary",), vmem_limit_bytes=VMEM_LIMIT),
        name="ffn_dense",
    )(x2, g.reshape(1, D).astype(F32), w1p, w3p, w2p)


def _router_kernel(x_ref, g_ref, wr_ref, tri_ref, hn_ref, rank_ref, rankt_ref, gatet_ref, cnt_ref,
                   carry_ref):
    @pl.when(pl.program_id(0) == 0)
    def _():
        carry_ref[...] = jnp.zeros(carry_ref.shape, F32)

    x = x_ref[...]
    hn = x * _rms_scale(x) * g_ref[...]
    hi = hn.astype(BF16)
    hn_ref[...] = hi
    tb = x.shape[0]
    lane = lax.broadcasted_iota(jnp.int32, (tb, LANES), 1).astype(F32)
    lo = (hn - hi.astype(F32)).astype(BF16)
    hw = _dot(hi, wr_ref[...])
    logits = hw[:, 0:LANES] + hw[:, LANES:] + _dot(lo, wr_ref[:, 0:LANES])
    logits = jnp.where(lane < float(N_EXPERTS), logits, NEG)
    m1 = jnp.max(logits, axis=1, keepdims=True)
    i1 = jnp.min(jnp.where(logits == m1, lane, float(LANES)), axis=1, keepdims=True)
    rest = jnp.where(lane == i1, NEG, logits)
    m2 = jnp.max(rest, axis=1, keepdims=True)
    i2 = jnp.min(jnp.where(rest == m2, lane, float(LANES)), axis=1, keepdims=True)
    r = jnp.exp(m2 - m1)
    g1 = 1.0 / (1.0 + r)
    g2 = r * g1
    sel1 = lane == i1
    sel2 = lane == i2
    gates = jnp.where(sel1, g1, jnp.where(sel2, g2, 0.0))
    sel = jnp.where(sel1, 1.0, jnp.where(sel2, 1.0, 0.0))
    before = _dot(tri_ref[...], sel.astype(BF16))
    rank = jnp.where(sel > 0.0, carry_ref[...] + before, -1.0)
    carry = carry_ref[...] + jnp.sum(sel, axis=0, keepdims=True)
    carry_ref[...] = carry
    rank_ref[...] = rank[:, 0:N_EXPERTS].astype(jnp.int32)
    rankt_ref[0] = rank.T[0:N_EXPERTS, :]
    gatet_ref[0] = gates.T[0:N_EXPERTS, :]
    cnt_ref[0] = carry


def _router(x2, g, w_router):
    T, D = x2.shape
    tb = min(ROUTER_BLOCK, T)
    assert T % tb == 0
    nb = T // tb
    tri = np.tril(np.ones((tb, tb), np.float32), -1)
    wr = jnp.pad(w_router.astype(F32), ((0, 0), (0, LANES - N_EXPERTS)))
    wh = wr.astype(BF16)
    wl = (wr - wh.astype(F32)).astype(BF16)
    return pl.pallas_call(
        _router_kernel,
        grid=(nb,),
        in_specs=[pl.BlockSpec((tb, D), lambda i: (i, 0)),
                  _const_spec((1, D)), _const_spec((D, 2 * LANES)), _const_spec((tb, tb))],
        out_specs=[pl.BlockSpec((tb, D), lambda i: (i, 0)),
                   pl.BlockSpec((tb, N_EXPERTS), lambda i: (i, 0)),
                   pl.BlockSpec((1, N_EXPERTS, tb), lambda i: (i, 0, 0)),
                   pl.BlockSpec((1, N_EXPERTS, tb), lambda i: (i, 0, 0)),
                   pl.BlockSpec((1, 1, LANES), lambda i: (i, 0, 0))],
        out_shape=[jax.ShapeDtypeStruct((T, D), BF16),
                   jax.ShapeDtypeStruct((T, N_EXPERTS), jnp.int32),
                   jax.ShapeDtypeStruct((nb, N_EXPERTS, tb), F32),
                   jax.ShapeDtypeStruct((nb, N_EXPERTS, tb), F32),
                   jax.ShapeDtypeStruct((nb, 1, LANES), F32)],
        scratch_shapes=[pltpu.VMEM((1, LANES), F32)],
        compiler_params=pltpu.CompilerParams(
            dimension_semantics=("arbitrary",), vmem_limit_bytes=VMEM_LIMIT),
        name="router",
    )(x2, g.reshape(1, D).astype(F32), jnp.concatenate([wh, wl], axis=1), jnp.asarray(tri, BF16))


def _moe_kernel(te_ref, nused_ref, wlo_ref, nwin_ref, rbase_ref, sublo_ref, subhi_ref,
                hn_hbm, rankt_ref, gatet_ref, w1_ref, w3_ref, w2_ref,
                o_ref,
                hbuf, sem, xacc, gacc, xs_ref, acc_ref):
    i = pl.program_id(0)
    f = pl.program_id(1)
    used = i < nused_ref[0]
    tm = o_ref.shape[0]
    gw = rankt_ref.shape[2]
    nbuf = MOE_WIN_BUFS
    kwin = MOE_GATHER_WINS
    rblk = MOE_ROW_BLOCK
    nsub = tm // rblk

    def window_copy(tile, k, slot):
        w = wlo_ref[tile] + k
        return pltpu.make_async_copy(hn_hbm.at[pl.ds(pl.multiple_of(w * gw, gw), gw), :],
                                     hbuf.at[pl.ds(slot * gw, gw), :], sem.at[slot])

    def start_windows(tile, k_lo):
        for k in range(nbuf):
            @pl.when(k_lo + k < nwin_ref[tile])
            def _():
                window_copy(tile, k_lo + k, k).start()

    @pl.when(jnp.logical_and(used, f == 0))
    def _dispatch():
        @pl.when(i == 0)
        def _():
            hbuf[...] = jnp.zeros(hbuf.shape, BF16)
            start_windows(0, 0)

        e = te_ref[i]
        rb = rbase_ref[i].astype(F32)
        nwin = nwin_ref[i]
        w0 = wlo_ref[i]
        xacc[...] = jnp.zeros(xacc.shape, F32)
        gacc[...] = jnp.zeros(gacc.shape, F32)

        def phase_body(p, carry):
            k_lo = p * nbuf

            @pl.when(p > 0)
            def _():
                start_windows(i, k_lo)

            for k in range(nbuf):
                @pl.when(k_lo + k < nwin)
                def _():
                    window_copy(i, k_lo + k, k).wait()

            for s in range(nsub):
                rs = slice(s * rblk, (s + 1) * rblk)
                k0 = jnp.maximum(sublo_ref[i * nsub + s], k_lo)
                k1 = jnp.minimum(subhi_ref[i * nsub + s], k_lo + nbuf - 1)
                nch = jnp.maximum((k1 - k0 + kwin) // kwin, 0)

                def chunk_body(c, cc, s=s, rs=rs, k0=k0, k1=k1):
                    kk = jnp.minimum(k0 + c * kwin, jnp.minimum(k_lo + nbuf, nwin) - 1)
                    rel, gate = [], []
                    for j in range(kwin):
                        live = jnp.logical_and(kk + j >= k0, kk + j <= k1)
                        rel.append(jnp.where(live, rankt_ref[w0 + kk + j, pl.ds(e, 1), :] - rb, -1.0))
                        gate.append(gatet_ref[w0 + kk + j, pl.ds(e, 1), :])
                    rel = jnp.concatenate(rel, axis=1)
                    gate = jnp.concatenate(gate, axis=1)
                    rows = lax.broadcasted_iota(jnp.int32, (rblk, kwin * gw), 0).astype(F32) + float(s * rblk)
                    hit = rel == rows
                    onehot = jnp.where(hit, 1.0, 0.0).astype(BF16)
                    src = hbuf[pl.ds(pl.multiple_of((kk - k_lo) * gw, gw), kwin * gw), :]
                    xacc[rs, :] += _dot(onehot, src)
                    gsum = jnp.sum(jnp.where(hit, gate, 0.0), axis=1, keepdims=True)
                    gacc[rs, :] += jnp.broadcast_to(gsum, (rblk, LANES))
                    return cc

                chunk_body(0, 0)
                lax.fori_loop(1, nch, chunk_body, 0)
            return carry

        lax.fori_loop(0, (nwin + nbuf - 1) // nbuf, phase_body, 0)
        xs_ref[...] = xacc[...].astype(BF16)

        @pl.when(i + 1 < nused_ref[0])
        def _():
            start_windows(i + 1, 0)

    @pl.when(used)
    def _ffn():
        x = xs_ref[...]
        part = None
        for c in range(w1_ref.shape[2] // MOE_FF_CHUNK):
            sl = slice(c * MOE_FF_CHUNK, (c + 1) * MOE_FF_CHUNK)
            h1 = _dot(x, w1_ref[0, :, sl])
            h3 = _dot(x, w3_ref[0, :, sl])
            p = _dot((_silu(h1) * h3).astype(BF16), w2_ref[0, sl, :])
            part = p if part is None else part + p

        @pl.when(f == 0)
        def _():
            acc_ref[...] = part

        @pl.when(f > 0)
        def _():
            acc_ref[...] += part

    @pl.when(f == pl.num_programs(1) - 1)
    def _():
        @pl.when(used)
        def _():
            o_ref[...] = (acc_ref[...] * gacc[:, 0:1]).astype(BF16)

        @pl.when(jnp.logical_not(used))
        def _():
            o_ref[...] = jnp.zeros(o_ref.shape, BF16)


def _moe_grouped(hn, rankt, gatet, tile_expert, n_used, wlo, nwin, rbase, sublo, subhi, nt, w1, w3, w2):
    T, D = hn.shape
    nbp, _, gw = rankt.shape
    E, _, ff = w1.shape
    tm, tf = MOE_TILE, MOE_FF_TILE
    assert ff % tf == 0 and tf % MOE_FF_CHUNK == 0 and tm % MOE_ROW_BLOCK == 0
    nf = ff // tf

    def fsel(i, f, nu):
        return jnp.where(i < nu[0], f, nf - 1)

    grid_spec = pltpu.PrefetchScalarGridSpec(
        num_scalar_prefetch=7,
        grid=(nt, nf),
        in_specs=[pl.BlockSpec(memory_space=pl.ANY),
                  pl.BlockSpec((nbp, N_EXPERTS, gw), lambda i, f, *_: (0, 0, 0), pipeline_mode=pl.Buffered(1)),
                  pl.BlockSpec((nbp, N_EXPERTS, gw), lambda i, f, *_: (0, 0, 0), pipeline_mode=pl.Buffered(1)),
                  pl.BlockSpec((1, D, tf), lambda i, f, te, nu, *_: (te[i], 0, fsel(i, f, nu))),
                  pl.BlockSpec((1, D, tf), lambda i, f, te, nu, *_: (te[i], 0, fsel(i, f, nu))),
                  pl.BlockSpec((1, tf, D), lambda i, f, te, nu, *_: (te[i], fsel(i, f, nu), 0))],
        out_specs=pl.BlockSpec((tm, D), lambda i, f, *_: (i, 0)),
        scratch_shapes=[pltpu.VMEM(((MOE_WIN_BUFS + MOE_GATHER_WINS - 1) * gw, D), BF16),
                        pltpu.SemaphoreType.DMA((MOE_WIN_BUFS,)),
                        pltpu.VMEM((tm, D), F32),
                        pltpu.VMEM((tm, LANES), F32),
                        pltpu.VMEM((tm, D), BF16),
                        pltpu.VMEM((tm, D), F32)],
    )
    return pl.pallas_call(
        _moe_kernel,
        grid_spec=grid_spec,
        out_shape=jax.ShapeDtypeStruct((nt * tm, D), BF16),
        compiler_params=pltpu.CompilerParams(
            dimension_semantics=("arbitrary", "arbitrary"), vmem_limit_bytes=VMEM_LIMIT),
        name="moe_ffn",
    )(tile_expert, n_used, wlo, nwin, rbase, sublo, subhi, hn, rankt, gatet, w1, w3, w2)


def _combine_kernel(ws_ref, nblk_ref, nround_ref,
                    x_ref, pos_ref, g_ref, ys_hbm, o_ref, ybuf, sem):
    j = pl.program_id(0)
    nj = pl.num_programs(0)
    tt = x_ref.shape[0]
    wc = COMBINE_WIN
    extra_slot = 2

    def window_copies(tile, rnd, slot):
        cps = []
        for e in range(N_EXPERTS):
            t = tile * N_EXPERTS + e
            s = ws_ref[t] + jnp.where(rnd < nblk_ref[t], rnd, 0) * wc
            cps.append(pltpu.make_async_copy(ys_hbm.at[pl.ds(pl.multiple_of(s, 16), wc), :],
                                             ybuf.at[slot, pl.ds(e * wc, wc), :], sem.at[slot]))
        return cps

    def select(rnd, slot):
        pos = pos_ref[...]
        lane = lax.broadcasted_iota(jnp.int32, (tt, wc), 1)
        parts = []
        for e in range(N_EXPERTS):
            t = j * N_EXPERTS + e
            base = jnp.where(rnd < nblk_ref[t], ws_ref[t] + rnd * wc, -2 * wc)
            parts.append(jnp.where(pos[:, e:e + 1] == base + lane, 1.0, 0.0).astype(BF16))
        return _dot(jnp.concatenate(parts, axis=1), ybuf[slot])

    @pl.when(j == 0)
    def _():
        for cp in window_copies(0, 0, 0):
            cp.start()

    @pl.when(j + 1 < nj)
    def _():
        for cp in window_copies(j + 1, 0, (j + 1) % 2):
            cp.start()

    for cp in window_copies(j, 0, j % 2):
        cp.wait()
    acc = select(0, j % 2)

    def round_body(rnd, acc):
        for cp in window_copies(j, rnd, extra_slot):
            cp.start()
        for cp in window_copies(j, rnd, extra_slot):
            cp.wait()
        return acc + select(rnd, extra_slot)

    acc = lax.fori_loop(1, nround_ref[j], round_body, acc)
    x = x_ref[...] + acc
    o_ref[...] = x * _rms_scale(x) * g_ref[...]


def _combine(x2, pos, ys, ws, nblk, nround, g, tt):
    T, D = x2.shape
    grid_spec = pltpu.PrefetchScalarGridSpec(
        num_scalar_prefetch=3,
        grid=(T // tt,),
        in_specs=[pl.BlockSpec((tt, D), lambda j, *_: (j, 0)),
                  pl.BlockSpec((tt, N_EXPERTS), lambda j, *_: (j, 0)),
                  pl.BlockSpec((1, D), lambda j, *_: (0, 0)),
                  pl.BlockSpec(memory_space=pl.ANY)],
        out_specs=pl.BlockSpec((tt, D), lambda j, *_: (j, 0)),
        scratch_shapes=[pltpu.VMEM((3, N_EXPERTS * COMBINE_WIN, D), BF16),
                        pltpu.SemaphoreType.DMA((3,))],
    )
    return pl.pallas_call(
        _combine_kernel,
        grid_spec=grid_spec,
        out_shape=jax.ShapeDtypeStruct((T, D), F32),
        compiler_params=pltpu.CompilerParams(
            dimension_semantics=("arbitrary",), vmem_limit_bytes=VMEM_LIMIT),
        name="combine",
    )(ws, nblk, nround, x2, pos, g.reshape(1, D).astype(F32), ys)


def _moe_layer(x2, g, w_router, w1, w3, w2, final_g):
    T, D = x2.shape
    tm = MOE_TILE
    hn, rank, rankt, gatet, cnt = _router(x2, g, w_router)
    nb, _, gw = rankt.shape
    cb = jnp.concatenate([jnp.zeros((1, N_EXPERTS), jnp.int32),
                          cnt[:, 0, :N_EXPERTS].astype(jnp.int32)], axis=0)
    counts = cb[-1]
    padded = (counts + tm - 1) // tm * tm
    start = jnp.cumsum(padded) - padded
    nt = (2 * T) // tm + N_EXPERTS + 1
    n_used = (jnp.sum(padded) // tm).astype(jnp.int32)
    tile_lo = jnp.arange(nt, dtype=jnp.int32) * tm
    te = jnp.sum((tile_lo[:, None] >= (start + padded)[None, :]).astype(jnp.int32), axis=1)
    te = jnp.minimum(te, N_EXPERTS - 1).astype(jnp.int32)
    rbase = tile_lo - start[te]
    nvalid = jnp.clip(counts[te] - rbase, 0, tm)
    cb_after = cb[1:, :][:, te]
    wlo = jnp.sum((cb_after <= rbase[None, :]).astype(jnp.int32), axis=0)
    whi = jnp.sum((cb_after <= (rbase + nvalid - 1)[None, :]).astype(jnp.int32), axis=0)
    nwin = jnp.where(nvalid > 0, whi - wlo + 1, 0).astype(jnp.int32)
    wlo = jnp.minimum(wlo, nb - 1).astype(jnp.int32)
    nsub = tm // MOE_ROW_BLOCK
    r0 = rbase[:, None] + jnp.arange(nsub, dtype=jnp.int32)[None, :] * MOE_ROW_BLOCK
    nv = jnp.clip(counts[te][:, None] - r0, 0, MOE_ROW_BLOCK)
    sub_a = jnp.sum((cb_after[:, :, None] <= r0[None]).astype(jnp.int32), axis=0)
    sub_b = jnp.sum((cb_after[:, :, None] <= (r0 + nv - 1)[None]).astype(jnp.int32), axis=0)
    sublo = jnp.where(nv > 0, sub_a - wlo[:, None], 0).astype(jnp.int32)
    subhi = jnp.where(nv > 0, sub_b - wlo[:, None], -1).astype(jnp.int32)
    pad = MOE_GATHER_WINS - 1
    rankt_p = jnp.concatenate([rankt, jnp.full((pad, N_EXPERTS, gw), -1.0, F32)], axis=0)
    gatet_p = jnp.concatenate([gatet, jnp.zeros((pad, N_EXPERTS, gw), F32)], axis=0)
    ys = _moe_grouped(hn, rankt_p, gatet_p, te, n_used.reshape(1), wlo, nwin, rbase.astype(jnp.int32),
                      sublo.reshape(-1), subhi.reshape(-1), nt,
                      w1.astype(BF16), w3.astype(BF16), w2.astype(BF16))
    pos = jnp.where(rank >= 0, start[None, :] + rank, -1).astype(jnp.int32)
    first = start[None, :] + cb[:-1, :]
    need = cb[1:, :] - cb[:-1, :]
    ws = first // 16 * 16
    nblk = jnp.where(need > 0, (first - ws + need + COMBINE_WIN - 1) // COMBINE_WIN, 0)
    nround = jnp.maximum(jnp.max(nblk, axis=1), 1)
    return _combine(x2, pos, ys, ws.reshape(-1).astype(jnp.int32), nblk.reshape(-1).astype(jnp.int32),
                    nround.astype(jnp.int32), final_g, gw)


def kernel(x, mix_norm_g, w_in, conv_w, conv_b, w_q, w_k, w_v, w_if, b_if, skip_m, mh_norm_g, gm_v_g, w_sp, b_sp, gm_out_g, w_out, ffn_norm_g, dense_w1, dense_w3, dense_w2, moe_router, moe_w1, moe_w3, moe_w2, final_norm_g):
    B, S, D = x.shape
    depth = w_in.shape[0]
    assert depth == 2 and dense_w1.shape[0] == 1 and moe_w1.shape[0] == 1
    for l in range(depth):
        x = _mixer(x, mix_norm_g[l], w_in[l], conv_w[l], conv_b[l], w_q[l], w_k[l], w_v[l],
                   w_if[l], b_if[l], skip_m[l], mh_norm_g[l], gm_v_g[l], w_sp[l], b_sp[l],
                   gm_out_g[l], w_out[l])
        x2 = x.reshape(B * S, D)
        if l % 2 == 0:
            x = _ffn_dense(x2, ffn_norm_g[l], dense_w1[l // 2], dense_w3[l // 2],
                           dense_w2[l // 2]).reshape(B, S, D)
        else:
            x = _moe_layer(x2, ffn_norm_g[l], moe_router[l // 2], moe_w1[l // 2], moe_w3[l // 2],
                           moe_w2[l // 2], final_norm_g).reshape(B, S, D)
    return x
```

```python
import numpy as np
import jax
import jax.numpy as jnp
from jax import lax
from jax.experimental import pallas as pl
from jax.experimental.pallas import tpu as pltpu

F32 = jnp.float32
BF16 = jnp.bfloat16
EPS = 1e-6
NEG = -1e30

D_MODEL = 1024
M_HEADS = 4
HEAD_DIM = 128
M_WIDTH = M_HEADS * HEAD_DIM
G_GROUPS = 4
G_CH = 128
G_WIDTH = G_GROUPS * G_CH
G_BLOCK = 128
G_CHUNK = 64
CONV_W = 4
N_EXPERTS = 8
LANES = 128

MLSTM_CHUNK = 128
MIX_BLOCK = 512
MIX_BATCH = 2
MIX_STAGGER = 5
FFN_BLOCK = 512
FFN_CHUNK = 256
ROUTER_BLOCK = 512
TOKEN_WINDOW = 256
MOE_TILE = 512
MOE_FF_TILE = 1792
MOE_FF_CHUNK = 256
MOE_ROW_BLOCK = 128
MOE_GATHER_WINS = 4
MOE_WIN_BUFS = 12
COMBINE_WIN = 128
VMEM_LIMIT = 56 * 1024 * 1024


def _dot(a, b):
    return jnp.dot(a, b, preferred_element_type=F32)


def _dot_nt(a, b):
    return lax.dot_general(a, b, (((1,), (1,)), ((), ())), preferred_element_type=F32)


def _dot_tn(a, b):
    return lax.dot_general(a, b, (((0,), (0,)), ((), ())), preferred_element_type=F32)


def _rms_scale(x):
    return lax.rsqrt(jnp.mean(x * x, axis=-1, keepdims=True) + EPS)


def _sigmoid(x):
    return 1.0 / (1.0 + jnp.exp(-x))


def _silu(x):
    return x * _sigmoid(x)


def _gelu_tanh(x):
    c = np.float32(np.sqrt(2.0 / np.pi))
    return 0.5 * x * (1.0 + jnp.tanh(c * (x + 0.044715 * (x * x * x))))


def _log_sigmoid(x):
    return jnp.minimum(x, 0.0) - jnp.log1p(jnp.exp(-jnp.abs(x)))


def _split3(x):
    hi = x.astype(BF16)
    r1 = x - hi.astype(F32)
    mid = r1.astype(BF16)
    lo = (r1 - mid.astype(F32)).astype(BF16)
    return hi, mid, lo


def _mixer_kernel(x_ref, *refs):
    weights = refs[:16]
    o_ref = refs[16]
    xm_ext, ct_ref, n_ref, m_ref = refs[17:21]
    scratch = refs[17:]

    @pl.when(pl.program_id(1) == 0)
    def _():
        xm_ext[:, 0:8, :] = jnp.zeros((xm_ext.shape[0], 8, M_WIDTH), F32)
        ct_ref[...] = jnp.zeros(ct_ref.shape, F32)
        n_ref[...] = jnp.zeros(n_ref.shape, F32)
        m_ref[...] = jnp.zeros(m_ref.shape, F32)

    rows = [_mixer_block(x_ref.at[bb], *weights, o_ref.at[bb], *[r.at[bb] for r in scratch])
            for bb in range(x_ref.shape[0])]
    live = list(range(len(rows)))
    tick = 0
    while live:
        for bb in list(live):
            if tick >= bb * MIX_STAGGER and next(rows[bb], "done") == "done":
                live.remove(bb)
        tick += 1


def _mixer_block(x_ref, g_ref, win_ref, convw_ref, convb_ref, wqk_ref, wv_ref,
                 wif_ref, bif_ref, tcol_ref,
                 skip_ref, mhg_ref, gvg_ref, wsp_ref, bsp_ref, gog_ref, wout_ref,
                 o_ref,
                 xm_ext, ct_ref, n_ref, m_ref, y_ref, yg_ref):
    sb = x_ref.shape[0]
    L = MLSTM_CHUNK

    x = x_ref[...]
    h = (x * _rms_scale(x) * g_ref[...]).astype(BF16)
    yield
    x_m = _dot(h, win_ref[:, 0:M_WIDTH])
    yield

    xm_ext[8:8 + sb, :] = x_m
    acc = jnp.zeros((sb, M_WIDTH), F32) + convb_ref[...]
    for j in range(CONV_W):
        off = 8 - (CONV_W - 1) + j
        acc = acc + convw_ref[j:j + 1, :] * xm_ext[off:off + sb, :]
    x_c = _silu(acc)
    xm_ext[0:8, :] = xm_ext[sb:sb + 8, :]
    xc_b = x_c.astype(BF16)
    xm_b = x_m.astype(BF16)
    yield
    gm_v = _dot(h, win_ref[:, 2 * M_WIDTH + G_WIDTH:])
    yield

    qs, ks, vs = [], [], []
    for hd in range(M_HEADS):
        sl = slice(hd * HEAD_DIM, (hd + 1) * HEAD_DIM)
        qk = _dot(xc_b[:, sl], wqk_ref[hd])
        qs.append(qk[:, 0:HEAD_DIM])
        ks.append(qk[:, HEAD_DIM:])
    for pr in range(M_HEADS // 2):
        sl = slice(2 * pr * HEAD_DIM, 2 * (pr + 1) * HEAD_DIM)
        vv = _dot(xm_b[:, sl], wv_ref[pr])
        vs.append(vv[:, 0:HEAD_DIM])
        vs.append(vv[:, HEAD_DIM:])
    yield
    qkv_b = jnp.concatenate([t.astype(BF16) for t in qs + ks + vs], axis=1)

    vg = _gelu_tanh(gm_v)
    mu = jnp.mean(vg, axis=1, keepdims=True)
    dev = vg - mu
    var = jnp.mean(dev * dev, axis=1, keepdims=True)
    vn = (dev * lax.rsqrt(var + EPS) * gvg_ref[...]).astype(BF16)
    yield

    gcol = _dot(qkv_b, wif_ref[...]) + bif_ref[...]
    gm_u = _dot(h, win_ref[:, 2 * M_WIDTH:2 * M_WIDTH + G_WIDTH])
    yield
    lf_col = _log_sigmoid(gcol)
    ch, cm, cl = _split3(lf_col)
    ug = _gelu_tanh(gm_u)
    yield
    bcol3 = _dot(tcol_ref[...], jnp.concatenate([ch, cm, cl], axis=1))
    bcol = bcol3[:, 0:LANES] + bcol3[:, LANES:2 * LANES] + bcol3[:, 2 * LANES:]
    grow = gcol.T
    brow = bcol.T
    yield

    for i in range(sb // G_BLOCK):
        rs = slice(i * G_BLOCK, (i + 1) * G_BLOCK)
        for gi in range(G_GROUPS):
            sl = slice(gi * G_CH, (gi + 1) * G_CH)
            sv = _dot(wsp_ref[gi], vn[rs, sl]) + bsp_ref[:, sl]
            yg_ref[rs, sl] = ug[rs, sl] * sv
    yield
    yg = yg_ref[...]
    y_ref[:, M_WIDTH:] = (yg * _rms_scale(yg) * gog_ref[...]).astype(BF16)
    o_pre = _dot(h, win_ref[:, M_WIDTH:2 * M_WIDTH])
    yield

    scale = np.float32(HEAD_DIM ** -0.5)
    o_gate = _sigmoid(o_pre)
    units = [(j, hd) for j in range(sb // L) for hd in range(M_HEADS)]
    rows = {u: slice(u[0] * L, (u[0] + 1) * L) for u in units}
    lanes = {u: slice(u[1] * HEAD_DIM, (u[1] + 1) * HEAD_DIM) for u in units}
    tril = (lax.broadcasted_iota(jnp.int32, (L, L), 1) <= lax.broadcasted_iota(jnp.int32, (L, L), 0))
    b_col = {u: bcol[rows[u], M_HEADS + u[1]:M_HEADS + u[1] + 1] for u in units}
    ig_col = {u: gcol[rows[u], u[1]:u[1] + 1] for u in units}
    c_row = {u: brow[M_HEADS + u[1]:M_HEADS + u[1] + 1, rows[u]] - grow[u[1]:u[1] + 1, rows[u]] for u in units}
    b_end = {u: b_col[u][L - 1:L, :] for u in units}
    c_max = {u: jnp.max(-c_row[u], axis=1, keepdims=True) for u in units}
    m_in, m_out = {}, {}
    for hd in range(M_HEADS):
        m = m_ref[hd][:, 0:1]
        for j in range(sb // L):
            m_in[(j, hd)] = m
            m = b_end[(j, hd)] + jnp.maximum(m, c_max[(j, hd)])
            m_out[(j, hd)] = m
        m_ref[hd] = jnp.broadcast_to(m, (1, LANES))
    yield

    d = {u: jnp.where(tril, b_col[u] - c_row[u], NEG) for u in units}
    d_max = {u: jnp.max(d[u], axis=1, keepdims=True) for u in units}
    yield
    w, a, m_t, a_prev, w_s = {}, {}, {}, {}, {}
    for u in units:
        inter = b_col[u] + m_in[u]
        m_t[u] = jnp.maximum(inter, d_max[u])
        w[u] = jnp.exp(d[u] - m_t[u])
        a[u] = jnp.exp(inter - m_t[u])
        a_prev[u] = jnp.exp(b_end[u] + m_in[u] - m_out[u])
        w_s[u] = jnp.exp(b_end[u] - m_out[u] - (b_col[u] - ig_col[u]))
    yield

    qf = {u: qs[u[1]][rows[u]] * scale for u in units}
    qb = {u: qf[u].astype(BF16) for u in units}
    kb = {u: ks[u[1]][rows[u]].astype(BF16) for u in units}
    vb = {u: vs[u[1]][rows[u]].astype(BF16) for u in units}
    qk = {u: _dot_nt(qb[u], kb[u]) for u in units}
    s = {u: qk[u] * w[u] for u in units}
    yield
    sv = {u: _dot(s[u].astype(BF16), vb[u]) for u in units}
    s_sum = {u: jnp.sum(s[u], axis=1, keepdims=True) for u in units}
    yield
    kw = {u: ks[u[1]][rows[u]] * w_s[u] for u in units}
    upd = {u: _dot_tn(kw[u].astype(BF16), vb[u]) for u in units}
    k_sum = {u: jnp.sum(kw[u], axis=0, keepdims=True) for u in units}
    yield
    ct_in, n_in = {}, {}
    for hd in range(M_HEADS):
        ct = ct_ref[hd]
        n = n_ref[hd]
        for j in range(sb // L):
            u = (j, hd)
            ct_in[u] = ct
            n_in[u] = n
            ct = a_prev[u] * ct + upd[u]
            n = a_prev[u] * n + k_sum[u]
        ct_ref[hd] = ct
        n_ref[hd] = n
    qc = {u: _dot(qb[u], ct_in[u].astype(BF16)) for u in units}
    qn = {u: jnp.sum(qf[u] * n_in[u], axis=1, keepdims=True) for u in units}
    yield
    hh = {}
    for u in units:
        num = a[u] * qc[u] + sv[u]
        den = jnp.maximum(jnp.abs(a[u] * qn[u] + s_sum[u]), jnp.exp(-m_t[u]))
        hh[u] = o_gate[rows[u], lanes[u]] * (num / den)
    mu = {u: jnp.mean(hh[u], axis=1, keepdims=True) for u in units}
    yield
    dev = {u: hh[u] - mu[u] for u in units}
    var = {u: jnp.mean(dev[u] * dev[u], axis=1, keepdims=True) for u in units}
    yield
    for u in units:
        y_m = (dev[u] * lax.rsqrt(var[u] + EPS) * mhg_ref[:, lanes[u]]
               + skip_ref[:, lanes[u]] * x_c[rows[u], lanes[u]])
        y_ref[rows[u], lanes[u]] = y_m.astype(BF16)
    yield

    o_ref[...] = x + _dot(y_ref[...], wout_ref[...])


def _const_spec(shape):
    nd = len(shape)
    return pl.BlockSpec(shape, lambda *_: (0,) * nd, pipeline_mode=pl.Buffered(1))


def _chunk_cumsum_matrix(n, chunk):
    i = np.arange(n)[:, None]
    j = np.arange(n)[None, :]
    return ((j <= i) & (i // chunk == j // chunk)).astype(np.float32)


def _mixer(x, g, w_in, conv_w, conv_b, w_q, w_k, w_v, w_if, b_if, skip_m, mh_norm_g,
           gm_v_g, w_sp, b_sp, gm_out_g, w_out):
    B, S, D = x.shape
    sb = min(MIX_BLOCK, S)
    assert S % sb == 0 and sb % MLSTM_CHUNK == 0 and sb % G_BLOCK == 0
    tcol = _chunk_cumsum_matrix(sb, MLSTM_CHUNK)
    wif = jnp.pad(w_if, ((0, 0), (0, LANES - 2 * M_HEADS)))
    bif = jnp.pad(b_if, (0, LANES - 2 * M_HEADS)).reshape(1, LANES)
    wqk = jnp.concatenate([w_q, w_k], axis=2)
    zero = jnp.zeros((HEAD_DIM, HEAD_DIM), w_v.dtype)
    wv2 = jnp.stack([jnp.block([[w_v[2 * p], zero], [zero, w_v[2 * p + 1]]]) for p in range(M_HEADS // 2)])
    pos = np.arange(G_BLOCK)
    chunk_mask = (pos[:, None] // G_CHUNK) >= (pos[None, :] // G_CHUNK)
    wsp = jnp.where(chunk_mask[None], w_sp, 0.0).astype(BF16)
    bsp = jnp.repeat(b_sp.T, G_CH, axis=1)
    row = lambda a: a.reshape(1, -1).astype(F32)
    args = (x, row(g), w_in.astype(BF16), conv_w.astype(F32), row(conv_b),
            wqk.astype(BF16), wv2.astype(BF16),
            wif.astype(BF16), bif.astype(F32),
            jnp.asarray(tcol, BF16),
            row(skip_m), row(mh_norm_g), row(gm_v_g), wsp, bsp.astype(F32), row(gm_out_g),
            w_out.astype(BF16))
    nbm = MIX_BATCH if B % MIX_BATCH == 0 else 1
    in_specs = [pl.BlockSpec((nbm, sb, D), lambda b, s: (b, s, 0))]
    in_specs += [_const_spec(a.shape) for a in args[1:]]
    return pl.pallas_call(
        _mixer_kernel,
        grid=(B // nbm, S // sb),
        in_specs=in_specs,
        out_specs=pl.BlockSpec((nbm, sb, D), lambda b, s: (b, s, 0)),
        out_shape=jax.ShapeDtypeStruct((B, S, D), F32),
        scratch_shapes=[
            pltpu.VMEM((nbm, sb + 8, M_WIDTH), F32),
            pltpu.VMEM((nbm, M_HEADS, HEAD_DIM, HEAD_DIM), F32),
            pltpu.VMEM((nbm, M_HEADS, 1, HEAD_DIM), F32),
            pltpu.VMEM((nbm, M_HEADS, 1, LANES), F32),
            pltpu.VMEM((nbm, sb, M_WIDTH + G_WIDTH), BF16),
            pltpu.VMEM((nbm, sb, G_WIDTH), F32),
        ],
        compiler_params=pltpu.CompilerParams(
            dimension_semantics=("arbitrary", "arbitrary"), vmem_limit_bytes=VMEM_LIMIT),
        name="mixer",
    )(*args)


def _ffn_kernel(x_ref, g_ref, w1_ref, w3_ref, w2_ref, o_ref):
    x = x_ref[...]
    hn = (x * _rms_scale(x) * g_ref[...]).astype(BF16)
    acc = x
    for c in range(w1_ref.shape[1] // FFN_CHUNK):
        sl = slice(c * FFN_CHUNK, (c + 1) * FFN_CHUNK)
        h1 = _dot(hn, w1_ref[:, sl])
        h3 = _dot(hn, w3_ref[:, sl])
        acc = acc + _dot((_silu(h1) * h3).astype(BF16), w2_ref[sl, :])
    o_ref[...] = acc


def _ffn_dense(x2, g, w1, w3, w2):
    T, D = x2.shape
    ff = w1.shape[1]
    ffp = -(-ff // FFN_CHUNK) * FFN_CHUNK
    w1p = jnp.pad(w1, ((0, 0), (0, ffp - ff))).astype(BF16)
    w3p = jnp.pad(w3, ((0, 0), (0, ffp - ff))).astype(BF16)
    w2p = jnp.pad(w2, ((0, ffp - ff), (0, 0))).astype(BF16)
    tm = min(FFN_BLOCK, T)
    assert T % tm == 0
    return pl.pallas_call(
        _ffn_kernel,
        grid=(T // tm,),
        in_specs=[pl.BlockSpec((tm, D), lambda i: (i, 0)),
                  _const_spec((1, D)), _const_spec((D, ffp)), _const_spec((D, ffp)), _const_spec((ffp, D))],
        out_specs=pl.BlockSpec((tm, D), lambda i: (i, 0)),
        out_shape=jax.ShapeDtypeStruct((T, D), F32),
        compiler_params=pltpu.CompilerParams(
            dimension_semantics=("arbitrary",), vmem_limit_bytes=VMEM_LIMIT),
        name="ffn_dense",
    )(x2, g.reshape(1, D).astype(F32), w1p, w3p, w2p)


def _router_kernel(x_ref, g_ref, wr_ref, tri_ref, hn_ref, rank_ref, rankt_ref, gatet_ref, cnt_ref,
                   carry_ref):
    @pl.when(pl.program_id(0) == 0)
    def _():
        carry_ref[...] = jnp.zeros(carry_ref.shape, F32)

    x = x_ref[...]
    hn = x * _rms_scale(x) * g_ref[...]
    hi = hn.astype(BF16)
    hn_ref[...] = hi
    tb = x.shape[0]
    lane = lax.broadcasted_iota(jnp.int32, (tb, LANES), 1).astype(F32)
    lo = (hn - hi.astype(F32)).astype(BF16)
    hw = _dot(hi, wr_ref[...])
    logits = hw[:, 0:LANES] + hw[:, LANES:] + _dot(lo, wr_ref[:, 0:LANES])
    logits = jnp.where(lane < float(N_EXPERTS), logits, NEG)
    m1 = jnp.max(logits, axis=1, keepdims=True)
    i1 = jnp.min(jnp.where(logits == m1, lane, float(LANES)), axis=1, keepdims=True)
    rest = jnp.where(lane == i1, NEG, logits)
    m2 = jnp.max(rest, axis=1, keepdims=True)
    i2 = jnp.min(jnp.where(rest == m2, lane, float(LANES)), axis=1, keepdims=True)
    r = jnp.exp(m2 - m1)
    g1 = 1.0 / (1.0 + r)
    g2 = r * g1
    sel1 = lane == i1
    sel2 = lane == i2
    gates = jnp.where(sel1, g1, jnp.where(sel2, g2, 0.0))
    sel = jnp.where(sel1, 1.0, jnp.where(sel2, 1.0, 0.0))
    before = _dot(tri_ref[...], sel.astype(BF16))
    rank = jnp.where(sel > 0.0, carry_ref[...] + before, -1.0)
    rank_ref[...] = rank[:, 0:N_EXPERTS].astype(jnp.int32)
    rank_t = rank.T
    gates_t = gates.T
    gw = rankt_ref.shape[2]
    carry = carry_ref[...]
    for wi in range(tb // gw):
        ws = slice(wi * gw, (wi + 1) * gw)
        rankt_ref[wi] = rank_t[0:N_EXPERTS, ws]
        gatet_ref[wi] = gates_t[0:N_EXPERTS, ws]
        carry = carry + jnp.sum(sel[ws, :], axis=0, keepdims=True)
        cnt_ref[wi] = carry
    carry_ref[...] = carry


def _router(x2, g, w_router):
    T, D = x2.shape
    tb = min(ROUTER_BLOCK, T)
    gw = min(TOKEN_WINDOW, tb)
    assert T % tb == 0 and tb % gw == 0
    nb = T // tb
    nwb = tb // gw
    tri = np.tril(np.ones((tb, tb), np.float32), -1)
    wr = jnp.pad(w_router.astype(F32), ((0, 0), (0, LANES - N_EXPERTS)))
    wh = wr.astype(BF16)
    wl = (wr - wh.astype(F32)).astype(BF16)
    return pl.pallas_call(
        _router_kernel,
        grid=(nb,),
        in_specs=[pl.BlockSpec((tb, D), lambda i: (i, 0)),
                  _const_spec((1, D)), _const_spec((D, 2 * LANES)), _const_spec((tb, tb))],
        out_specs=[pl.BlockSpec((tb, D), lambda i: (i, 0)),
                   pl.BlockSpec((tb, N_EXPERTS), lambda i: (i, 0)),
                   pl.BlockSpec((nwb, N_EXPERTS, gw), lambda i: (i, 0, 0)),
                   pl.BlockSpec((nwb, N_EXPERTS, gw), lambda i: (i, 0, 0)),
                   pl.BlockSpec((nwb, 1, LANES), lambda i: (i, 0, 0))],
        out_shape=[jax.ShapeDtypeStruct((T, D), BF16),
                   jax.ShapeDtypeStruct((T, N_EXPERTS), jnp.int32),
                   jax.ShapeDtypeStruct((T // gw, N_EXPERTS, gw), F32),
                   jax.ShapeDtypeStruct((T // gw, N_EXPERTS, gw), F32),
                   jax.ShapeDtypeStruct((T // gw, 1, LANES), F32)],
        scratch_shapes=[pltpu.VMEM((1, LANES), F32)],
        compiler_params=pltpu.CompilerParams(
            dimension_semantics=("arbitrary",), vmem_limit_bytes=VMEM_LIMIT),
        name="router",
    )(x2, g.reshape(1, D).astype(F32), jnp.concatenate([wh, wl], axis=1), jnp.asarray(tri, BF16))


def _moe_kernel(te_ref, nused_ref, wlo_ref, nwin_ref, rbase_ref, sublo_ref, subhi_ref,
                hn_hbm, rankt_ref, gatet_ref, w1_ref, w3_ref, w2_ref,
                o_ref,
                hbuf, sem, xacc, gacc, xs_ref, acc_ref):
    i = pl.program_id(0)
    f = pl.program_id(1)
    used = i < nused_ref[0]
    tm = o_ref.shape[0]
    gw = rankt_ref.shape[2]
    nbuf = MOE_WIN_BUFS
    kwin = MOE_GATHER_WINS
    rblk = MOE_ROW_BLOCK
    nsub = tm // rblk

    def window_copy(tile, k, slot):
        w = wlo_ref[tile] + k
        return pltpu.make_async_copy(hn_hbm.at[pl.ds(pl.multiple_of(w * gw, gw), gw), :],
                                     hbuf.at[pl.ds(slot * gw, gw), :], sem.at[slot])

    def start_windows(tile, k_lo):
        for k in range(nbuf):
            @pl.when(k_lo + k < nwin_ref[tile])
            def _():
                window_copy(tile, k_lo + k, k).start()

    @pl.when(jnp.logical_and(used, f == 0))
    def _dispatch():
        @pl.when(i == 0)
        def _():
            hbuf[...] = jnp.zeros(hbuf.shape, BF16)
            start_windows(0, 0)

        e = te_ref[i]
        rb = rbase_ref[i].astype(F32)
        nwin = nwin_ref[i]
        w0 = wlo_ref[i]
        def run_phase(p, first):
            k_lo = p * nbuf
            if not first:
                start_windows(i, k_lo)
            for k in range(nbuf):
                @pl.when(k_lo + k < nwin)
                def _():
                    window_copy(i, k_lo + k, k).wait()

            def select(s, c):
                k0 = jnp.maximum(sublo_ref[i * nsub + s], k_lo)
                k1 = jnp.minimum(subhi_ref[i * nsub + s], k_lo + nbuf - 1)
                kk = jnp.minimum(k0 + c * kwin, jnp.minimum(k_lo + nbuf, nwin) - 1)
                rel, gate = [], []
                for j in range(kwin):
                    live = jnp.logical_and(kk + j >= k0, kk + j <= k1)
                    rel.append(jnp.where(live, rankt_ref[w0 + kk + j, pl.ds(e, 1), :] - rb, -1.0))
                    gate.append(gatet_ref[w0 + kk + j, pl.ds(e, 1), :])
                rel = jnp.concatenate(rel, axis=1)
                gate = jnp.concatenate(gate, axis=1)
                rows = lax.broadcasted_iota(jnp.int32, (rblk, kwin * gw), 0).astype(F32) + float(s * rblk)
                hit = rel == rows
                gsum = jnp.sum(jnp.where(hit, gate, 0.0), axis=1, keepdims=True)
                src = hbuf[pl.ds(pl.multiple_of((kk - k_lo) * gw, gw), kwin * gw), :]
                return jnp.where(hit, 1.0, 0.0).astype(BF16), src, jnp.broadcast_to(gsum, (rblk, LANES))

            picks = [select(s, 0) for s in range(nsub)]
            prods = [_dot(onehot, src) for onehot, src, _ in picks]
            for s in range(nsub):
                rs = slice(s * rblk, (s + 1) * rblk)
                if first:
                    xacc[rs, :] = prods[s]
                    gacc[rs, :] = picks[s][2]
                else:
                    xacc[rs, :] += prods[s]
                    gacc[rs, :] += picks[s][2]

            for s in range(nsub):
                rs = slice(s * rblk, (s + 1) * rblk)
                k0 = jnp.maximum(sublo_ref[i * nsub + s], k_lo)
                k1 = jnp.minimum(subhi_ref[i * nsub + s], k_lo + nbuf - 1)

                def chunk_body(c, cc, s=s, rs=rs):
                    onehot, src, gsum = select(s, c)
                    xacc[rs, :] += _dot(onehot, src)
                    gacc[rs, :] += gsum
                    return cc

                lax.fori_loop(1, jnp.maximum((k1 - k0 + kwin) // kwin, 0), chunk_body, 0)

        run_phase(0, True)
        lax.fori_loop(1, (nwin + nbuf - 1) // nbuf, lambda p, c: (run_phase(p, False), c)[1], 0)
        xs_ref[...] = xacc[...].astype(BF16)

        @pl.when(i + 1 < nused_ref[0])
        def _():
            start_windows(i + 1, 0)

    @pl.when(used)
    def _ffn():
        x = xs_ref[...]
        part = None
        for c in range(w1_ref.shape[2] // MOE_FF_CHUNK):
            sl = slice(c * MOE_FF_CHUNK, (c + 1) * MOE_FF_CHUNK)
            h1 = _dot(x, w1_ref[0, :, sl])
            h3 = _dot(x, w3_ref[0, :, sl])
            p = _dot((_silu(h1) * h3).astype(BF16), w2_ref[0, sl, :])
            part = p if part is None else part + p

        @pl.when(f == 0)
        def _():
            acc_ref[...] = part

        @pl.when(f > 0)
        def _():
            acc_ref[...] += part

    @pl.when(f == pl.num_programs(1) - 1)
    def _():
        @pl.when(used)
        def _():
            o_ref[...] = (acc_ref[...] * gacc[:, 0:1]).astype(BF16)

        @pl.when(jnp.logical_not(used))
        def _():
            o_ref[...] = jnp.zeros(o_ref.shape, BF16)


def _moe_grouped(hn, rankt, gatet, tile_expert, n_used, wlo, nwin, rbase, sublo, subhi, nt, w1, w3, w2):
    T, D = hn.shape
    nbp, _, gw = rankt.shape
    E, _, ff = w1.shape
    tm, tf = MOE_TILE, MOE_FF_TILE
    assert ff % tf == 0 and tf % MOE_FF_CHUNK == 0 and tm % MOE_ROW_BLOCK == 0
    nf = ff // tf

    def fsel(i, f, nu):
        return jnp.where(i < nu[0], f, nf - 1)

    grid_spec = pltpu.PrefetchScalarGridSpec(
        num_scalar_prefetch=7,
        grid=(nt, nf),
        in_specs=[pl.BlockSpec(memory_space=pl.ANY),
                  pl.BlockSpec((nbp, N_EXPERTS, gw), lambda i, f, *_: (0, 0, 0), pipeline_mode=pl.Buffered(1)),
                  pl.BlockSpec((nbp, N_EXPERTS, gw), lambda i, f, *_: (0, 0, 0), pipeline_mode=pl.Buffered(1)),
                  pl.BlockSpec((1, D, tf), lambda i, f, te, nu, *_: (te[i], 0, fsel(i, f, nu))),
                  pl.BlockSpec((1, D, tf), lambda i, f, te, nu, *_: (te[i], 0, fsel(i, f, nu))),
                  pl.BlockSpec((1, tf, D), lambda i, f, te, nu, *_: (te[i], fsel(i, f, nu), 0))],
        out_specs=pl.BlockSpec((tm, D), lambda i, f, *_: (i, 0)),
        scratch_shapes=[pltpu.VMEM(((MOE_WIN_BUFS + MOE_GATHER_WINS - 1) * gw, D), BF16),
                        pltpu.SemaphoreType.DMA((MOE_WIN_BUFS,)),
                        pltpu.VMEM((tm, D), F32),
                        pltpu.VMEM((tm, LANES), F32),
                        pltpu.VMEM((tm, D), BF16),
                        pltpu.VMEM((tm, D), F32)],
    )
    return pl.pallas_call(
        _moe_kernel,
        grid_spec=grid_spec,
        out_shape=jax.ShapeDtypeStruct((nt * tm, D), BF16),
        compiler_params=pltpu.CompilerParams(
            dimension_semantics=("arbitrary", "arbitrary"), vmem_limit_bytes=VMEM_LIMIT),
        name="moe_ffn",
    )(tile_expert, n_used, wlo, nwin, rbase, sublo, subhi, hn, rankt, gatet, w1, w3, w2)


def _combine_kernel(ws_ref, nblk_ref, nround_ref,
                    x_ref, pos_ref, g_ref, ys_hbm, o_ref, ybuf, sem):
    j = pl.program_id(0)
    nj = pl.num_programs(0)
    tt = x_ref.shape[0]
    wc = COMBINE_WIN
    extra_slot = 2

    def window_copies(tile, rnd, slot):
        cps = []
        for e in range(N_EXPERTS):
            t = tile * N_EXPERTS + e
            s = ws_ref[t] + jnp.where(rnd < nblk_ref[t], rnd, 0) * wc
            cps.append(pltpu.make_async_copy(ys_hbm.at[pl.ds(pl.multiple_of(s, 16), wc), :],
                                             ybuf.at[slot, pl.ds(e * wc, wc), :], sem.at[slot]))
        return cps

    def select(rnd, slot):
        pos = pos_ref[...]
        lane = lax.broadcasted_iota(jnp.int32, (tt, wc), 1)
        parts = []
        for e in range(N_EXPERTS):
            t = j * N_EXPERTS + e
            base = jnp.where(rnd < nblk_ref[t], ws_ref[t] + rnd * wc, -2 * wc)
            parts.append(jnp.where(pos[:, e:e + 1] == base + lane, 1.0, 0.0).astype(BF16))
        return _dot(jnp.concatenate(parts, axis=1), ybuf[slot])

    @pl.when(j == 0)
    def _():
        for cp in window_copies(0, 0, 0):
            cp.start()

    @pl.when(j + 1 < nj)
    def _():
        for cp in window_copies(j + 1, 0, (j + 1) % 2):
            cp.start()

    for cp in window_copies(j, 0, j % 2):
        cp.wait()
    acc = select(0, j % 2)

    def round_body(rnd, acc):
        for cp in window_copies(j, rnd, extra_slot):
            cp.start()
        for cp in window_copies(j, rnd, extra_slot):
            cp.wait()
        return acc + select(rnd, extra_slot)

    acc = lax.fori_loop(1, nround_ref[j], round_body, acc)
    x = x_ref[...] + acc
    o_ref[...] = x * _rms_scale(x) * g_ref[...]


def _combine(x2, pos, ys, ws, nblk, nround, g, tt):
    T, D = x2.shape
    grid_spec = pltpu.PrefetchScalarGridSpec(
        num_scalar_prefetch=3,
        grid=(T // tt,),
        in_specs=[pl.BlockSpec((tt, D), lambda j, *_: (j, 0)),
                  pl.BlockSpec((tt, N_EXPERTS), lambda j, *_: (j, 0)),
                  pl.BlockSpec((1, D), lambda j, *_: (0, 0)),
                  pl.BlockSpec(memory_space=pl.ANY)],
        out_specs=pl.BlockSpec((tt, D), lambda j, *_: (j, 0)),
        scratch_shapes=[pltpu.VMEM((3, N_EXPERTS * COMBINE_WIN, D), BF16),
                        pltpu.SemaphoreType.DMA((3,))],
    )
    return pl.pallas_call(
        _combine_kernel,
        grid_spec=grid_spec,
        out_shape=jax.ShapeDtypeStruct((T, D), F32),
        compiler_params=pltpu.CompilerParams(
            dimension_semantics=("arbitrary",), vmem_limit_bytes=VMEM_LIMIT),
        name="combine",
    )(ws, nblk, nround, x2, pos, g.reshape(1, D).astype(F32), ys)


def _moe_layer(x2, g, w_router, w1, w3, w2, final_g):
    T, D = x2.shape
    tm = MOE_TILE
    hn, rank, rankt, gatet, cnt = _router(x2, g, w_router)
    nb, _, gw = rankt.shape
    cb = jnp.concatenate([jnp.zeros((1, N_EXPERTS), jnp.int32),
                          cnt[:, 0, :N_EXPERTS].astype(jnp.int32)], axis=0)
    counts = cb[-1]
    padded = (counts + tm - 1) // tm * tm
    start = jnp.cumsum(padded) - padded
    nt = (2 * T) // tm + N_EXPERTS + 1
    n_used = (jnp.sum(padded) // tm).astype(jnp.int32)
    tile_lo = jnp.arange(nt, dtype=jnp.int32) * tm
    te = jnp.sum((tile_lo[:, None] >= (start + padded)[None, :]).astype(jnp.int32), axis=1)
    te = jnp.minimum(te, N_EXPERTS - 1).astype(jnp.int32)
    rbase = tile_lo - start[te]
    nvalid = jnp.clip(counts[te] - rbase, 0, tm)
    cb_after = cb[1:, :][:, te]
    wlo = jnp.sum((cb_after <= rbase[None, :]).astype(jnp.int32), axis=0)
    whi = jnp.sum((cb_after <= (rbase + nvalid - 1)[None, :]).astype(jnp.int32), axis=0)
    nwin = jnp.where(nvalid > 0, whi - wlo + 1, 0).astype(jnp.int32)
    wlo = jnp.minimum(wlo, nb - 1).astype(jnp.int32)
    nsub = tm // MOE_ROW_BLOCK
    r0 = rbase[:, None] + jnp.arange(nsub, dtype=jnp.int32)[None, :] * MOE_ROW_BLOCK
    nv = jnp.clip(counts[te][:, None] - r0, 0, MOE_ROW_BLOCK)
    sub_a = jnp.sum((cb_after[:, :, None] <= r0[None]).astype(jnp.int32), axis=0)
    sub_b = jnp.sum((cb_after[:, :, None] <= (r0 + nv - 1)[None]).astype(jnp.int32), axis=0)
    sublo = jnp.where(nv > 0, sub_a - wlo[:, None], 0).astype(jnp.int32)
    subhi = jnp.where(nv > 0, sub_b - wlo[:, None], -1).astype(jnp.int32)
    pad = MOE_GATHER_WINS - 1
    rankt_p = jnp.concatenate([rankt, jnp.full((pad, N_EXPERTS, gw), -1.0, F32)], axis=0)
    gatet_p = jnp.concatenate([gatet, jnp.zeros((pad, N_EXPERTS, gw), F32)], axis=0)
    ys = _moe_grouped(hn, rankt_p, gatet_p, te, n_used.reshape(1), wlo, nwin, rbase.astype(jnp.int32),
                      sublo.reshape(-1), subhi.reshape(-1), nt,
                      w1.astype(BF16), w3.astype(BF16), w2.astype(BF16))
    pos = jnp.where(rank >= 0, start[None, :] + rank, -1).astype(jnp.int32)
    first = start[None, :] + cb[:-1, :]
    need = cb[1:, :] - cb[:-1, :]
    ws = first // 16 * 16
    nblk = jnp.where(need > 0, (first - ws + need + COMBINE_WIN - 1) // COMBINE_WIN, 0)
    nround = jnp.maximum(jnp.max(nblk, axis=1), 1)
    return _combine(x2, pos, ys, ws.reshape(-1).astype(jnp.int32), nblk.reshape(-1).astype(jnp.int32),
                    nround.astype(jnp.int32), final_g, gw)


def kernel(x, mix_norm_g, w_in, conv_w, conv_b, w_q, w_k, w_v, w_if, b_if, skip_m, mh_norm_g, gm_v_g, w_sp, b_sp, gm_out_g, w_out, ffn_norm_g, dense_w1, dense_w3, dense_w2, moe_router, moe_w1, moe_w3, moe_w2, final_norm_g):
    B, S, D = x.shape
    depth = w_in.shape[0]
    assert depth == 2 and dense_w1.shape[0] == 1 and moe_w1.shape[0] == 1
    for l in range(depth):
        x = _mixer(x, mix_norm_g[l], w_in[l], conv_w[l], conv_b[l], w_q[l], w_k[l], w_v[l],
                   w_if[l], b_if[l], skip_m[l], mh_norm_g[l], gm_v_g[l], w_sp[l], b_sp[l],
                   gm_out_g[l], w_out[l])
        x2 = x.reshape(B * S, D)
        if l % 2 == 0:
            x = _ffn_dense(x2, ffn_norm_g[l], dense_w1[l // 2], dense_w3[l // 2],
                           dense_w2[l // 2]).reshape(B, S, D)
        else:
            x = _moe_layer(x2, ffn_norm_g[l], moe_router[l // 2], moe_w1[l // 2], moe_w3[l // 2],
                           moe_w2[l // 2], final_norm_g).reshape(B, S, D)
    return x
```

```python
import numpy as np
import jax
import jax.numpy as jnp
from jax import lax
from jax.experimental import pallas as pl
from jax.experimental.pallas import tpu as pltpu

F32 = jnp.float32
BF16 = jnp.bfloat16
EPS = 1e-6
NEG = -1e30

D_MODEL = 1024
M_HEADS = 4
HEAD_DIM = 128
M_WIDTH = M_HEADS * HEAD_DIM
G_GROUPS = 4
G_CH = 128
G_WIDTH = G_GROUPS * G_CH
G_BLOCK = 128
G_CHUNK = 64
CONV_W = 4
N_EXPERTS = 8
LANES = 128

MLSTM_CHUNK = 128
MIX_BLOCK = 512
MIX_BATCH = 2
MIX_STAGGER = 12
FFN_BLOCK = 512
FFN_CHUNK = 256
ROUTER_BLOCK = 512
TOKEN_WINDOW = 256
MOE_TILE = 512
MOE_FF_TILE = 1792
MOE_FF_CHUNK = 256
MOE_ROW_BLOCK = 128
MOE_GATHER_WINS = 4
MOE_WIN_BUFS = 12
COMBINE_TILE = 512
COMBINE_WIN = 256
VMEM_LIMIT = 56 * 1024 * 1024


def _dot(a, b):
    return jnp.dot(a, b, preferred_element_type=F32)


def _dot_nt(a, b):
    return lax.dot_general(a, b, (((1,), (1,)), ((), ())), preferred_element_type=F32)


def _dot_tn(a, b):
    return lax.dot_general(a, b, (((0,), (0,)), ((), ())), preferred_element_type=F32)


def _rms_scale(x):
    return lax.rsqrt(jnp.mean(x * x, axis=-1, keepdims=True) + EPS)


def _sigmoid(x):
    return 1.0 / (1.0 + jnp.exp(-x))


def _silu(x):
    return x * _sigmoid(x)


def _gelu_tanh(x):
    c = np.float32(np.sqrt(2.0 / np.pi))
    return 0.5 * x * (1.0 + jnp.tanh(c * (x + 0.044715 * (x * x * x))))


def _log_sigmoid(x):
    return jnp.minimum(x, 0.0) - jnp.log1p(jnp.exp(-jnp.abs(x)))


def _split3(x):
    hi = x.astype(BF16)
    r1 = x - hi.astype(F32)
    mid = r1.astype(BF16)
    lo = (r1 - mid.astype(F32)).astype(BF16)
    return hi, mid, lo


def _mixer_kernel(x_ref, *refs):
    weights = refs[:16]
    o_ref = refs[16]
    xm_ext, ct_ref, n_ref, m_ref = refs[17:21]
    scratch = refs[17:]

    @pl.when(pl.program_id(1) == 0)
    def _():
        xm_ext[:, 0:8, :] = jnp.zeros((xm_ext.shape[0], 8, M_WIDTH), F32)
        ct_ref[...] = jnp.zeros(ct_ref.shape, F32)
        n_ref[...] = jnp.zeros(n_ref.shape, F32)
        m_ref[...] = jnp.zeros(m_ref.shape, F32)

    rows = [_mixer_block(x_ref.at[bb], *weights, o_ref.at[bb], *[r.at[bb] for r in scratch])
            for bb in range(x_ref.shape[0])]
    live = list(range(len(rows)))
    tick = 0
    while live:
        for bb in list(live):
            if tick >= bb * MIX_STAGGER and next(rows[bb], "done") == "done":
                live.remove(bb)
        tick += 1


def _mixer_block(x_ref, g_ref, win_ref, convw_ref, convb_ref, wqk_ref, wv_ref,
                 wif_ref, bif_ref, tcol_ref,
                 skip_ref, mhg_ref, gvg_ref, wsp_ref, bsp_ref, gog_ref, wout_ref,
                 o_ref,
                 xm_ext, ct_ref, n_ref, m_ref, y_ref, yg_ref):
    sb = x_ref.shape[0]
    L = MLSTM_CHUNK

    x = x_ref[...]
    h = (x * _rms_scale(x) * g_ref[...]).astype(BF16)
    yield
    x_m = _dot(h, win_ref[:, 0:M_WIDTH])
    yield

    xm_ext[8:8 + sb, :] = x_m
    acc = jnp.zeros((sb, M_WIDTH), F32) + convb_ref[...]
    for j in range(CONV_W):
        off = 8 - (CONV_W - 1) + j
        acc = acc + convw_ref[j:j + 1, :] * xm_ext[off:off + sb, :]
    x_c = _silu(acc)
    xm_ext[0:8, :] = xm_ext[sb:sb + 8, :]
    xc_b = x_c.astype(BF16)
    xm_b = x_m.astype(BF16)
    yield
    gm_v = _dot(h, win_ref[:, 2 * M_WIDTH + G_WIDTH:])
    yield

    qs, ks, vs = [], [], []
    for hd in range(M_HEADS):
        sl = slice(hd * HEAD_DIM, (hd + 1) * HEAD_DIM)
        qk = _dot(xc_b[:, sl], wqk_ref[hd])
        qs.append(qk[:, 0:HEAD_DIM])
        ks.append(qk[:, HEAD_DIM:])
    for pr in range(M_HEADS // 2):
        sl = slice(2 * pr * HEAD_DIM, 2 * (pr + 1) * HEAD_DIM)
        vv = _dot(xm_b[:, sl], wv_ref[pr])
        vs.append(vv[:, 0:HEAD_DIM])
        vs.append(vv[:, HEAD_DIM:])
    yield
    qkv_b = jnp.concatenate([t.astype(BF16) for t in qs + ks + vs], axis=1)

    vg = _gelu_tanh(gm_v)
    mu = jnp.mean(vg, axis=1, keepdims=True)
    dev = vg - mu
    var = jnp.mean(dev * dev, axis=1, keepdims=True)
    vn = (dev * lax.rsqrt(var + EPS) * gvg_ref[...]).astype(BF16)
    yield

    gcol = _dot(qkv_b, wif_ref[...]) + bif_ref[...]
    gm_u = _dot(h, win_ref[:, 2 * M_WIDTH:2 * M_WIDTH + G_WIDTH])
    yield
    lf_col = _log_sigmoid(gcol)
    ch, cm, cl = _split3(lf_col)
    ug = _gelu_tanh(gm_u)
    yield
    bcol3 = _dot(tcol_ref[...], jnp.concatenate([ch, cm, cl], axis=1))
    bcol = bcol3[:, 0:LANES] + bcol3[:, LANES:2 * LANES] + bcol3[:, 2 * LANES:]
    grow = gcol.T
    brow = bcol.T
    yield

    for i in range(sb // G_BLOCK):
        rs = slice(i * G_BLOCK, (i + 1) * G_BLOCK)
        for gi in range(G_GROUPS):
            sl = slice(gi * G_CH, (gi + 1) * G_CH)
            sv = _dot(wsp_ref[gi], vn[rs, sl]) + bsp_ref[:, sl]
            yg_ref[rs, sl] = ug[rs, sl] * sv
    yield
    yg = yg_ref[...]
    y_ref[:, M_WIDTH:] = (yg * _rms_scale(yg) * gog_ref[...]).astype(BF16)
    o_pre = _dot(h, win_ref[:, M_WIDTH:2 * M_WIDTH])
    yield

    scale = np.float32(HEAD_DIM ** -0.5)
    o_gate = _sigmoid(o_pre)
    units = [(j, hd) for j in range(sb // L) for hd in range(M_HEADS)]
    rows = {u: slice(u[0] * L, (u[0] + 1) * L) for u in units}
    lanes = {u: slice(u[1] * HEAD_DIM, (u[1] + 1) * HEAD_DIM) for u in units}
    tril = (lax.broadcasted_iota(jnp.int32, (L, L), 1) <= lax.broadcasted_iota(jnp.int32, (L, L), 0))
    b_col = {u: bcol[rows[u], M_HEADS + u[1]:M_HEADS + u[1] + 1] for u in units}
    ig_col = {u: gcol[rows[u], u[1]:u[1] + 1] for u in units}
    c_row = {u: brow[M_HEADS + u[1]:M_HEADS + u[1] + 1, rows[u]] - grow[u[1]:u[1] + 1, rows[u]] for u in units}
    b_end = {u: b_col[u][L - 1:L, :] for u in units}
    c_max = {u: jnp.max(-c_row[u], axis=1, keepdims=True) for u in units}
    m_in, m_out = {}, {}
    for hd in range(M_HEADS):
        m = m_ref[hd][:, 0:1]
        for j in range(sb // L):
            m_in[(j, hd)] = m
            m = b_end[(j, hd)] + jnp.maximum(m, c_max[(j, hd)])
            m_out[(j, hd)] = m
        m_ref[hd] = jnp.broadcast_to(m, (1, LANES))
    yield

    d = {u: jnp.where(tril, b_col[u] - c_row[u], NEG) for u in units}
    d_max = {u: jnp.max(d[u], axis=1, keepdims=True) for u in units}
    yield
    w, a, m_t, a_prev, w_s = {}, {}, {}, {}, {}
    for u in units:
        inter = b_col[u] + m_in[u]
        m_t[u] = jnp.maximum(inter, d_max[u])
        w[u] = jnp.exp(d[u] - m_t[u])
        a[u] = jnp.exp(inter - m_t[u])
        a_prev[u] = jnp.exp(b_end[u] + m_in[u] - m_out[u])
        w_s[u] = jnp.exp(b_end[u] - m_out[u] - (b_col[u] - ig_col[u]))
    yield

    qf = {u: qs[u[1]][rows[u]] * scale for u in units}
    qb = {u: qf[u].astype(BF16) for u in units}
    kb = {u: ks[u[1]][rows[u]].astype(BF16) for u in units}
    vb = {u: vs[u[1]][rows[u]].astype(BF16) for u in units}
    qk = {u: _dot_nt(qb[u], kb[u]) for u in units}
    s = {u: qk[u] * w[u] for u in units}
    yield
    sv = {u: _dot(s[u].astype(BF16), vb[u]) for u in units}
    s_sum = {u: jnp.sum(s[u], axis=1, keepdims=True) for u in units}
    yield
    kw = {u: ks[u[1]][rows[u]] * w_s[u] for u in units}
    upd = {u: _dot_tn(kw[u].astype(BF16), vb[u]) for u in units}
    k_sum = {u: jnp.sum(kw[u], axis=0, keepdims=True) for u in units}
    yield
    ct_in, n_in = {}, {}
    for hd in range(M_HEADS):
        ct = ct_ref[hd]
        n = n_ref[hd]
        for j in range(sb // L):
            u = (j, hd)
            ct_in[u] = ct
            n_in[u] = n
            ct = a_prev[u] * ct + upd[u]
            n = a_prev[u] * n + k_sum[u]
        ct_ref[hd] = ct
        n_ref[hd] = n
    qc = {u: _dot(qb[u], ct_in[u].astype(BF16)) for u in units}
    qn = {u: jnp.sum(qf[u] * n_in[u], axis=1, keepdims=True) for u in units}
    yield
    hh = {}
    for u in units:
        num = a[u] * qc[u] + sv[u]
        den = jnp.maximum(jnp.abs(a[u] * qn[u] + s_sum[u]), jnp.exp(-m_t[u]))
        hh[u] = o_gate[rows[u], lanes[u]] * (num / den)
    mu = {u: jnp.mean(hh[u], axis=1, keepdims=True) for u in units}
    yield
    dev = {u: hh[u] - mu[u] for u in units}
    var = {u: jnp.mean(dev[u] * dev[u], axis=1, keepdims=True) for u in units}
    yield
    for u in units:
        y_m = (dev[u] * lax.rsqrt(var[u] + EPS) * mhg_ref[:, lanes[u]]
               + skip_ref[:, lanes[u]] * x_c[rows[u], lanes[u]])
        y_ref[rows[u], lanes[u]] = y_m.astype(BF16)
    yield

    o_ref[...] = x + _dot(y_ref[...], wout_ref[...])


def _const_spec(shape):
    nd = len(shape)
    return pl.BlockSpec(shape, lambda *_: (0,) * nd, pipeline_mode=pl.Buffered(1))


def _chunk_cumsum_matrix(n, chunk):
    i = np.arange(n)[:, None]
    j = np.arange(n)[None, :]
    return ((j <= i) & (i // chunk == j // chunk)).astype(np.float32)


def _mixer(x, g, w_in, conv_w, conv_b, w_q, w_k, w_v, w_if, b_if, skip_m, mh_norm_g,
           gm_v_g, w_sp, b_sp, gm_out_g, w_out):
    B, S, D = x.shape
    sb = min(MIX_BLOCK, S)
    assert S % sb == 0 and sb % MLSTM_CHUNK == 0 and sb % G_BLOCK == 0
    tcol = _chunk_cumsum_matrix(sb, MLSTM_CHUNK)
    wif = jnp.pad(w_if, ((0, 0), (0, LANES - 2 * M_HEADS)))
    bif = jnp.pad(b_if, (0, LANES - 2 * M_HEADS)).reshape(1, LANES)
    wqk = jnp.concatenate([w_q, w_k], axis=2)
    zero = jnp.zeros((HEAD_DIM, HEAD_DIM), w_v.dtype)
    wv2 = jnp.stack([jnp.block([[w_v[2 * p], zero], [zero, w_v[2 * p + 1]]]) for p in range(M_HEADS // 2)])
    pos = np.arange(G_BLOCK)
    chunk_mask = (pos[:, None] // G_CHUNK) >= (pos[None, :] // G_CHUNK)
    wsp = jnp.where(chunk_mask[None], w_sp, 0.0).astype(BF16)
    bsp = jnp.repeat(b_sp.T, G_CH, axis=1)
    row = lambda a: a.reshape(1, -1).astype(F32)
    args = (x, row(g), w_in.astype(BF16), conv_w.astype(F32), row(conv_b),
            wqk.astype(BF16), wv2.astype(BF16),
            wif.astype(BF16), bif.astype(F32),
            jnp.asarray(tcol, BF16),
            row(skip_m), row(mh_norm_g), row(gm_v_g), wsp, bsp.astype(F32), row(gm_out_g),
            w_out.astype(BF16))
    nbm = MIX_BATCH if B % MIX_BATCH == 0 else 1
    in_specs = [pl.BlockSpec((nbm, sb, D), lambda b, s: (b, s, 0))]
    in_specs += [_const_spec(a.shape) for a in args[1:]]
    return pl.pallas_call(
        _mixer_kernel,
        grid=(B // nbm, S // sb),
        in_specs=in_specs,
        out_specs=pl.BlockSpec((nbm, sb, D), lambda b, s: (b, s, 0)),
        out_shape=jax.ShapeDtypeStruct((B, S, D), F32),
        scratch_shapes=[
            pltpu.VMEM((nbm, sb + 8, M_WIDTH), F32),
            pltpu.VMEM((nbm, M_HEADS, HEAD_DIM, HEAD_DIM), F32),
            pltpu.VMEM((nbm, M_HEADS, 1, HEAD_DIM), F32),
            pltpu.VMEM((nbm, M_HEADS, 1, LANES), F32),
            pltpu.VMEM((nbm, sb, M_WIDTH + G_WIDTH), BF16),
            pltpu.VMEM((nbm, sb, G_WIDTH), F32),
        ],
        compiler_params=pltpu.CompilerParams(
            dimension_semantics=("arbitrary", "arbitrary"), vmem_limit_bytes=VMEM_LIMIT),
        name="mixer",
    )(*args)


def _ffn_kernel(x_ref, g_ref, w1_ref, w3_ref, w2_ref, o_ref):
    x = x_ref[...]
    hn = (x * _rms_scale(x) * g_ref[...]).astype(BF16)
    acc = x
    for c in range(w1_ref.shape[1] // FFN_CHUNK):
        sl = slice(c * FFN_CHUNK, (c + 1) * FFN_CHUNK)
        h1 = _dot(hn, w1_ref[:, sl])
        h3 = _dot(hn, w3_ref[:, sl])
        acc = acc + _dot((_silu(h1) * h3).astype(BF16), w2_ref[sl, :])
    o_ref[...] = acc


def _ffn_dense(x2, g, w1, w3, w2):
    T, D = x2.shape
    ff = w1.shape[1]
    ffp = -(-ff // FFN_CHUNK) * FFN_CHUNK
    w1p = jnp.pad(w1, ((0, 0), (0, ffp - ff))).astype(BF16)
    w3p = jnp.pad(w3, ((0, 0), (0, ffp - ff))).astype(BF16)
    w2p = jnp.pad(w2, ((0, ffp - ff), (0, 0))).astype(BF16)
    tm = min(FFN_BLOCK, T)
    assert T % tm == 0
    return pl.pallas_call(
        _ffn_kernel,
        grid=(T // tm,),
        in_specs=[pl.BlockSpec((tm, D), lambda i: (i, 0)),
                  _const_spec((1, D)), _const_spec((D, ffp)), _const_spec((D, ffp)), _const_spec((ffp, D))],
        out_specs=pl.BlockSpec((tm, D), lambda i: (i, 0)),
        out_shape=jax.ShapeDtypeStruct((T, D), F32),
        compiler_params=pltpu.CompilerParams(
            dimension_semantics=("arbitrary",), vmem_limit_bytes=VMEM_LIMIT),
        name="ffn_dense",
    )(x2, g.reshape(1, D).astype(F32), w1p, w3p, w2p)


def _router_kernel(x_ref, g_ref, wr_ref, tri_ref, hn_ref, rank_ref, rankt_ref, gatet_ref, cnt_ref,
                   carry_ref):
    @pl.when(pl.program_id(0) == 0)
    def _():
        carry_ref[...] = jnp.zeros(carry_ref.shape, F32)

    x = x_ref[...]
    hn = x * _rms_scale(x) * g_ref[...]
    hi = hn.astype(BF16)
    hn_ref[...] = hi
    tb = x.shape[0]
    lane = lax.broadcasted_iota(jnp.int32, (tb, LANES), 1).astype(F32)
    lo = (hn - hi.astype(F32)).astype(BF16)
    hw = _dot(hi, wr_ref[...])
    logits = hw[:, 0:LANES] + hw[:, LANES:] + _dot(lo, wr_ref[:, 0:LANES])
    logits = jnp.where(lane < float(N_EXPERTS), logits, NEG)
    m1 = jnp.max(logits, axis=1, keepdims=True)
    i1 = jnp.min(jnp.where(logits == m1, lane, float(LANES)), axis=1, keepdims=True)
    rest = jnp.where(lane == i1, NEG, logits)
    m2 = jnp.max(rest, axis=1, keepdims=True)
    i2 = jnp.min(jnp.where(rest == m2, lane, float(LANES)), axis=1, keepdims=True)
    r = jnp.exp(m2 - m1)
    g1 = 1.0 / (1.0 + r)
    g2 = r * g1
    sel1 = lane == i1
    sel2 = lane == i2
    gates = jnp.where(sel1, g1, jnp.where(sel2, g2, 0.0))
    sel = jnp.where(sel1, 1.0, jnp.where(sel2, 1.0, 0.0))
    before = _dot(tri_ref[...], sel.astype(BF16))
    rank = jnp.where(sel > 0.0, carry_ref[...] + before, -1.0)
    rank_ref[...] = rank[:, 0:N_EXPERTS].astype(jnp.int32)
    rank_t = rank.T
    gates_t = gates.T
    gw = rankt_ref.shape[2]
    carry = carry_ref[...]
    for wi in range(tb // gw):
        ws = slice(wi * gw, (wi + 1) * gw)
        rankt_ref[wi] = rank_t[0:N_EXPERTS, ws]
        gatet_ref[wi] = gates_t[0:N_EXPERTS, ws]
        carry = carry + jnp.sum(sel[ws, :], axis=0, keepdims=True)
        cnt_ref[wi] = carry
    carry_ref[...] = carry


def _router(x2, g, w_router):
    T, D = x2.shape
    tb = min(ROUTER_BLOCK, T)
    gw = min(TOKEN_WINDOW, tb)
    assert T % tb == 0 and tb % gw == 0
    nb = T // tb
    nwb = tb // gw
    tri = np.tril(np.ones((tb, tb), np.float32), -1)
    wr = jnp.pad(w_router.astype(F32), ((0, 0), (0, LANES - N_EXPERTS)))
    wh = wr.astype(BF16)
    wl = (wr - wh.astype(F32)).astype(BF16)
    return pl.pallas_call(
        _router_kernel,
        grid=(nb,),
        in_specs=[pl.BlockSpec((tb, D), lambda i: (i, 0)),
                  _const_spec((1, D)), _const_spec((D, 2 * LANES)), _const_spec((tb, tb))],
        out_specs=[pl.BlockSpec((tb, D), lambda i: (i, 0)),
                   pl.BlockSpec((tb, N_EXPERTS), lambda i: (i, 0)),
                   pl.BlockSpec((nwb, N_EXPERTS, gw), lambda i: (i, 0, 0)),
                   pl.BlockSpec((nwb, N_EXPERTS, gw), lambda i: (i, 0, 0)),
                   pl.BlockSpec((nwb, 1, LANES), lambda i: (i, 0, 0))],
        out_shape=[jax.ShapeDtypeStruct((T, D), BF16),
                   jax.ShapeDtypeStruct((T, N_EXPERTS), jnp.int32),
                   jax.ShapeDtypeStruct((T // gw, N_EXPERTS, gw), F32),
                   jax.ShapeDtypeStruct((T // gw, N_EXPERTS, gw), F32),
                   jax.ShapeDtypeStruct((T // gw, 1, LANES), F32)],
        scratch_shapes=[pltpu.VMEM((1, LANES), F32)],
        compiler_params=pltpu.CompilerParams(
            dimension_semantics=("arbitrary",), vmem_limit_bytes=VMEM_LIMIT),
        name="router",
    )(x2, g.reshape(1, D).astype(F32), jnp.concatenate([wh, wl], axis=1), jnp.asarray(tri, BF16))


def _moe_kernel(te_ref, nused_ref, wlo_ref, nwin_ref, rbase_ref, sublo_ref, subhi_ref,
                hn_hbm, rankt_ref, gatet_ref, w1_ref, w3_ref, w2_ref,
                o_ref,
                hbuf, sem, xacc, gacc, xs_ref, acc_ref):
    i = pl.program_id(0)
    f = pl.program_id(1)
    used = i < nused_ref[0]
    tm = o_ref.shape[0]
    gw = rankt_ref.shape[2]
    nbuf = MOE_WIN_BUFS
    kwin = MOE_GATHER_WINS
    rblk = MOE_ROW_BLOCK
    nsub = tm // rblk

    def window_copy(tile, k, slot):
        w = wlo_ref[tile] + k
        return pltpu.make_async_copy(hn_hbm.at[pl.ds(pl.multiple_of(w * gw, gw), gw), :],
                                     hbuf.at[pl.ds(slot * gw, gw), :], sem.at[slot])

    def start_windows(tile, k_lo):
        for k in range(nbuf):
            @pl.when(k_lo + k < nwin_ref[tile])
            def _():
                window_copy(tile, k_lo + k, k).start()

    @pl.when(jnp.logical_and(used, f == 0))
    def _dispatch():
        @pl.when(i == 0)
        def _():
            hbuf[...] = jnp.zeros(hbuf.shape, BF16)
            start_windows(0, 0)

        e = te_ref[i]
        rb = rbase_ref[i].astype(F32)
        nwin = nwin_ref[i]
        w0 = wlo_ref[i]
        last_w = rankt_ref.shape[0] - 1
        def run_phase(p, first):
            k_lo = p * nbuf
            if not first:
                start_windows(i, k_lo)
            for k in range(nbuf):
                @pl.when(k_lo + k < nwin)
                def _():
                    window_copy(i, k_lo + k, k).wait()

            def select(s, c):
                k0 = jnp.maximum(sublo_ref[i * nsub + s], k_lo)
                k1 = jnp.minimum(subhi_ref[i * nsub + s], k_lo + nbuf - 1)
                kk = jnp.minimum(k0 + c * kwin, jnp.minimum(k_lo + nbuf, nwin) - 1)
                rel, gate = [], []
                for j in range(kwin):
                    live = jnp.logical_and(kk + j >= k0, kk + j <= k1)
                    w = jnp.minimum(w0 + kk + j, last_w)
                    rel.append(jnp.where(live, rankt_ref[w, pl.ds(e, 1), :] - rb, -1.0))
                    gate.append(gatet_ref[w, pl.ds(e, 1), :])
                rel = jnp.concatenate(rel, axis=1)
                gate = jnp.concatenate(gate, axis=1)
                rows = lax.broadcasted_iota(jnp.int32, (rblk, kwin * gw), 0).astype(F32) + float(s * rblk)
                hit = rel == rows
                gsum = jnp.sum(jnp.where(hit, gate, 0.0), axis=1, keepdims=True)
                src = hbuf[pl.ds(pl.multiple_of((kk - k_lo) * gw, gw), kwin * gw), :]
                return jnp.where(hit, 1.0, 0.0).astype(BF16), src, jnp.broadcast_to(gsum, (rblk, LANES))

            picks = [select(s, 0) for s in range(nsub)]
            prods = [_dot(onehot, src) for onehot, src, _ in picks]
            for s in range(nsub):
                rs = slice(s * rblk, (s + 1) * rblk)
                if first:
                    xacc[rs, :] = prods[s]
                    gacc[rs, :] = picks[s][2]
                else:
                    xacc[rs, :] += prods[s]
                    gacc[rs, :] += picks[s][2]

            for s in range(nsub):
                rs = slice(s * rblk, (s + 1) * rblk)
                k0 = jnp.maximum(sublo_ref[i * nsub + s], k_lo)
                k1 = jnp.minimum(subhi_ref[i * nsub + s], k_lo + nbuf - 1)

                def chunk_body(c, cc, s=s, rs=rs):
                    onehot, src, gsum = select(s, c)
                    xacc[rs, :] += _dot(onehot, src)
                    gacc[rs, :] += gsum
                    return cc

                lax.fori_loop(1, jnp.maximum((k1 - k0 + kwin) // kwin, 0), chunk_body, 0)

        run_phase(0, True)
        lax.fori_loop(1, (nwin + nbuf - 1) // nbuf, lambda p, c: (run_phase(p, False), c)[1], 0)
        xs_ref[...] = xacc[...].astype(BF16)

        @pl.when(i + 1 < nused_ref[0])
        def _():
            start_windows(i + 1, 0)

    @pl.when(used)
    def _ffn():
        x = xs_ref[...]
        part = None
        for c in range(w1_ref.shape[2] // MOE_FF_CHUNK):
            sl = slice(c * MOE_FF_CHUNK, (c + 1) * MOE_FF_CHUNK)
            h1 = _dot(x, w1_ref[0, :, sl])
            h3 = _dot(x, w3_ref[0, :, sl])
            p = _dot((_silu(h1) * h3).astype(BF16), w2_ref[0, sl, :])
            part = p if part is None else part + p

        @pl.when(f == 0)
        def _():
            acc_ref[...] = part

        @pl.when(f > 0)
        def _():
            acc_ref[...] += part

    @pl.when(f == pl.num_programs(1) - 1)
    def _():
        @pl.when(used)
        def _():
            o_ref[...] = (acc_ref[...] * gacc[:, 0:1]).astype(BF16)

        @pl.when(jnp.logical_not(used))
        def _():
            o_ref[...] = jnp.zeros(o_ref.shape, BF16)


def _moe_grouped(hn, rankt, gatet, tile_expert, n_used, wlo, nwin, rbase, sublo, subhi, nt, w1, w3, w2):
    T, D = hn.shape
    nbp, _, gw = rankt.shape
    E, _, ff = w1.shape
    tm, tf = MOE_TILE, MOE_FF_TILE
    assert ff % tf == 0 and tf % MOE_FF_CHUNK == 0 and tm % MOE_ROW_BLOCK == 0
    nf = ff // tf

    def fsel(i, f, nu):
        return jnp.where(i < nu[0], f, nf - 1)

    grid_spec = pltpu.PrefetchScalarGridSpec(
        num_scalar_prefetch=7,
        grid=(nt, nf),
        in_specs=[pl.BlockSpec(memory_space=pl.ANY),
                  pl.BlockSpec((nbp, N_EXPERTS, gw), lambda i, f, *_: (0, 0, 0), pipeline_mode=pl.Buffered(1)),
                  pl.BlockSpec((nbp, N_EXPERTS, gw), lambda i, f, *_: (0, 0, 0), pipeline_mode=pl.Buffered(1)),
                  pl.BlockSpec((1, D, tf), lambda i, f, te, nu, *_: (te[i], 0, fsel(i, f, nu))),
                  pl.BlockSpec((1, D, tf), lambda i, f, te, nu, *_: (te[i], 0, fsel(i, f, nu))),
                  pl.BlockSpec((1, tf, D), lambda i, f, te, nu, *_: (te[i], fsel(i, f, nu), 0))],
        out_specs=pl.BlockSpec((tm, D), lambda i, f, *_: (i, 0)),
        scratch_shapes=[pltpu.VMEM(((MOE_WIN_BUFS + MOE_GATHER_WINS - 1) * gw, D), BF16),
                        pltpu.SemaphoreType.DMA((MOE_WIN_BUFS,)),
                        pltpu.VMEM((tm, D), F32),
                        pltpu.VMEM((tm, LANES), F32),
                        pltpu.VMEM((tm, D), BF16),
                        pltpu.VMEM((tm, D), F32)],
    )
    return pl.pallas_call(
        _moe_kernel,
        grid_spec=grid_spec,
        out_shape=jax.ShapeDtypeStruct((nt * tm, D), BF16),
        compiler_params=pltpu.CompilerParams(
            dimension_semantics=("arbitrary", "arbitrary"), vmem_limit_bytes=VMEM_LIMIT),
        name="moe_ffn",
    )(tile_expert, n_used, wlo, nwin, rbase, sublo, subhi, hn, rankt, gatet, w1, w3, w2)


def _combine_kernel(ws_ref, nblk_ref, nround_ref,
                    x_ref, rank_ref, start_ref, g_ref, ys_hbm, o_ref, ybuf, sem):
    j = pl.program_id(0)
    nj = pl.num_programs(0)
    tt = x_ref.shape[0]
    wc = COMBINE_WIN
    extra_slot = 2

    def window_copies(tile, rnd, slot):
        cps = []
        for e in range(N_EXPERTS):
            t = tile * N_EXPERTS + e
            s = ws_ref[t] + jnp.where(rnd < nblk_ref[t], rnd, 0) * wc
            cps.append(pltpu.make_async_copy(ys_hbm.at[pl.ds(pl.multiple_of(s, 16), wc), :],
                                             ybuf.at[slot, pl.ds(e * wc, wc), :], sem.at[slot]))
        return cps

    def select(rnd, slot):
        rank = rank_ref[...]
        pos = jnp.where(rank >= 0, rank + start_ref[...], -1)
        lane = lax.broadcasted_iota(jnp.int32, (tt, wc), 1)
        parts = []
        for e in range(N_EXPERTS):
            t = j * N_EXPERTS + e
            base = jnp.where(rnd < nblk_ref[t], ws_ref[t] + rnd * wc, -2 * wc)
            parts.append(jnp.where(pos[:, e:e + 1] == base + lane, 1.0, 0.0).astype(BF16))
        return _dot(jnp.concatenate(parts, axis=1), ybuf[slot])

    @pl.when(j == 0)
    def _():
        for cp in window_copies(0, 0, 0):
            cp.start()

    @pl.when(j + 1 < nj)
    def _():
        for cp in window_copies(j + 1, 0, (j + 1) % 2):
            cp.start()

    for cp in window_copies(j, 0, j % 2):
        cp.wait()
    acc = select(0, j % 2)

    def round_body(rnd, acc):
        for cp in window_copies(j, rnd, extra_slot):
            cp.start()
        for cp in window_copies(j, rnd, extra_slot):
            cp.wait()
        return acc + select(rnd, extra_slot)

    acc = lax.fori_loop(1, nround_ref[j], round_body, acc)
    x = x_ref[...] + acc
    o_ref[...] = x * _rms_scale(x) * g_ref[...]


def _combine(x2, rank, start, ys, ws, nblk, nround, g, tt):
    T, D = x2.shape
    grid_spec = pltpu.PrefetchScalarGridSpec(
        num_scalar_prefetch=3,
        grid=(T // tt,),
        in_specs=[pl.BlockSpec((tt, D), lambda j, *_: (j, 0)),
                  pl.BlockSpec((tt, N_EXPERTS), lambda j, *_: (j, 0)),
                  pl.BlockSpec((1, N_EXPERTS), lambda j, *_: (0, 0)),
                  pl.BlockSpec((1, D), lambda j, *_: (0, 0)),
                  pl.BlockSpec(memory_space=pl.ANY)],
        out_specs=pl.BlockSpec((tt, D), lambda j, *_: (j, 0)),
        scratch_shapes=[pltpu.VMEM((3, N_EXPERTS * COMBINE_WIN, D), BF16),
                        pltpu.SemaphoreType.DMA((3,))],
    )
    return pl.pallas_call(
        _combine_kernel,
        grid_spec=grid_spec,
        out_shape=jax.ShapeDtypeStruct((T, D), F32),
        compiler_params=pltpu.CompilerParams(
            dimension_semantics=("arbitrary",), vmem_limit_bytes=VMEM_LIMIT),
        name="combine",
    )(ws, nblk, nround, x2, rank, start.reshape(1, N_EXPERTS).astype(jnp.int32), g.reshape(1, D).astype(F32), ys)


def _moe_layer(x2, g, w_router, w1, w3, w2, final_g):
    T, D = x2.shape
    tm = MOE_TILE
    hn, rank, rankt, gatet, cnt = _router(x2, g, w_router)
    nb, _, gw = rankt.shape
    cb = jnp.concatenate([jnp.zeros((1, N_EXPERTS), jnp.int32),
                          cnt[:, 0, :N_EXPERTS].astype(jnp.int32)], axis=0)
    counts = cb[-1]
    padded = (counts + tm - 1) // tm * tm
    start = jnp.cumsum(padded) - padded
    nt = (2 * T) // tm + N_EXPERTS + 1
    n_used = (jnp.sum(padded) // tm).astype(jnp.int32)
    tile_lo = jnp.arange(nt, dtype=jnp.int32) * tm
    te = jnp.sum((tile_lo[:, None] >= (start + padded)[None, :]).astype(jnp.int32), axis=1)
    te = jnp.minimum(te, N_EXPERTS - 1).astype(jnp.int32)
    rbase = tile_lo - start[te]
    nvalid = jnp.clip(counts[te] - rbase, 0, tm)
    cb_after = cb[1:, :][:, te]
    wlo = jnp.sum((cb_after <= rbase[None, :]).astype(jnp.int32), axis=0)
    whi = jnp.sum((cb_after <= (rbase + nvalid - 1)[None, :]).astype(jnp.int32), axis=0)
    nwin = jnp.where(nvalid > 0, whi - wlo + 1, 0).astype(jnp.int32)
    wlo = jnp.minimum(wlo, nb - 1).astype(jnp.int32)
    nsub = tm // MOE_ROW_BLOCK
    r0 = rbase[:, None] + jnp.arange(nsub, dtype=jnp.int32)[None, :] * MOE_ROW_BLOCK
    nv = jnp.clip(counts[te][:, None] - r0, 0, MOE_ROW_BLOCK)
    sub_a = jnp.sum((cb_after[:, :, None] <= r0[None]).astype(jnp.int32), axis=0)
    sub_b = jnp.sum((cb_after[:, :, None] <= (r0 + nv - 1)[None]).astype(jnp.int32), axis=0)
    sublo = jnp.where(nv > 0, sub_a - wlo[:, None], 0).astype(jnp.int32)
    subhi = jnp.where(nv > 0, sub_b - wlo[:, None], -1).astype(jnp.int32)
    ys = _moe_grouped(hn, rankt, gatet, te, n_used.reshape(1), wlo, nwin, rbase.astype(jnp.int32),
                      sublo.reshape(-1), subhi.reshape(-1), nt,
                      w1.astype(BF16), w3.astype(BF16), w2.astype(BF16))
    tt = min(COMBINE_TILE, T)
    assert T % tt == 0 and tt % gw == 0
    cbt = cb[::tt // gw]
    first = start[None, :] + cbt[:-1, :]
    need = cbt[1:, :] - cbt[:-1, :]
    ws = first // 16 * 16
    nblk = jnp.where(need > 0, (first - ws + need + COMBINE_WIN - 1) // COMBINE_WIN, 0)
    nround = jnp.maximum(jnp.max(nblk, axis=1), 1)
    return _combine(x2, rank, start, ys, ws.reshape(-1).astype(jnp.int32), nblk.reshape(-1).astype(jnp.int32),
                    nround.astype(jnp.int32), final_g, tt)


def kernel(x, mix_norm_g, w_in, conv_w, conv_b, w_q, w_k, w_v, w_if, b_if, skip_m, mh_norm_g, gm_v_g, w_sp, b_sp, gm_out_g, w_out, ffn_norm_g, dense_w1, dense_w3, dense_w2, moe_router, moe_w1, moe_w3, moe_w2, final_norm_g):
    B, S, D = x.shape
    depth = w_in.shape[0]
    assert depth == 2 and dense_w1.shape[0] == 1 and moe_w1.shape[0] == 1
    for l in range(depth):
        x = _mixer(x, mix_norm_g[l], w_in[l], conv_w[l], conv_b[l], w_q[l], w_k[l], w_v[l],
                   w_if[l], b_if[l], skip_m[l], mh_norm_g[l], gm_v_g[l], w_sp[l], b_sp[l],
                   gm_out_g[l], w_out[l])
        x2 = x.reshape(B * S, D)
        if l % 2 == 0:
            x = _ffn_dense(x2, ffn_norm_g[l], dense_w1[l // 2], dense_w3[l // 2],
                           dense_w2[l // 2]).reshape(B, S, D)
        else:
            x = _moe_layer(x2, ffn_norm_g[l], moe_router[l // 2], moe_w1[l // 2], moe_w3[l // 2],
                           moe_w2[l // 2], final_norm_g).reshape(B, S, D)
    return x
```

```python
import numpy as np
import jax
import jax.numpy as jnp
from jax import lax
from jax.experimental import pallas as pl
from jax.experimental.pallas import tpu as pltpu

F32 = jnp.float32
BF16 = jnp.bfloat16
EPS = 1e-6
NEG = -1e30

D_MODEL = 1024
M_HEADS = 4
HEAD_DIM = 128
M_WIDTH = M_HEADS * HEAD_DIM
G_GROUPS = 4
G_CH = 128
G_WIDTH = G_GROUPS * G_CH
G_BLOCK = 128
G_CHUNK = 64
CONV_W = 4
N_EXPERTS = 8
LANES = 128

MLSTM_CHUNK = 128
MIX_BLOCK = 256
MIX_BATCH = 4
MIX_STAGGER = 6
FFN_BLOCK = 512
FFN_CHUNK = 256
ROUTER_BLOCK = 512
TOKEN_WINDOW = 256
MOE_TILE = 512
MOE_FF_TILE = 1792
MOE_FF_CHUNK = 256
MOE_ROW_BLOCK = 128
MOE_GATHER_WINS = 4
MOE_WIN_BUFS = 12
COMBINE_TILE = 512
COMBINE_WIN = 256
VMEM_LIMIT = 56 * 1024 * 1024


def _dot(a, b):
    return jnp.dot(a, b, preferred_element_type=F32)


def _dot_nt(a, b):
    return lax.dot_general(a, b, (((1,), (1,)), ((), ())), preferred_element_type=F32)


def _dot_tn(a, b):
    return lax.dot_general(a, b, (((0,), (0,)), ((), ())), preferred_element_type=F32)


def _rms_scale(x):
    return lax.rsqrt(jnp.mean(x * x, axis=-1, keepdims=True) + EPS)


def _sigmoid(x):
    return 1.0 / (1.0 + jnp.exp(-x))


def _silu(x):
    return x * _sigmoid(x)


def _gelu_tanh(x):
    c = np.float32(np.sqrt(2.0 / np.pi))
    return 0.5 * x * (1.0 + jnp.tanh(c * (x + 0.044715 * (x * x * x))))


def _log_sigmoid(x):
    return jnp.minimum(x, 0.0) - jnp.log1p(jnp.exp(-jnp.abs(x)))


def _split3(x):
    hi = x.astype(BF16)
    r1 = x - hi.astype(F32)
    mid = r1.astype(BF16)
    lo = (r1 - mid.astype(F32)).astype(BF16)
    return hi, mid, lo


def _mixer_kernel(x_ref, *refs):
    weights = refs[:16]
    o_ref = refs[16]
    xm_ext, ct_ref, n_ref, m_ref = refs[17:21]
    scratch = refs[17:]

    @pl.when(pl.program_id(1) == 0)
    def _():
        xm_ext[:, 0:8, :] = jnp.zeros((xm_ext.shape[0], 8, M_WIDTH), F32)
        ct_ref[...] = jnp.zeros(ct_ref.shape, F32)
        n_ref[...] = jnp.zeros(n_ref.shape, F32)
        m_ref[...] = jnp.zeros(m_ref.shape, F32)

    rows = [_mixer_block(x_ref.at[bb], *weights, o_ref.at[bb], *[r.at[bb] for r in scratch])
            for bb in range(x_ref.shape[0])]
    live = list(range(len(rows)))
    tick = 0
    while live:
        for bb in list(live):
            if tick >= bb * MIX_STAGGER and next(rows[bb], "done") == "done":
                live.remove(bb)
        tick += 1


def _mixer_block(x_ref, g_ref, win_ref, convw_ref, convb_ref, wqk_ref, wv_ref,
                 wif_ref, bif_ref, tcol_ref,
                 skip_ref, mhg_ref, gvg_ref, wsp_ref, bsp_ref, gog_ref, wout_ref,
                 o_ref,
                 xm_ext, ct_ref, n_ref, m_ref, y_ref, yg_ref):
    sb = x_ref.shape[0]
    L = MLSTM_CHUNK

    x = x_ref[...]
    h = (x * _rms_scale(x) * g_ref[...]).astype(BF16)
    yield
    x_m = _dot(h, win_ref[:, 0:M_WIDTH])
    yield

    xm_ext[8:8 + sb, :] = x_m
    acc = jnp.zeros((sb, M_WIDTH), F32) + convb_ref[...]
    for j in range(CONV_W):
        off = 8 - (CONV_W - 1) + j
        acc = acc + convw_ref[j:j + 1, :] * xm_ext[off:off + sb, :]
    x_c = _silu(acc)
    xm_ext[0:8, :] = xm_ext[sb:sb + 8, :]
    xc_b = x_c.astype(BF16)
    xm_b = x_m.astype(BF16)
    yield
    gm_v = _dot(h, win_ref[:, 2 * M_WIDTH + G_WIDTH:])
    yield

    qs, ks, vs = [], [], []
    for hd in range(M_HEADS):
        sl = slice(hd * HEAD_DIM, (hd + 1) * HEAD_DIM)
        qk = _dot(xc_b[:, sl], wqk_ref[hd])
        qs.append(qk[:, 0:HEAD_DIM])
        ks.append(qk[:, HEAD_DIM:])
    for pr in range(M_HEADS // 2):
        sl = slice(2 * pr * HEAD_DIM, 2 * (pr + 1) * HEAD_DIM)
        vv = _dot(xm_b[:, sl], wv_ref[pr])
        vs.append(vv[:, 0:HEAD_DIM])
        vs.append(vv[:, HEAD_DIM:])
    yield
    qkv_b = jnp.concatenate([t.astype(BF16) for t in qs + ks + vs], axis=1)

    vg = _gelu_tanh(gm_v)
    mu = jnp.mean(vg, axis=1, keepdims=True)
    dev = vg - mu
    var = jnp.mean(dev * dev, axis=1, keepdims=True)
    vn = (dev * lax.rsqrt(var + EPS) * gvg_ref[...]).astype(BF16)
    yield

    gcol = _dot(qkv_b, wif_ref[...]) + bif_ref[...]
    gm_u = _dot(h, win_ref[:, 2 * M_WIDTH:2 * M_WIDTH + G_WIDTH])
    yield
    lf_col = _log_sigmoid(gcol)
    ch, cm, cl = _split3(lf_col)
    ug = _gelu_tanh(gm_u)
    yield
    bcol3 = _dot(tcol_ref[...], jnp.concatenate([ch, cm, cl], axis=1))
    bcol = bcol3[:, 0:LANES] + bcol3[:, LANES:2 * LANES] + bcol3[:, 2 * LANES:]
    grow = gcol.T
    brow = bcol.T
    yield

    for i in range(sb // G_BLOCK):
        rs = slice(i * G_BLOCK, (i + 1) * G_BLOCK)
        for gi in range(G_GROUPS):
            sl = slice(gi * G_CH, (gi + 1) * G_CH)
            sv = _dot(wsp_ref[gi], vn[rs, sl]) + bsp_ref[:, sl]
            yg_ref[rs, sl] = ug[rs, sl] * sv
    yield
    yg = yg_ref[...]
    y_ref[:, M_WIDTH:] = (yg * _rms_scale(yg) * gog_ref[...]).astype(BF16)
    o_pre = _dot(h, win_ref[:, M_WIDTH:2 * M_WIDTH])
    yield

    scale = np.float32(HEAD_DIM ** -0.5)
    o_gate = _sigmoid(o_pre)
    units = [(j, hd) for j in range(sb // L) for hd in range(M_HEADS)]
    rows = {u: slice(u[0] * L, (u[0] + 1) * L) for u in units}
    lanes = {u: slice(u[1] * HEAD_DIM, (u[1] + 1) * HEAD_DIM) for u in units}
    tril = (lax.broadcasted_iota(jnp.int32, (L, L), 1) <= lax.broadcasted_iota(jnp.int32, (L, L), 0))
    b_col = {u: bcol[rows[u], M_HEADS + u[1]:M_HEADS + u[1] + 1] for u in units}
    ig_col = {u: gcol[rows[u], u[1]:u[1] + 1] for u in units}
    c_row = {u: brow[M_HEADS + u[1]:M_HEADS + u[1] + 1, rows[u]] - grow[u[1]:u[1] + 1, rows[u]] for u in units}
    b_end = {u: b_col[u][L - 1:L, :] for u in units}
    c_max = {u: jnp.max(-c_row[u], axis=1, keepdims=True) for u in units}
    m_in, m_out = {}, {}
    for hd in range(M_HEADS):
        m = m_ref[hd][:, 0:1]
        for j in range(sb // L):
            m_in[(j, hd)] = m
            m = b_end[(j, hd)] + jnp.maximum(m, c_max[(j, hd)])
            m_out[(j, hd)] = m
        m_ref[hd] = jnp.broadcast_to(m, (1, LANES))
    yield

    d = {u: jnp.where(tril, b_col[u] - c_row[u], NEG) for u in units}
    d_max = {u: jnp.max(d[u], axis=1, keepdims=True) for u in units}
    yield
    w, a, m_t, a_prev, w_s = {}, {}, {}, {}, {}
    for u in units:
        inter = b_col[u] + m_in[u]
        m_t[u] = jnp.maximum(inter, d_max[u])
        w[u] = jnp.exp(d[u] - m_t[u])
        a[u] = jnp.exp(inter - m_t[u])
        a_prev[u] = jnp.exp(b_end[u] + m_in[u] - m_out[u])
        w_s[u] = jnp.exp(b_end[u] - m_out[u] - (b_col[u] - ig_col[u]))
    yield

    qf = {u: qs[u[1]][rows[u]] * scale for u in units}
    qb = {u: qf[u].astype(BF16) for u in units}
    kb = {u: ks[u[1]][rows[u]].astype(BF16) for u in units}
    vb = {u: vs[u[1]][rows[u]].astype(BF16) for u in units}
    qk = {u: _dot_nt(qb[u], kb[u]) for u in units}
    s = {u: qk[u] * w[u] for u in units}
    yield
    sv = {u: _dot(s[u].astype(BF16), vb[u]) for u in units}
    s_sum = {u: jnp.sum(s[u], axis=1, keepdims=True) for u in units}
    yield
    kw = {u: ks[u[1]][rows[u]] * w_s[u] for u in units}
    upd = {u: _dot_tn(kw[u].astype(BF16), vb[u]) for u in units}
    k_sum = {u: jnp.sum(kw[u], axis=0, keepdims=True) for u in units}
    yield
    ct_in, n_in = {}, {}
    for hd in range(M_HEADS):
        ct = ct_ref[hd]
        n = n_ref[hd]
        for j in range(sb // L):
            u = (j, hd)
            ct_in[u] = ct
            n_in[u] = n
            ct = a_prev[u] * ct + upd[u]
            n = a_prev[u] * n + k_sum[u]
        ct_ref[hd] = ct
        n_ref[hd] = n
    qc = {u: _dot(qb[u], ct_in[u].astype(BF16)) for u in units}
    qn = {u: jnp.sum(qf[u] * n_in[u], axis=1, keepdims=True) for u in units}
    yield
    hh = {}
    for u in units:
        num = a[u] * qc[u] + sv[u]
        den = jnp.maximum(jnp.abs(a[u] * qn[u] + s_sum[u]), jnp.exp(-m_t[u]))
        hh[u] = o_gate[rows[u], lanes[u]] * (num / den)
    mu = {u: jnp.mean(hh[u], axis=1, keepdims=True) for u in units}
    yield
    dev = {u: hh[u] - mu[u] for u in units}
    var = {u: jnp.mean(dev[u] * dev[u], axis=1, keepdims=True) for u in units}
    yield
    for u in units:
        y_m = (dev[u] * lax.rsqrt(var[u] + EPS) * mhg_ref[:, lanes[u]]
               + skip_ref[:, lanes[u]] * x_c[rows[u], lanes[u]])
        y_ref[rows[u], lanes[u]] = y_m.astype(BF16)
    yield

    o_ref[...] = x + _dot(y_ref[...], wout_ref[...])


def _const_spec(shape):
    nd = len(shape)
    return pl.BlockSpec(shape, lambda *_: (0,) * nd, pipeline_mode=pl.Buffered(1))


def _chunk_cumsum_matrix(n, chunk):
    i = np.arange(n)[:, None]
    j = np.arange(n)[None, :]
    return ((j <= i) & (i // chunk == j // chunk)).astype(np.float32)


def _mixer(x, g, w_in, conv_w, conv_b, w_q, w_k, w_v, w_if, b_if, skip_m, mh_norm_g,
           gm_v_g, w_sp, b_sp, gm_out_g, w_out):
    B, S, D = x.shape
    sb = min(MIX_BLOCK, S)
    assert S % sb == 0 and sb % MLSTM_CHUNK == 0 and sb % G_BLOCK == 0
    tcol = _chunk_cumsum_matrix(sb, MLSTM_CHUNK)
    wif = jnp.pad(w_if, ((0, 0), (0, LANES - 2 * M_HEADS)))
    bif = jnp.pad(b_if, (0, LANES - 2 * M_HEADS)).reshape(1, LANES)
    wqk = jnp.concatenate([w_q, w_k], axis=2)
    zero = jnp.zeros((HEAD_DIM, HEAD_DIM), w_v.dtype)
    wv2 = jnp.stack([jnp.block([[w_v[2 * p], zero], [zero, w_v[2 * p + 1]]]) for p in range(M_HEADS // 2)])
    pos = np.arange(G_BLOCK)
    chunk_mask = (pos[:, None] // G_CHUNK) >= (pos[None, :] // G_CHUNK)
    wsp = jnp.where(chunk_mask[None], w_sp, 0.0).astype(BF16)
    bsp = jnp.repeat(b_sp.T, G_CH, axis=1)
    row = lambda a: a.reshape(1, -1).astype(F32)
    args = (x, row(g), w_in.astype(BF16), conv_w.astype(F32), row(conv_b),
            wqk.astype(BF16), wv2.astype(BF16),
            wif.astype(BF16), bif.astype(F32),
            jnp.asarray(tcol, BF16),
            row(skip_m), row(mh_norm_g), row(gm_v_g), wsp, bsp.astype(F32), row(gm_out_g),
            w_out.astype(BF16))
    nbm = MIX_BATCH if B % MIX_BATCH == 0 else 1
    in_specs = [pl.BlockSpec((nbm, sb, D), lambda b, s: (b, s, 0))]
    in_specs += [_const_spec(a.shape) for a in args[1:]]
    return pl.pallas_call(
        _mixer_kernel,
        grid=(B // nbm, S // sb),
        in_specs=in_specs,
        out_specs=pl.BlockSpec((nbm, sb, D), lambda b, s: (b, s, 0)),
        out_shape=jax.ShapeDtypeStruct((B, S, D), F32),
        scratch_shapes=[
            pltpu.VMEM((nbm, sb + 8, M_WIDTH), F32),
            pltpu.VMEM((nbm, M_HEADS, HEAD_DIM, HEAD_DIM), F32),
            pltpu.VMEM((nbm, M_HEADS, 1, HEAD_DIM), F32),
            pltpu.VMEM((nbm, M_HEADS, 1, LANES), F32),
            pltpu.VMEM((nbm, sb, M_WIDTH + G_WIDTH), BF16),
            pltpu.VMEM((nbm, sb, G_WIDTH), F32),
        ],
        compiler_params=pltpu.CompilerParams(
            dimension_semantics=("arbitrary", "arbitrary"), vmem_limit_bytes=VMEM_LIMIT),
        name="mixer",
    )(*args)


def _ffn_kernel(x_ref, g_ref, w1_ref, w3_ref, w2_ref, o_ref):
    x = x_ref[...]
    hn = (x * _rms_scale(x) * g_ref[...]).astype(BF16)
    acc = x
    for c in range(w1_ref.shape[1] // FFN_CHUNK):
        sl = slice(c * FFN_CHUNK, (c + 1) * FFN_CHUNK)
        h1 = _dot(hn, w1_ref[:, sl])
        h3 = _dot(hn, w3_ref[:, sl])
        acc = acc + _dot((_silu(h1) * h3).astype(BF16), w2_ref[sl, :])
    o_ref[...] = acc


def _ffn_dense(x2, g, w1, w3, w2):
    T, D = x2.shape
    ff = w1.shape[1]
    ffp = -(-ff // FFN_CHUNK) * FFN_CHUNK
    w1p = jnp.pad(w1, ((0, 0), (0, ffp - ff))).astype(BF16)
    w3p = jnp.pad(w3, ((0, 0), (0, ffp - ff))).astype(BF16)
    w2p = jnp.pad(w2, ((0, ffp - ff), (0, 0))).astype(BF16)
    tm = min(FFN_BLOCK, T)
    assert T % tm == 0
    return pl.pallas_call(
        _ffn_kernel,
        grid=(T // tm,),
        in_specs=[pl.BlockSpec((tm, D), lambda i: (i, 0)),
                  _const_spec((1, D)), _const_spec((D, ffp)), _const_spec((D, ffp)), _const_spec((ffp, D))],
        out_specs=pl.BlockSpec((tm, D), lambda i: (i, 0)),
        out_shape=jax.ShapeDtypeStruct((T, D), F32),
        compiler_params=pltpu.CompilerParams(
            dimension_semantics=("arbitrary",), vmem_limit_bytes=VMEM_LIMIT),
        name="ffn_dense",
    )(x2, g.reshape(1, D).astype(F32), w1p, w3p, w2p)


def _router_kernel(x_ref, g_ref, wr_ref, tri_ref, hn_ref, rank_ref, rankt_ref, gatet_ref, cnt_ref,
                   carry_ref):
    @pl.when(pl.program_id(0) == 0)
    def _():
        carry_ref[...] = jnp.zeros(carry_ref.shape, F32)

    x = x_ref[...]
    hn = x * _rms_scale(x) * g_ref[...]
    hi = hn.astype(BF16)
    hn_ref[...] = hi
    tb = x.shape[0]
    lane = lax.broadcasted_iota(jnp.int32, (tb, LANES), 1).astype(F32)
    lo = (hn - hi.astype(F32)).astype(BF16)
    hw = _dot(hi, wr_ref[...])
    logits = hw[:, 0:LANES] + hw[:, LANES:] + _dot(lo, wr_ref[:, 0:LANES])
    logits = jnp.where(lane < float(N_EXPERTS), logits, NEG)
    m1 = jnp.max(logits, axis=1, keepdims=True)
    i1 = jnp.min(jnp.where(logits == m1, lane, float(LANES)), axis=1, keepdims=True)
    rest = jnp.where(lane == i1, NEG, logits)
    m2 = jnp.max(rest, axis=1, keepdims=True)
    i2 = jnp.min(jnp.where(rest == m2, lane, float(LANES)), axis=1, keepdims=True)
    r = jnp.exp(m2 - m1)
    g1 = 1.0 / (1.0 + r)
    g2 = r * g1
    sel1 = lane == i1
    sel2 = lane == i2
    gates = jnp.where(sel1, g1, jnp.where(sel2, g2, 0.0))
    sel = jnp.where(sel1, 1.0, jnp.where(sel2, 1.0, 0.0))
    before = _dot(tri_ref[...], sel.astype(BF16))
    rank = jnp.where(sel > 0.0, carry_ref[...] + before, -1.0)
    rank_ref[...] = rank[:, 0:N_EXPERTS].astype(jnp.int32)
    rank_t = rank.T
    gates_t = gates.T
    gw = rankt_ref.shape[2]
    carry = carry_ref[...]
    for wi in range(tb // gw):
        ws = slice(wi * gw, (wi + 1) * gw)
        rankt_ref[wi] = rank_t[0:N_EXPERTS, ws]
        gatet_ref[wi] = gates_t[0:N_EXPERTS, ws]
        carry = carry + jnp.sum(sel[ws, :], axis=0, keepdims=True)
        cnt_ref[wi] = carry
    carry_ref[...] = carry


def _router(x2, g, w_router):
    T, D = x2.shape
    tb = min(ROUTER_BLOCK, T)
    gw = min(TOKEN_WINDOW, tb)
    assert T % tb == 0 and tb % gw == 0
    nb = T // tb
    nwb = tb // gw
    tri = np.tril(np.ones((tb, tb), np.float32), -1)
    wr = jnp.pad(w_router.astype(F32), ((0, 0), (0, LANES - N_EXPERTS)))
    wh = wr.astype(BF16)
    wl = (wr - wh.astype(F32)).astype(BF16)
    return pl.pallas_call(
        _router_kernel,
        grid=(nb,),
        in_specs=[pl.BlockSpec((tb, D), lambda i: (i, 0)),
                  _const_spec((1, D)), _const_spec((D, 2 * LANES)), _const_spec((tb, tb))],
        out_specs=[pl.BlockSpec((tb, D), lambda i: (i, 0)),
                   pl.BlockSpec((tb, N_EXPERTS), lambda i: (i, 0)),
                   pl.BlockSpec((nwb, N_EXPERTS, gw), lambda i: (i, 0, 0)),
                   pl.BlockSpec((nwb, N_EXPERTS, gw), lambda i: (i, 0, 0)),
                   pl.BlockSpec((nwb, 1, LANES), lambda i: (i, 0, 0))],
        out_shape=[jax.ShapeDtypeStruct((T, D), BF16),
                   jax.ShapeDtypeStruct((T, N_EXPERTS), jnp.int32),
                   jax.ShapeDtypeStruct((T // gw, N_EXPERTS, gw), F32),
                   jax.ShapeDtypeStruct((T // gw, N_EXPERTS, gw), F32),
                   jax.ShapeDtypeStruct((T // gw, 1, LANES), F32)],
        scratch_shapes=[pltpu.VMEM((1, LANES), F32)],
        compiler_params=pltpu.CompilerParams(
            dimension_semantics=("arbitrary",), vmem_limit_bytes=VMEM_LIMIT),
        name="router",
    )(x2, g.reshape(1, D).astype(F32), jnp.concatenate([wh, wl], axis=1), jnp.asarray(tri, BF16))


def _moe_kernel(te_ref, nused_ref, wlo_ref, nwin_ref, rbase_ref, sublo_ref, subhi_ref,
                hn_hbm, rankt_ref, gatet_ref, w1_ref, w3_ref, w2_ref,
                o_ref,
                hbuf, sem, xacc, gacc, xs_ref, acc_ref):
    i = pl.program_id(0)
    f = pl.program_id(1)
    used = i < nused_ref[0]
    tm = o_ref.shape[0]
    gw = rankt_ref.shape[2]
    nbuf = MOE_WIN_BUFS
    kwin = MOE_GATHER_WINS
    rblk = MOE_ROW_BLOCK
    nsub = tm // rblk

    def window_copy(tile, k, slot):
        w = wlo_ref[tile] + k
        return pltpu.make_async_copy(hn_hbm.at[pl.ds(pl.multiple_of(w * gw, gw), gw), :],
                                     hbuf.at[pl.ds(slot * gw, gw), :], sem.at[slot])

    def start_windows(tile, k_lo):
        for k in range(nbuf):
            @pl.when(k_lo + k < nwin_ref[tile])
            def _():
                window_copy(tile, k_lo + k, k).start()

    @pl.when(jnp.logical_and(used, f == 0))
    def _dispatch():
        @pl.when(i == 0)
        def _():
            hbuf[...] = jnp.zeros(hbuf.shape, BF16)
            start_windows(0, 0)

        e = te_ref[i]
        rb = rbase_ref[i].astype(F32)
        nwin = nwin_ref[i]
        w0 = wlo_ref[i]
        last_w = rankt_ref.shape[0] - 1
        def run_phase(p, first):
            k_lo = p * nbuf
            if not first:
                start_windows(i, k_lo)
            for k in range(nbuf):
                @pl.when(k_lo + k < nwin)
                def _():
                    window_copy(i, k_lo + k, k).wait()

            def select(s, c):
                k0 = jnp.maximum(sublo_ref[i * nsub + s], k_lo)
                k1 = jnp.minimum(subhi_ref[i * nsub + s], k_lo + nbuf - 1)
                kk = jnp.minimum(k0 + c * kwin, jnp.minimum(k_lo + nbuf, nwin) - 1)
                rel, gate = [], []
                for j in range(kwin):
                    live = jnp.logical_and(kk + j >= k0, kk + j <= k1)
                    w = jnp.minimum(w0 + kk + j, last_w)
                    rel.append(jnp.where(live, rankt_ref[w, pl.ds(e, 1), :] - rb, -1.0))
                    gate.append(gatet_ref[w, pl.ds(e, 1), :])
                rel = jnp.concatenate(rel, axis=1)
                gate = jnp.concatenate(gate, axis=1)
                rows = lax.broadcasted_iota(jnp.int32, (rblk, kwin * gw), 0).astype(F32) + float(s * rblk)
                hit = rel == rows
                gsum = jnp.sum(jnp.where(hit, gate, 0.0), axis=1, keepdims=True)
                src = hbuf[pl.ds(pl.multiple_of((kk - k_lo) * gw, gw), kwin * gw), :]
                return jnp.where(hit, 1.0, 0.0).astype(BF16), src, jnp.broadcast_to(gsum, (rblk, LANES))

            picks = [select(s, 0) for s in range(nsub)]
            prods = [_dot(onehot, src) for onehot, src, _ in picks]
            for s in range(nsub):
                rs = slice(s * rblk, (s + 1) * rblk)
                if first:
                    xacc[rs, :] = prods[s]
                    gacc[rs, :] = picks[s][2]
                else:
                    xacc[rs, :] += prods[s]
                    gacc[rs, :] += picks[s][2]

            for s in range(nsub):
                rs = slice(s * rblk, (s + 1) * rblk)
                k0 = jnp.maximum(sublo_ref[i * nsub + s], k_lo)
                k1 = jnp.minimum(subhi_ref[i * nsub + s], k_lo + nbuf - 1)

                def chunk_body(c, cc, s=s, rs=rs):
                    onehot, src, gsum = select(s, c)
                    xacc[rs, :] += _dot(onehot, src)
                    gacc[rs, :] += gsum
                    return cc

                lax.fori_loop(1, jnp.maximum((k1 - k0 + kwin) // kwin, 0), chunk_body, 0)

        run_phase(0, True)
        lax.fori_loop(1, (nwin + nbuf - 1) // nbuf, lambda p, c: (run_phase(p, False), c)[1], 0)
        xs_ref[...] = xacc[...].astype(BF16)

        @pl.when(i + 1 < nused_ref[0])
        def _():
            start_windows(i + 1, 0)

    @pl.when(used)
    def _ffn():
        x = xs_ref[...]
        part = None
        for c in range(w1_ref.shape[2] // MOE_FF_CHUNK):
            sl = slice(c * MOE_FF_CHUNK, (c + 1) * MOE_FF_CHUNK)
            h1 = _dot(x, w1_ref[0, :, sl])
            h3 = _dot(x, w3_ref[0, :, sl])
            p = _dot((_silu(h1) * h3).astype(BF16), w2_ref[0, sl, :].astype(BF16))
            part = p if part is None else part + p

        @pl.when(f == 0)
        def _():
            acc_ref[...] = part

        @pl.when(f > 0)
        def _():
            acc_ref[...] += part

    @pl.when(f == pl.num_programs(1) - 1)
    def _():
        @pl.when(used)
        def _():
            o_ref[...] = (acc_ref[...] * gacc[:, 0:1]).astype(BF16)

        @pl.when(jnp.logical_not(used))
        def _():
            o_ref[...] = jnp.zeros(o_ref.shape, BF16)


def _moe_grouped(hn, rankt, gatet, tile_expert, n_used, wlo, nwin, rbase, sublo, subhi, nt, w1, w3, w2):
    T, D = hn.shape
    nbp, _, gw = rankt.shape
    E, _, ff = w1.shape
    tm, tf = MOE_TILE, MOE_FF_TILE
    assert ff % tf == 0 and tf % MOE_FF_CHUNK == 0 and tm % MOE_ROW_BLOCK == 0
    nf = ff // tf

    def fsel(i, f, nu):
        return jnp.where(i < nu[0], f, nf - 1)

    grid_spec = pltpu.PrefetchScalarGridSpec(
        num_scalar_prefetch=7,
        grid=(nt, nf),
        in_specs=[pl.BlockSpec(memory_space=pl.ANY),
                  pl.BlockSpec((nbp, N_EXPERTS, gw), lambda i, f, *_: (0, 0, 0), pipeline_mode=pl.Buffered(1)),
                  pl.BlockSpec((nbp, N_EXPERTS, gw), lambda i, f, *_: (0, 0, 0), pipeline_mode=pl.Buffered(1)),
                  pl.BlockSpec((1, D, tf), lambda i, f, te, nu, *_: (te[i], 0, fsel(i, f, nu))),
                  pl.BlockSpec((1, D, tf), lambda i, f, te, nu, *_: (te[i], 0, fsel(i, f, nu))),
                  pl.BlockSpec((1, tf, D), lambda i, f, te, nu, *_: (te[i], fsel(i, f, nu), 0))],
        out_specs=pl.BlockSpec((tm, D), lambda i, f, *_: (i, 0)),
        scratch_shapes=[pltpu.VMEM(((MOE_WIN_BUFS + MOE_GATHER_WINS - 1) * gw, D), BF16),
                        pltpu.SemaphoreType.DMA((MOE_WIN_BUFS,)),
                        pltpu.VMEM((tm, D), F32),
                        pltpu.VMEM((tm, LANES), F32),
                        pltpu.VMEM((tm, D), BF16),
                        pltpu.VMEM((tm, D), F32)],
    )
    return pl.pallas_call(
        _moe_kernel,
        grid_spec=grid_spec,
        out_shape=jax.ShapeDtypeStruct((nt * tm, D), BF16),
        compiler_params=pltpu.CompilerParams(
            dimension_semantics=("arbitrary", "arbitrary"), vmem_limit_bytes=VMEM_LIMIT),
        name="moe_ffn",
    )(tile_expert, n_used, wlo, nwin, rbase, sublo, subhi, hn, rankt, gatet, w1, w3, w2)


def _combine_kernel(ws_ref, nblk_ref, nround_ref,
                    x_ref, rank_ref, start_ref, g_ref, ys_hbm, o_ref, ybuf, sem):
    j = pl.program_id(0)
    nj = pl.num_programs(0)
    tt = x_ref.shape[0]
    wc = COMBINE_WIN
    extra_slot = 2

    def window_copies(tile, rnd, slot):
        cps = []
        for e in range(N_EXPERTS):
            t = tile * N_EXPERTS + e
            s = ws_ref[t] + jnp.where(rnd < nblk_ref[t], rnd, 0) * wc
            cps.append(pltpu.make_async_copy(ys_hbm.at[pl.ds(pl.multiple_of(s, 16), wc), :],
                                             ybuf.at[slot, pl.ds(e * wc, wc), :], sem.at[slot]))
        return cps

    def select(rnd, slot):
        rank = rank_ref[...]
        pos = jnp.where(rank >= 0, rank + start_ref[...], -1)
        lane = lax.broadcasted_iota(jnp.int32, (tt, wc), 1)
        parts = []
        for e in range(N_EXPERTS):
            t = j * N_EXPERTS + e
            base = jnp.where(rnd < nblk_ref[t], ws_ref[t] + rnd * wc, -2 * wc)
            parts.append(jnp.where(pos[:, e:e + 1] == base + lane, 1.0, 0.0).astype(BF16))
        return _dot(jnp.concatenate(parts, axis=1), ybuf[slot])

    @pl.when(j == 0)
    def _():
        for cp in window_copies(0, 0, 0):
            cp.start()

    @pl.when(j + 1 < nj)
    def _():
        for cp in window_copies(j + 1, 0, (j + 1) % 2):
            cp.start()

    for cp in window_copies(j, 0, j % 2):
        cp.wait()
    acc = select(0, j % 2)

    def round_body(rnd, acc):
        for cp in window_copies(j, rnd, extra_slot):
            cp.start()
        for cp in window_copies(j, rnd, extra_slot):
            cp.wait()
        return acc + select(rnd, extra_slot)

    acc = lax.fori_loop(1, nround_ref[j], round_body, acc)
    x = x_ref[...] + acc
    o_ref[...] = x * _rms_scale(x) * g_ref[...]


def _combine(x2, rank, start, ys, ws, nblk, nround, g, tt):
    T, D = x2.shape
    grid_spec = pltpu.PrefetchScalarGridSpec(
        num_scalar_prefetch=3,
        grid=(T // tt,),
        in_specs=[pl.BlockSpec((tt, D), lambda j, *_: (j, 0)),
                  pl.BlockSpec((tt, N_EXPERTS), lambda j, *_: (j, 0)),
                  pl.BlockSpec((1, N_EXPERTS), lambda j, *_: (0, 0)),
                  pl.BlockSpec((1, D), lambda j, *_: (0, 0)),
                  pl.BlockSpec(memory_space=pl.ANY)],
        out_specs=pl.BlockSpec((tt, D), lambda j, *_: (j, 0)),
        scratch_shapes=[pltpu.VMEM((3, N_EXPERTS * COMBINE_WIN, D), BF16),
                        pltpu.SemaphoreType.DMA((3,))],
    )
    return pl.pallas_call(
        _combine_kernel,
        grid_spec=grid_spec,
        out_shape=jax.ShapeDtypeStruct((T, D), F32),
        compiler_params=pltpu.CompilerParams(
            dimension_semantics=("arbitrary",), vmem_limit_bytes=VMEM_LIMIT),
        name="combine",
    )(ws, nblk, nround, x2, rank, start.reshape(1, N_EXPERTS).astype(jnp.int32), g.reshape(1, D).astype(F32), ys)


def _moe_layer(x2, g, w_router, w1, w3, w2, final_g):
    T, D = x2.shape
    tm = MOE_TILE
    hn, rank, rankt, gatet, cnt = _router(x2, g, w_router)
    nb, _, gw = rankt.shape
    cb = jnp.concatenate([jnp.zeros((1, N_EXPERTS), jnp.int32),
                          cnt[:, 0, :N_EXPERTS].astype(jnp.int32)], axis=0)
    counts = cb[-1]
    padded = (counts + tm - 1) // tm * tm
    start = jnp.cumsum(padded) - padded
    nt = (2 * T) // tm + N_EXPERTS + 1
    n_used = (jnp.sum(padded) // tm).astype(jnp.int32)
    tile_lo = jnp.arange(nt, dtype=jnp.int32) * tm
    te = jnp.sum((tile_lo[:, None] >= (start + padded)[None, :]).astype(jnp.int32), axis=1)
    te = jnp.minimum(te, N_EXPERTS - 1).astype(jnp.int32)
    rbase = tile_lo - start[te]
    nvalid = jnp.clip(counts[te] - rbase, 0, tm)
    cb_after = cb[1:, :][:, te]
    wlo = jnp.sum((cb_after <= rbase[None, :]).astype(jnp.int32), axis=0)
    whi = jnp.sum((cb_after <= (rbase + nvalid - 1)[None, :]).astype(jnp.int32), axis=0)
    nwin = jnp.where(nvalid > 0, whi - wlo + 1, 0).astype(jnp.int32)
    wlo = jnp.minimum(wlo, nb - 1).astype(jnp.int32)
    nsub = tm // MOE_ROW_BLOCK
    r0 = rbase[:, None] + jnp.arange(nsub, dtype=jnp.int32)[None, :] * MOE_ROW_BLOCK
    nv = jnp.clip(counts[te][:, None] - r0, 0, MOE_ROW_BLOCK)
    sub_a = jnp.sum((cb_after[:, :, None] <= r0[None]).astype(jnp.int32), axis=0)
    sub_b = jnp.sum((cb_after[:, :, None] <= (r0 + nv - 1)[None]).astype(jnp.int32), axis=0)
    sublo = jnp.where(nv > 0, sub_a - wlo[:, None], 0).astype(jnp.int32)
    subhi = jnp.where(nv > 0, sub_b - wlo[:, None], -1).astype(jnp.int32)
    ys = _moe_grouped(hn, rankt, gatet, te, n_used.reshape(1), wlo, nwin, rbase.astype(jnp.int32),
                      sublo.reshape(-1), subhi.reshape(-1), nt,
                      w1.astype(BF16), w3.astype(BF16), w2)
    tt = min(COMBINE_TILE, T)
    assert T % tt == 0 and tt % gw == 0
    cbt = cb[::tt // gw]
    first = start[None, :] + cbt[:-1, :]
    need = cbt[1:, :] - cbt[:-1, :]
    ws = first // 16 * 16
    nblk = jnp.where(need > 0, (first - ws + need + COMBINE_WIN - 1) // COMBINE_WIN, 0)
    nround = jnp.maximum(jnp.max(nblk, axis=1), 1)
    return _combine(x2, rank, start, ys, ws.reshape(-1).astype(jnp.int32), nblk.reshape(-1).astype(jnp.int32),
                    nround.astype(jnp.int32), final_g, tt)


def kernel(x, mix_norm_g, w_in, conv_w, conv_b, w_q, w_k, w_v, w_if, b_if, skip_m, mh_norm_g, gm_v_g, w_sp, b_sp, gm_out_g, w_out, ffn_norm_g, dense_w1, dense_w3, dense_w2, moe_router, moe_w1, moe_w3, moe_w2, final_norm_g):
    B, S, D = x.shape
    depth = w_in.shape[0]
    assert depth == 2 and dense_w1.shape[0] == 1 and moe_w1.shape[0] == 1
    for l in range(depth):
        x = _mixer(x, mix_norm_g[l], w_in[l], conv_w[l], conv_b[l], w_q[l], w_k[l], w_v[l],
                   w_if[l], b_if[l], skip_m[l], mh_norm_g[l], gm_v_g[l], w_sp[l], b_sp[l],
                   gm_out_g[l], w_out[l])
        x2 = x.reshape(B * S, D)
        if l % 2 == 0:
            x = _ffn_dense(x2, ffn_norm_g[l], dense_w1[l // 2], dense_w3[l // 2],
                           dense_w2[l // 2]).reshape(B, S, D)
        else:
            x = _moe_layer(x2, ffn_norm_g[l], moe_router[l // 2], moe_w1[l // 2], moe_w3[l // 2],
                           moe_w2[l // 2], final_norm_g).reshape(B, S, D)
    return x
```

```python
import numpy as np
import jax
import jax.numpy as jnp
from jax import lax
from jax.experimental import pallas as pl
from jax.experimental.pallas import tpu as pltpu

F32 = jnp.float32
BF16 = jnp.bfloat16
EPS = 1e-6
NEG = -1e30

D_MODEL = 1024
M_HEADS = 4
HEAD_DIM = 128
M_WIDTH = M_HEADS * HEAD_DIM
G_GROUPS = 4
G_CH = 128
G_WIDTH = G_GROUPS * G_CH
G_BLOCK = 128
G_CHUNK = 64
CONV_W = 4
N_EXPERTS = 8
LANES = 128
SUBLANES = 8
BF16_ROWS = 16

MLSTM_CHUNK = 128
MIX_BLOCK = 256
MIX_BATCH = 4
MIX_STAGGER = 6
FFN_BLOCK = 512
FFN_CHUNK = 256
ROUTER_BLOCK = 512
TOKEN_WINDOW = 256
MOE_TILE = 512
MOE_FF_TILE = 1792
MOE_FF_CHUNK = 256
MOE_ROW_BLOCK = 128
MOE_GATHER_WINS = 4
MOE_WIN_BUFS = 12
COMBINE_TILE = 512
COMBINE_WIN = 256
VMEM_LIMIT = 56 * 1024 * 1024


def _dot(a, b):
    return jnp.dot(a, b, preferred_element_type=F32)


def _dot_nt(a, b):
    return lax.dot_general(a, b, (((1,), (1,)), ((), ())), preferred_element_type=F32)


def _dot_tn(a, b):
    return lax.dot_general(a, b, (((0,), (0,)), ((), ())), preferred_element_type=F32)


def _rms_scale(x):
    return lax.rsqrt(jnp.mean(x * x, axis=-1, keepdims=True) + EPS)


def _sigmoid(x):
    return 1.0 / (1.0 + jnp.exp(-x))


def _silu(x):
    return x * _sigmoid(x)


def _gelu_tanh(x):
    c = np.float32(np.sqrt(2.0 / np.pi))
    return 0.5 * x * (1.0 + jnp.tanh(c * (x + 0.044715 * (x * x * x))))


def _log_sigmoid(x):
    return jnp.minimum(x, 0.0) - jnp.log1p(jnp.exp(-jnp.abs(x)))


def _split3(x):
    hi = x.astype(BF16)
    r1 = x - hi.astype(F32)
    mid = r1.astype(BF16)
    lo = (r1 - mid.astype(F32)).astype(BF16)
    return hi, mid, lo


def _mixer_kernel(x_ref, *refs):
    weights = refs[:16]
    o_ref = refs[16]
    xm_ext, ct_ref, n_ref, m_ref = refs[17:21]
    scratch = refs[17:]

    @pl.when(pl.program_id(1) == 0)
    def _():
        xm_ext[:, 0:SUBLANES, :] = jnp.zeros((xm_ext.shape[0], SUBLANES, M_WIDTH), F32)
        ct_ref[...] = jnp.zeros(ct_ref.shape, F32)
        n_ref[...] = jnp.zeros(n_ref.shape, F32)
        m_ref[...] = jnp.zeros(m_ref.shape, F32)

    rows = [_mixer_block(x_ref.at[bb], *weights, o_ref.at[bb], *[r.at[bb] for r in scratch])
            for bb in range(x_ref.shape[0])]
    live = list(range(len(rows)))
    tick = 0
    while live:
        for bb in list(live):
            if tick >= bb * MIX_STAGGER and next(rows[bb], "done") == "done":
                live.remove(bb)
        tick += 1


def _mixer_block(x_ref, g_ref, win_ref, convw_ref, convb_ref, wqk_ref, wv_ref,
                 wif_ref, bif_ref, tcol_ref,
                 skip_ref, mhg_ref, gvg_ref, wsp_ref, bsp_ref, gog_ref, wout_ref,
                 o_ref,
                 xm_ext, ct_ref, n_ref, m_ref, y_ref, yg_ref):
    sb = x_ref.shape[0]
    L = MLSTM_CHUNK

    x = x_ref[...]
    h = (x * _rms_scale(x) * g_ref[...]).astype(BF16)
    yield
    x_m = _dot(h, win_ref[:, 0:M_WIDTH])
    yield

    xm_ext[SUBLANES:SUBLANES + sb, :] = x_m
    acc = jnp.zeros((sb, M_WIDTH), F32) + convb_ref[...]
    for j in range(CONV_W):
        off = SUBLANES - (CONV_W - 1) + j
        acc = acc + convw_ref[j:j + 1, :] * xm_ext[off:off + sb, :]
    x_c = _silu(acc)
    xm_ext[0:SUBLANES, :] = xm_ext[sb:sb + SUBLANES, :]
    xc_b = x_c.astype(BF16)
    xm_b = x_m.astype(BF16)
    yield
    gm_v = _dot(h, win_ref[:, 2 * M_WIDTH + G_WIDTH:])
    yield

    qs, ks, vs = [], [], []
    for hd in range(M_HEADS):
        sl = slice(hd * HEAD_DIM, (hd + 1) * HEAD_DIM)
        qk = _dot(xc_b[:, sl], wqk_ref[hd])
        qs.append(qk[:, 0:HEAD_DIM])
        ks.append(qk[:, HEAD_DIM:])
    for pr in range(M_HEADS // 2):
        sl = slice(2 * pr * HEAD_DIM, 2 * (pr + 1) * HEAD_DIM)
        vv = _dot(xm_b[:, sl], wv_ref[pr])
        vs.append(vv[:, 0:HEAD_DIM])
        vs.append(vv[:, HEAD_DIM:])
    yield
    qkv_b = jnp.concatenate([t.astype(BF16) for t in qs + ks + vs], axis=1)

    vg = _gelu_tanh(gm_v)
    mu = jnp.mean(vg, axis=1, keepdims=True)
    dev = vg - mu
    var = jnp.mean(dev * dev, axis=1, keepdims=True)
    vn = (dev * lax.rsqrt(var + EPS) * gvg_ref[...]).astype(BF16)
    yield

    gcol = _dot(qkv_b, wif_ref[...]) + bif_ref[...]
    gm_u = _dot(h, win_ref[:, 2 * M_WIDTH:2 * M_WIDTH + G_WIDTH])
    yield
    lf_col = _log_sigmoid(gcol)
    ch, cm, cl = _split3(lf_col)
    ug = _gelu_tanh(gm_u)
    yield
    bcol3 = _dot(tcol_ref[...], jnp.concatenate([ch, cm, cl], axis=1))
    bcol = bcol3[:, 0:LANES] + bcol3[:, LANES:2 * LANES] + bcol3[:, 2 * LANES:]
    grow = gcol.T
    brow = bcol.T
    yield

    for i in range(sb // G_BLOCK):
        rs = slice(i * G_BLOCK, (i + 1) * G_BLOCK)
        for gi in range(G_GROUPS):
            sl = slice(gi * G_CH, (gi + 1) * G_CH)
            sv = _dot(wsp_ref[gi], vn[rs, sl]) + bsp_ref[:, sl]
            yg_ref[rs, sl] = ug[rs, sl] * sv
    yield
    yg = yg_ref[...]
    y_ref[:, M_WIDTH:] = (yg * _rms_scale(yg) * gog_ref[...]).astype(BF16)
    o_pre = _dot(h, win_ref[:, M_WIDTH:2 * M_WIDTH])
    yield

    scale = np.float32(HEAD_DIM ** -0.5)
    o_gate = _sigmoid(o_pre)
    units = [(j, hd) for j in range(sb // L) for hd in range(M_HEADS)]
    rows = {u: slice(u[0] * L, (u[0] + 1) * L) for u in units}
    lanes = {u: slice(u[1] * HEAD_DIM, (u[1] + 1) * HEAD_DIM) for u in units}
    tril = (lax.broadcasted_iota(jnp.int32, (L, L), 1) <= lax.broadcasted_iota(jnp.int32, (L, L), 0))
    b_col = {u: bcol[rows[u], M_HEADS + u[1]:M_HEADS + u[1] + 1] for u in units}
    ig_col = {u: gcol[rows[u], u[1]:u[1] + 1] for u in units}
    c_row = {u: brow[M_HEADS + u[1]:M_HEADS + u[1] + 1, rows[u]] - grow[u[1]:u[1] + 1, rows[u]] for u in units}
    b_end = {u: b_col[u][L - 1:L, :] for u in units}
    c_max = {u: jnp.max(-c_row[u], axis=1, keepdims=True) for u in units}
    m_in, m_out = {}, {}
    for hd in range(M_HEADS):
        m = m_ref[hd][:, 0:1]
        for j in range(sb // L):
            m_in[(j, hd)] = m
            m = b_end[(j, hd)] + jnp.maximum(m, c_max[(j, hd)])
            m_out[(j, hd)] = m
        m_ref[hd] = jnp.broadcast_to(m, (1, LANES))
    yield

    d = {u: jnp.where(tril, b_col[u] - c_row[u], NEG) for u in units}
    d_max = {u: jnp.max(d[u], axis=1, keepdims=True) for u in units}
    yield
    w, a, m_t, a_prev, w_s = {}, {}, {}, {}, {}
    for u in units:
        inter = b_col[u] + m_in[u]
        m_t[u] = jnp.maximum(inter, d_max[u])
        w[u] = jnp.exp(d[u] - m_t[u])
        a[u] = jnp.exp(inter - m_t[u])
        a_prev[u] = jnp.exp(b_end[u] + m_in[u] - m_out[u])
        w_s[u] = jnp.exp(b_end[u] - m_out[u] - (b_col[u] - ig_col[u]))
    yield

    qf = {u: qs[u[1]][rows[u]] * scale for u in units}
    qb = {u: qf[u].astype(BF16) for u in units}
    kb = {u: ks[u[1]][rows[u]].astype(BF16) for u in units}
    vb = {u: vs[u[1]][rows[u]].astype(BF16) for u in units}
    qk = {u: _dot_nt(qb[u], kb[u]) for u in units}
    s = {u: qk[u] * w[u] for u in units}
    yield
    sv = {u: _dot(s[u].astype(BF16), vb[u]) for u in units}
    s_sum = {u: jnp.sum(s[u], axis=1, keepdims=True) for u in units}
    yield
    kw = {u: ks[u[1]][rows[u]] * w_s[u] for u in units}
    upd = {u: _dot_tn(kw[u].astype(BF16), vb[u]) for u in units}
    k_sum = {u: jnp.sum(kw[u], axis=0, keepdims=True) for u in units}
    yield
    ct_in, n_in = {}, {}
    for hd in range(M_HEADS):
        ct = ct_ref[hd]
        n = n_ref[hd]
        for j in range(sb // L):
            u = (j, hd)
            ct_in[u] = ct
            n_in[u] = n
            ct = a_prev[u] * ct + upd[u]
            n = a_prev[u] * n + k_sum[u]
        ct_ref[hd] = ct
        n_ref[hd] = n
    qc = {u: _dot(qb[u], ct_in[u].astype(BF16)) for u in units}
    qn = {u: jnp.sum(qf[u] * n_in[u], axis=1, keepdims=True) for u in units}
    yield
    hh = {}
    for u in units:
        num = a[u] * qc[u] + sv[u]
        den = jnp.maximum(jnp.abs(a[u] * qn[u] + s_sum[u]), jnp.exp(-m_t[u]))
        hh[u] = o_gate[rows[u], lanes[u]] * (num / den)
    mu = {u: jnp.mean(hh[u], axis=1, keepdims=True) for u in units}
    yield
    dev = {u: hh[u] - mu[u] for u in units}
    var = {u: jnp.mean(dev[u] * dev[u], axis=1, keepdims=True) for u in units}
    yield
    for u in units:
        y_m = (dev[u] * lax.rsqrt(var[u] + EPS) * mhg_ref[:, lanes[u]]
               + skip_ref[:, lanes[u]] * x_c[rows[u], lanes[u]])
        y_ref[rows[u], lanes[u]] = y_m.astype(BF16)
    yield

    o_ref[...] = x + _dot(y_ref[...], wout_ref[...])


def _const_spec(shape):
    nd = len(shape)
    return pl.BlockSpec(shape, lambda *_: (0,) * nd, pipeline_mode=pl.Buffered(1))


def _chunk_cumsum_matrix(n, chunk):
    i = np.arange(n)[:, None]
    j = np.arange(n)[None, :]
    return ((j <= i) & (i // chunk == j // chunk)).astype(np.float32)


def _mixer(x, g, w_in, conv_w, conv_b, w_q, w_k, w_v, w_if, b_if, skip_m, mh_norm_g,
           gm_v_g, w_sp, b_sp, gm_out_g, w_out):
    B, S, D = x.shape
    sb = min(MIX_BLOCK, S)
    assert S % sb == 0 and sb % MLSTM_CHUNK == 0 and sb % G_BLOCK == 0
    tcol = _chunk_cumsum_matrix(sb, MLSTM_CHUNK)
    wif = jnp.pad(w_if, ((0, 0), (0, LANES - 2 * M_HEADS)))
    bif = jnp.pad(b_if, (0, LANES - 2 * M_HEADS)).reshape(1, LANES)
    wqk = jnp.concatenate([w_q, w_k], axis=2)
    zero = jnp.zeros((HEAD_DIM, HEAD_DIM), w_v.dtype)
    wv2 = jnp.stack([jnp.block([[w_v[2 * p], zero], [zero, w_v[2 * p + 1]]]) for p in range(M_HEADS // 2)])
    pos = np.arange(G_BLOCK)
    chunk_mask = (pos[:, None] // G_CHUNK) >= (pos[None, :] // G_CHUNK)
    wsp = jnp.where(chunk_mask[None], w_sp, 0.0).astype(BF16)
    bsp = jnp.repeat(b_sp.T, G_CH, axis=1)
    row = lambda a: a.reshape(1, -1).astype(F32)
    args = (x, row(g), w_in.astype(BF16), conv_w.astype(F32), row(conv_b),
            wqk.astype(BF16), wv2.astype(BF16),
            wif.astype(BF16), bif.astype(F32),
            jnp.asarray(tcol, BF16),
            row(skip_m), row(mh_norm_g), row(gm_v_g), wsp, bsp.astype(F32), row(gm_out_g),
            w_out.astype(BF16))
    nbm = MIX_BATCH if B % MIX_BATCH == 0 else 1
    in_specs = [pl.BlockSpec((nbm, sb, D), lambda b, s: (b, s, 0))]
    in_specs += [_const_spec(a.shape) for a in args[1:]]
    return pl.pallas_call(
        _mixer_kernel,
        grid=(B // nbm, S // sb),
        in_specs=in_specs,
        out_specs=pl.BlockSpec((nbm, sb, D), lambda b, s: (b, s, 0)),
        out_shape=jax.ShapeDtypeStruct((B, S, D), F32),
        scratch_shapes=[
            pltpu.VMEM((nbm, sb + SUBLANES, M_WIDTH), F32),
            pltpu.VMEM((nbm, M_HEADS, HEAD_DIM, HEAD_DIM), F32),
            pltpu.VMEM((nbm, M_HEADS, 1, HEAD_DIM), F32),
            pltpu.VMEM((nbm, M_HEADS, 1, LANES), F32),
            pltpu.VMEM((nbm, sb, M_WIDTH + G_WIDTH), BF16),
            pltpu.VMEM((nbm, sb, G_WIDTH), F32),
        ],
        compiler_params=pltpu.CompilerParams(
            dimension_semantics=("arbitrary", "arbitrary"), vmem_limit_bytes=VMEM_LIMIT),
        name="mixer",
    )(*args)


def _ffn_kernel(x_ref, g_ref, w1_ref, w3_ref, w2_ref, o_ref):
    x = x_ref[...]
    hn = (x * _rms_scale(x) * g_ref[...]).astype(BF16)
    acc = x
    for c in range(w1_ref.shape[1] // FFN_CHUNK):
        sl = slice(c * FFN_CHUNK, (c + 1) * FFN_CHUNK)
        h1 = _dot(hn, w1_ref[:, sl])
        h3 = _dot(hn, w3_ref[:, sl])
        acc = acc + _dot((_silu(h1) * h3).astype(BF16), w2_ref[sl, :])
    o_ref[...] = acc


def _ffn_dense(x2, g, w1, w3, w2):
    T, D = x2.shape
    ff = w1.shape[1]
    ffp = -(-ff // FFN_CHUNK) * FFN_CHUNK
    w1p = jnp.pad(w1, ((0, 0), (0, ffp - ff))).astype(BF16)
    w3p = jnp.pad(w3, ((0, 0), (0, ffp - ff))).astype(BF16)
    w2p = jnp.pad(w2, ((0, ffp - ff), (0, 0))).astype(BF16)
    tm = min(FFN_BLOCK, T)
    assert T % tm == 0
    return pl.pallas_call(
        _ffn_kernel,
        grid=(T // tm,),
        in_specs=[pl.BlockSpec((tm, D), lambda i: (i, 0)),
                  _const_spec((1, D)), _const_spec((D, ffp)), _const_spec((D, ffp)), _const_spec((ffp, D))],
        out_specs=pl.BlockSpec((tm, D), lambda i: (i, 0)),
        out_shape=jax.ShapeDtypeStruct((T, D), F32),
        compiler_params=pltpu.CompilerParams(
            dimension_semantics=("arbitrary",), vmem_limit_bytes=VMEM_LIMIT),
        name="ffn_dense",
    )(x2, g.reshape(1, D).astype(F32), w1p, w3p, w2p)


def _router_kernel(x_ref, g_ref, wr_ref, tri_ref, hn_ref, rank_ref, rankt_ref, gatet_ref, cnt_ref,
                   carry_ref):
    @pl.when(pl.program_id(0) == 0)
    def _():
        carry_ref[...] = jnp.zeros(carry_ref.shape, F32)

    x = x_ref[...]
    hn = x * _rms_scale(x) * g_ref[...]
    hi = hn.astype(BF16)
    hn_ref[...] = hi
    tb = x.shape[0]
    lane = lax.broadcasted_iota(jnp.int32, (tb, LANES), 1).astype(F32)
    lo = (hn - hi.astype(F32)).astype(BF16)
    hw = _dot(hi, wr_ref[...])
    logits = hw[:, 0:LANES] + hw[:, LANES:] + _dot(lo, wr_ref[:, 0:LANES])
    logits = jnp.where(lane < float(N_EXPERTS), logits, NEG)
    m1 = jnp.max(logits, axis=1, keepdims=True)
    i1 = jnp.min(jnp.where(logits == m1, lane, float(LANES)), axis=1, keepdims=True)
    rest = jnp.where(lane == i1, NEG, logits)
    m2 = jnp.max(rest, axis=1, keepdims=True)
    i2 = jnp.min(jnp.where(rest == m2, lane, float(LANES)), axis=1, keepdims=True)
    r = jnp.exp(m2 - m1)
    g1 = 1.0 / (1.0 + r)
    g2 = r * g1
    sel1 = lane == i1
    sel2 = lane == i2
    gates = jnp.where(sel1, g1, jnp.where(sel2, g2, 0.0))
    sel = jnp.where(sel1, 1.0, jnp.where(sel2, 1.0, 0.0))
    before = _dot(tri_ref[...], sel.astype(BF16))
    rank = jnp.where(sel > 0.0, carry_ref[...] + before, -1.0)
    rank_ref[...] = rank[:, 0:N_EXPERTS].astype(jnp.int32)
    rank_t = rank.T
    gates_t = gates.T
    gw = rankt_ref.shape[2]
    carry = carry_ref[...]
    for wi in range(tb // gw):
        ws = slice(wi * gw, (wi + 1) * gw)
        rankt_ref[wi] = rank_t[0:N_EXPERTS, ws]
        gatet_ref[wi] = gates_t[0:N_EXPERTS, ws]
        carry = carry + jnp.sum(sel[ws, :], axis=0, keepdims=True)
        cnt_ref[wi] = carry
    carry_ref[...] = carry


def _router(x2, g, w_router):
    T, D = x2.shape
    tb = min(ROUTER_BLOCK, T)
    gw = min(TOKEN_WINDOW, tb)
    assert T % tb == 0 and tb % gw == 0
    nb = T // tb
    nwb = tb // gw
    tri = np.tril(np.ones((tb, tb), np.float32), -1)
    wr = jnp.pad(w_router.astype(F32), ((0, 0), (0, LANES - N_EXPERTS)))
    wh = wr.astype(BF16)
    wl = (wr - wh.astype(F32)).astype(BF16)
    return pl.pallas_call(
        _router_kernel,
        grid=(nb,),
        in_specs=[pl.BlockSpec((tb, D), lambda i: (i, 0)),
                  _const_spec((1, D)), _const_spec((D, 2 * LANES)), _const_spec((tb, tb))],
        out_specs=[pl.BlockSpec((tb, D), lambda i: (i, 0)),
                   pl.BlockSpec((tb, N_EXPERTS), lambda i: (i, 0)),
                   pl.BlockSpec((nwb, N_EXPERTS, gw), lambda i: (i, 0, 0)),
                   pl.BlockSpec((nwb, N_EXPERTS, gw), lambda i: (i, 0, 0)),
                   pl.BlockSpec((nwb, 1, LANES), lambda i: (i, 0, 0))],
        out_shape=[jax.ShapeDtypeStruct((T, D), BF16),
                   jax.ShapeDtypeStruct((T, N_EXPERTS), jnp.int32),
                   jax.ShapeDtypeStruct((T // gw, N_EXPERTS, gw), F32),
                   jax.ShapeDtypeStruct((T // gw, N_EXPERTS, gw), F32),
                   jax.ShapeDtypeStruct((T // gw, 1, LANES), F32)],
        scratch_shapes=[pltpu.VMEM((1, LANES), F32)],
        compiler_params=pltpu.CompilerParams(
            dimension_semantics=("arbitrary",), vmem_limit_bytes=VMEM_LIMIT),
        name="router",
    )(x2, g.reshape(1, D).astype(F32), jnp.concatenate([wh, wl], axis=1), jnp.asarray(tri, BF16))


def _moe_kernel(te_ref, nused_ref, wlo_ref, nwin_ref, rbase_ref, sublo_ref, subhi_ref,
                hn_hbm, rankt_ref, gatet_ref, w1_ref, w3_ref, w2_ref,
                o_ref,
                hbuf, sem, gacc, xs_ref, acc_ref):
    i = pl.program_id(0)
    f = pl.program_id(1)
    used = i < nused_ref[0]
    tm = o_ref.shape[0]
    gw = rankt_ref.shape[2]
    nbuf = MOE_WIN_BUFS
    kwin = MOE_GATHER_WINS
    rblk = MOE_ROW_BLOCK
    nsub = tm // rblk

    def window_copy(tile, k, slot):
        w = wlo_ref[tile] + k
        return pltpu.make_async_copy(hn_hbm.at[pl.ds(pl.multiple_of(w * gw, gw), gw), :],
                                     hbuf.at[pl.ds(slot * gw, gw), :], sem.at[slot])

    def start_windows(tile, k_lo):
        for k in range(nbuf):
            @pl.when(k_lo + k < nwin_ref[tile])
            def _():
                window_copy(tile, k_lo + k, k).start()

    @pl.when(jnp.logical_and(used, f == 0))
    def _dispatch():
        @pl.when(i == 0)
        def _():
            hbuf[...] = jnp.zeros(hbuf.shape, BF16)
            start_windows(0, 0)

        e = te_ref[i]
        rb = rbase_ref[i].astype(F32)
        nwin = nwin_ref[i]
        w0 = wlo_ref[i]
        last_w = rankt_ref.shape[0] - 1
        def run_phase(p, first):
            k_lo = p * nbuf
            if not first:
                start_windows(i, k_lo)
            for k in range(nbuf):
                @pl.when(k_lo + k < nwin)
                def _():
                    window_copy(i, k_lo + k, k).wait()

            def select(s, c):
                k0 = jnp.maximum(sublo_ref[i * nsub + s], k_lo)
                k1 = jnp.minimum(subhi_ref[i * nsub + s], k_lo + nbuf - 1)
                kk = jnp.minimum(k0 + c * kwin, jnp.minimum(k_lo + nbuf, nwin) - 1)
                rel, gate = [], []
                for j in range(kwin):
                    live = jnp.logical_and(kk + j >= k0, kk + j <= k1)
                    w = jnp.minimum(w0 + kk + j, last_w)
                    rel.append(jnp.where(live, rankt_ref[w, pl.ds(e, 1), :] - rb, -1.0))
                    gate.append(gatet_ref[w, pl.ds(e, 1), :])
                rel = jnp.concatenate(rel, axis=1)
                gate = jnp.concatenate(gate, axis=1)
                rows = lax.broadcasted_iota(jnp.int32, (rblk, kwin * gw), 0).astype(F32) + float(s * rblk)
                hit = rel == rows
                gsum = jnp.sum(jnp.where(hit, gate, 0.0), axis=1, keepdims=True)
                src = hbuf[pl.ds(pl.multiple_of((kk - k_lo) * gw, gw), kwin * gw), :]
                return jnp.where(hit, 1.0, 0.0).astype(BF16), src, jnp.broadcast_to(gsum, (rblk, LANES))

            picks = [select(s, 0) for s in range(nsub)]
            prods = [_dot(onehot, src) for onehot, src, _ in picks]
            for s in range(nsub):
                rs = slice(s * rblk, (s + 1) * rblk)
                if first:
                    xs_ref[rs, :] = prods[s].astype(BF16)
                    gacc[rs, :] = picks[s][2]
                else:
                    xs_ref[rs, :] += prods[s].astype(BF16)
                    gacc[rs, :] += picks[s][2]

            for s in range(nsub):
                rs = slice(s * rblk, (s + 1) * rblk)
                k0 = jnp.maximum(sublo_ref[i * nsub + s], k_lo)
                k1 = jnp.minimum(subhi_ref[i * nsub + s], k_lo + nbuf - 1)

                def chunk_body(c, cc, s=s, rs=rs):
                    onehot, src, gsum = select(s, c)
                    xs_ref[rs, :] += _dot(onehot, src).astype(BF16)
                    gacc[rs, :] += gsum
                    return cc

                lax.fori_loop(1, jnp.maximum((k1 - k0 + kwin) // kwin, 0), chunk_body, 0)

        run_phase(0, True)
        lax.fori_loop(1, (nwin + nbuf - 1) // nbuf, lambda p, c: (run_phase(p, False), c)[1], 0)

        @pl.when(i + 1 < nused_ref[0])
        def _():
            start_windows(i + 1, 0)

    @pl.when(used)
    def _ffn():
        x = xs_ref[...]
        part = None
        for c in range(w1_ref.shape[2] // MOE_FF_CHUNK):
            sl = slice(c * MOE_FF_CHUNK, (c + 1) * MOE_FF_CHUNK)
            h1 = _dot(x, w1_ref[0, :, sl])
            h3 = _dot(x, w3_ref[0, :, sl])
            p = _dot((_silu(h1) * h3).astype(BF16), w2_ref[0, sl, :].astype(BF16))
            part = p if part is None else part + p

        @pl.when(f == 0)
        def _():
            acc_ref[...] = part

        @pl.when(f > 0)
        def _():
            acc_ref[...] += part

    @pl.when(f == pl.num_programs(1) - 1)
    def _():
        @pl.when(used)
        def _():
            o_ref[...] = (acc_ref[...] * gacc[:, 0:1]).astype(BF16)

        @pl.when(jnp.logical_not(used))
        def _():
            o_ref[...] = jnp.zeros(o_ref.shape, BF16)


def _moe_grouped(hn, rankt, gatet, tile_expert, n_used, wlo, nwin, rbase, sublo, subhi, nt, w1, w3, w2):
    T, D = hn.shape
    nbp, _, gw = rankt.shape
    E, _, ff = w1.shape
    tm, tf = MOE_TILE, MOE_FF_TILE
    assert ff % tf == 0 and tf % MOE_FF_CHUNK == 0 and tm % MOE_ROW_BLOCK == 0
    nf = ff // tf

    def fsel(i, f, nu):
        return jnp.where(i < nu[0], f, nf - 1)

    grid_spec = pltpu.PrefetchScalarGridSpec(
        num_scalar_prefetch=7,
        grid=(nt, nf),
        in_specs=[pl.BlockSpec(memory_space=pl.ANY),
                  pl.BlockSpec((nbp, N_EXPERTS, gw), lambda i, f, *_: (0, 0, 0), pipeline_mode=pl.Buffered(1)),
                  pl.BlockSpec((nbp, N_EXPERTS, gw), lambda i, f, *_: (0, 0, 0), pipeline_mode=pl.Buffered(1)),
                  pl.BlockSpec((1, D, tf), lambda i, f, te, nu, *_: (te[i], 0, fsel(i, f, nu))),
                  pl.BlockSpec((1, D, tf), lambda i, f, te, nu, *_: (te[i], 0, fsel(i, f, nu))),
                  pl.BlockSpec((1, tf, D), lambda i, f, te, nu, *_: (te[i], fsel(i, f, nu), 0))],
        out_specs=pl.BlockSpec((tm, D), lambda i, f, *_: (i, 0)),
        scratch_shapes=[pltpu.VMEM(((MOE_WIN_BUFS + MOE_GATHER_WINS - 1) * gw, D), BF16),
                        pltpu.SemaphoreType.DMA((MOE_WIN_BUFS,)),
                        pltpu.VMEM((tm, LANES), F32),
                        pltpu.VMEM((tm, D), BF16),
                        pltpu.VMEM((tm, D), F32)],
    )
    return pl.pallas_call(
        _moe_kernel,
        grid_spec=grid_spec,
        out_shape=jax.ShapeDtypeStruct((nt * tm, D), BF16),
        compiler_params=pltpu.CompilerParams(
            dimension_semantics=("arbitrary", "arbitrary"), vmem_limit_bytes=VMEM_LIMIT),
        name="moe_ffn",
    )(tile_expert, n_used, wlo, nwin, rbase, sublo, subhi, hn, rankt, gatet, w1, w3, w2)


def _combine_kernel(ws_ref, nblk_ref, nround_ref,
                    x_ref, rank_ref, start_ref, g_ref, ys_hbm, o_ref, ybuf, sem):
    j = pl.program_id(0)
    nj = pl.num_programs(0)
    tt = x_ref.shape[0]
    wc = COMBINE_WIN
    extra_slot = 2

    def window_copies(tile, rnd, slot):
        cps = []
        for e in range(N_EXPERTS):
            t = tile * N_EXPERTS + e
            s = ws_ref[t] + jnp.where(rnd < nblk_ref[t], rnd, 0) * wc
            cps.append(pltpu.make_async_copy(ys_hbm.at[pl.ds(pl.multiple_of(s, BF16_ROWS), wc), :],
                                             ybuf.at[slot, pl.ds(e * wc, wc), :], sem.at[slot]))
        return cps

    def select(rnd, slot):
        rank = rank_ref[...]
        pos = jnp.where(rank >= 0, rank + start_ref[...], -1)
        lane = lax.broadcasted_iota(jnp.int32, (tt, wc), 1)
        parts = []
        for e in range(N_EXPERTS):
            t = j * N_EXPERTS + e
            base = jnp.where(rnd < nblk_ref[t], ws_ref[t] + rnd * wc, -2 * wc)
            parts.append(jnp.where(pos[:, e:e + 1] == base + lane, 1.0, 0.0).astype(BF16))
        return _dot(jnp.concatenate(parts, axis=1), ybuf[slot])

    @pl.when(j == 0)
    def _():
        for cp in window_copies(0, 0, 0):
            cp.start()

    @pl.when(j + 1 < nj)
    def _():
        for cp in window_copies(j + 1, 0, (j + 1) % 2):
            cp.start()

    for cp in window_copies(j, 0, j % 2):
        cp.wait()
    acc = select(0, j % 2)

    def round_body(rnd, acc):
        for cp in window_copies(j, rnd, extra_slot):
            cp.start()
        for cp in window_copies(j, rnd, extra_slot):
            cp.wait()
        return acc + select(rnd, extra_slot)

    acc = lax.fori_loop(1, nround_ref[j], round_body, acc)
    x = x_ref[...] + acc
    o_ref[...] = x * _rms_scale(x) * g_ref[...]


def _combine(x2, rank, start, ys, ws, nblk, nround, g, tt):
    T, D = x2.shape
    grid_spec = pltpu.PrefetchScalarGridSpec(
        num_scalar_prefetch=3,
        grid=(T // tt,),
        in_specs=[pl.BlockSpec((tt, D), lambda j, *_: (j, 0)),
                  pl.BlockSpec((tt, N_EXPERTS), lambda j, *_: (j, 0)),
                  pl.BlockSpec((1, N_EXPERTS), lambda j, *_: (0, 0)),
                  pl.BlockSpec((1, D), lambda j, *_: (0, 0)),
                  pl.BlockSpec(memory_space=pl.ANY)],
        out_specs=pl.BlockSpec((tt, D), lambda j, *_: (j, 0)),
        scratch_shapes=[pltpu.VMEM((3, N_EXPERTS * COMBINE_WIN, D), BF16),
                        pltpu.SemaphoreType.DMA((3,))],
    )
    return pl.pallas_call(
        _combine_kernel,
        grid_spec=grid_spec,
        out_shape=jax.ShapeDtypeStruct((T, D), F32),
        compiler_params=pltpu.CompilerParams(
            dimension_semantics=("arbitrary",), vmem_limit_bytes=VMEM_LIMIT),
        name="combine",
    )(ws, nblk, nround, x2, rank, start.reshape(1, N_EXPERTS).astype(jnp.int32), g.reshape(1, D).astype(F32), ys)


def _moe_layer(x2, g, w_router, w1, w3, w2, final_g):
    T, D = x2.shape
    tm = MOE_TILE
    hn, rank, rankt, gatet, cnt = _router(x2, g, w_router)
    nb, _, gw = rankt.shape
    cb = jnp.concatenate([jnp.zeros((1, N_EXPERTS), jnp.int32),
                          cnt[:, 0, :N_EXPERTS].astype(jnp.int32)], axis=0)
    counts = cb[-1]
    padded = (counts + tm - 1) // tm * tm
    start = jnp.cumsum(padded) - padded
    nt = (2 * T) // tm + N_EXPERTS + 1
    n_used = (jnp.sum(padded) // tm).astype(jnp.int32)
    tile_lo = jnp.arange(nt, dtype=jnp.int32) * tm
    te = jnp.sum((tile_lo[:, None] >= (start + padded)[None, :]).astype(jnp.int32), axis=1)
    te = jnp.minimum(te, N_EXPERTS - 1).astype(jnp.int32)
    rbase = tile_lo - start[te]
    nvalid = jnp.clip(counts[te] - rbase, 0, tm)
    cb_after = cb[1:, :][:, te]
    wlo = jnp.sum((cb_after <= rbase[None, :]).astype(jnp.int32), axis=0)
    whi = jnp.sum((cb_after <= (rbase + nvalid - 1)[None, :]).astype(jnp.int32), axis=0)
    nwin = jnp.where(nvalid > 0, whi - wlo + 1, 0).astype(jnp.int32)
    wlo = jnp.minimum(wlo, nb - 1).astype(jnp.int32)
    nsub = tm // MOE_ROW_BLOCK
    r0 = rbase[:, None] + jnp.arange(nsub, dtype=jnp.int32)[None, :] * MOE_ROW_BLOCK
    nv = jnp.clip(counts[te][:, None] - r0, 0, MOE_ROW_BLOCK)
    sub_a = jnp.sum((cb_after[:, :, None] <= r0[None]).astype(jnp.int32), axis=0)
    sub_b = jnp.sum((cb_after[:, :, None] <= (r0 + nv - 1)[None]).astype(jnp.int32), axis=0)
    sublo = jnp.where(nv > 0, sub_a - wlo[:, None], 0).astype(jnp.int32)
    subhi = jnp.where(nv > 0, sub_b - wlo[:, None], -1).astype(jnp.int32)
    ys = _moe_grouped(hn, rankt, gatet, te, n_used.reshape(1), wlo, nwin, rbase.astype(jnp.int32),
                      sublo.reshape(-1), subhi.reshape(-1), nt,
                      w1.astype(BF16), w3.astype(BF16), w2)
    tt = min(COMBINE_TILE, T)
    assert T % tt == 0 and tt % gw == 0
    cbt = cb[::tt // gw]
    first = start[None, :] + cbt[:-1, :]
    need = cbt[1:, :] - cbt[:-1, :]
    ws = first // BF16_ROWS * BF16_ROWS
    nblk = jnp.where(need > 0, (first - ws + need + COMBINE_WIN - 1) // COMBINE_WIN, 0)
    nround = jnp.maximum(jnp.max(nblk, axis=1), 1)
    return _combine(x2, rank, start, ys, ws.reshape(-1).astype(jnp.int32), nblk.reshape(-1).astype(jnp.int32),
                    nround.astype(jnp.int32), final_g, tt)


def kernel(x, mix_norm_g, w_in, conv_w, conv_b, w_q, w_k, w_v, w_if, b_if, skip_m, mh_norm_g, gm_v_g, w_sp, b_sp, gm_out_g, w_out, ffn_norm_g, dense_w1, dense_w3, dense_w2, moe_router, moe_w1, moe_w3, moe_w2, final_norm_g):
    B, S, D = x.shape
    depth = w_in.shape[0]
    assert depth == 2 and dense_w1.shape[0] == 1 and moe_w1.shape[0] == 1
    for l in range(depth):
        x = _mixer(x, mix_norm_g[l], w_in[l], conv_w[l], conv_b[l], w_q[l], w_k[l], w_v[l],
                   w_if[l], b_if[l], skip_m[l], mh_norm_g[l], gm_v_g[l], w_sp[l], b_sp[l],
                   gm_out_g[l], w_out[l])
        x2 = x.reshape(B * S, D)
        if l % 2 == 0:
            x = _ffn_dense(x2, ffn_norm_g[l], dense_w1[l // 2], dense_w3[l // 2],
                           dense_w2[l // 2]).reshape(B, S, D)
        else:
            x = _moe_layer(x2, ffn_norm_g[l], moe_router[l // 2], moe_w1[l // 2], moe_w3[l // 2],
                           moe_w2[l // 2], final_norm_g).reshape(B, S, D)
    return x
```

```python
import numpy as np
import jax
import jax.numpy as jnp
from jax import lax
from jax.experimental import pallas as pl
from jax.experimental.pallas import tpu as pltpu

F32 = jnp.float32
BF16 = jnp.bfloat16
EPS = 1e-6
NEG = -1e30

D_MODEL = 1024
M_HEADS = 4
HEAD_DIM = 128
M_WIDTH = M_HEADS * HEAD_DIM
G_GROUPS = 4
G_CH = 128
G_WIDTH = G_GROUPS * G_CH
G_BLOCK = 128
G_CHUNK = 64
CONV_W = 4
N_EXPERTS = 8
LANES = 128
SUBLANES = 8
BF16_ROWS = 16

MLSTM_CHUNK = 128
MIX_BLOCK = 256
MIX_BATCH = 4
MIX_STAGGER = 6
FFN_BLOCK = 512
FFN_CHUNK = 256
ROUTER_BLOCK = 512
TOKEN_WINDOW = 256
MOE_TILE = 512
MOE_FF_TILE = 1792
MOE_FF_CHUNK = 256
MOE_ROW_BLOCK = 128
MOE_GATHER_WINS = 4
MOE_WIN_BUFS = 12
COMBINE_TILE = 512
COMBINE_WIN = 192
VMEM_LIMIT = 56 * 1024 * 1024


def _dot(a, b):
    return jnp.dot(a, b, preferred_element_type=F32)


def _dot_nt(a, b):
    return lax.dot_general(a, b, (((1,), (1,)), ((), ())), preferred_element_type=F32)


def _dot_tn(a, b):
    return lax.dot_general(a, b, (((0,), (0,)), ((), ())), preferred_element_type=F32)


def _rms_scale(x):
    return lax.rsqrt(jnp.mean(x * x, axis=-1, keepdims=True) + EPS)


def _sigmoid(x):
    return 1.0 / (1.0 + jnp.exp(-x))


def _silu(x):
    return x * _sigmoid(x)


def _gelu_tanh(x):
    c = np.float32(np.sqrt(2.0 / np.pi))
    return 0.5 * x * (1.0 + jnp.tanh(c * (x + 0.044715 * (x * x * x))))


def _log_sigmoid(x):
    return jnp.minimum(x, 0.0) - jnp.log1p(jnp.exp(-jnp.abs(x)))


def _split3(x):
    hi = x.astype(BF16)
    r1 = x - hi.astype(F32)
    mid = r1.astype(BF16)
    lo = (r1 - mid.astype(F32)).astype(BF16)
    return hi, mid, lo


def _mixer_kernel(x_ref, *refs):
    weights = refs[:16]
    o_ref = refs[16]
    xm_ext, ct_ref, n_ref, m_ref = refs[17:21]
    scratch = refs[17:]

    @pl.when(pl.program_id(1) == 0)
    def _():
        xm_ext[:, 0:SUBLANES, :] = jnp.zeros((xm_ext.shape[0], SUBLANES, M_WIDTH), F32)
        ct_ref[...] = jnp.zeros(ct_ref.shape, F32)
        n_ref[...] = jnp.zeros(n_ref.shape, F32)
        m_ref[...] = jnp.zeros(m_ref.shape, F32)

    rows = [_mixer_block(x_ref.at[bb], *weights, o_ref.at[bb], *[r.at[bb] for r in scratch])
            for bb in range(x_ref.shape[0])]
    live = list(range(len(rows)))
    tick = 0
    while live:
        for bb in list(live):
            if tick >= bb * MIX_STAGGER and next(rows[bb], "done") == "done":
                live.remove(bb)
        tick += 1


def _mixer_block(x_ref, g_ref, win_ref, convw_ref, convb_ref, wqk_ref, wv_ref,
                 wif_ref, bif_ref, tcol_ref,
                 skip_ref, mhg_ref, gvg_ref, wsp_ref, bsp_ref, gog_ref, wout_ref,
                 o_ref,
                 xm_ext, ct_ref, n_ref, m_ref, y_ref, yg_ref):
    sb = x_ref.shape[0]
    L = MLSTM_CHUNK

    x = x_ref[...]
    h = (x * _rms_scale(x) * g_ref[...]).astype(BF16)
    yield
    x_m = _dot(h, win_ref[:, 0:M_WIDTH])
    yield

    xm_ext[SUBLANES:SUBLANES + sb, :] = x_m
    acc = jnp.zeros((sb, M_WIDTH), F32) + convb_ref[...]
    for j in range(CONV_W):
        off = SUBLANES - (CONV_W - 1) + j
        acc = acc + convw_ref[j:j + 1, :] * xm_ext[off:off + sb, :]
    x_c = _silu(acc)
    xm_ext[0:SUBLANES, :] = xm_ext[sb:sb + SUBLANES, :]
    xc_b = x_c.astype(BF16)
    xm_b = x_m.astype(BF16)
    yield
    gm_v = _dot(h, win_ref[:, 2 * M_WIDTH + G_WIDTH:])
    yield

    qs, ks, vs = [], [], []
    for hd in range(M_HEADS):
        sl = slice(hd * HEAD_DIM, (hd + 1) * HEAD_DIM)
        qk = _dot(xc_b[:, sl], wqk_ref[hd])
        qs.append(qk[:, 0:HEAD_DIM])
        ks.append(qk[:, HEAD_DIM:])
    for pr in range(M_HEADS // 2):
        sl = slice(2 * pr * HEAD_DIM, 2 * (pr + 1) * HEAD_DIM)
        vv = _dot(xm_b[:, sl], wv_ref[pr])
        vs.append(vv[:, 0:HEAD_DIM])
        vs.append(vv[:, HEAD_DIM:])
    yield
    qkv_b = jnp.concatenate([t.astype(BF16) for t in qs + ks + vs], axis=1)

    vg = _gelu_tanh(gm_v)
    mu = jnp.mean(vg, axis=1, keepdims=True)
    dev = vg - mu
    var = jnp.mean(dev * dev, axis=1, keepdims=True)
    vn = (dev * lax.rsqrt(var + EPS) * gvg_ref[...]).astype(BF16)
    yield

    gcol = _dot(qkv_b, wif_ref[...]) + bif_ref[...]
    gm_u = _dot(h, win_ref[:, 2 * M_WIDTH:2 * M_WIDTH + G_WIDTH])
    yield
    lf_col = _log_sigmoid(gcol)
    ch, cm, cl = _split3(lf_col)
    ug = _gelu_tanh(gm_u)
    yield
    bcol3 = _dot(tcol_ref[...], jnp.concatenate([ch, cm, cl], axis=1))
    bcol = bcol3[:, 0:LANES] + bcol3[:, LANES:2 * LANES] + bcol3[:, 2 * LANES:]
    grow = gcol.T
    brow = bcol.T
    yield

    for i in range(sb // G_BLOCK):
        rs = slice(i * G_BLOCK, (i + 1) * G_BLOCK)
        for gi in range(G_GROUPS):
            sl = slice(gi * G_CH, (gi + 1) * G_CH)
            sv = _dot(wsp_ref[gi], vn[rs, sl]) + bsp_ref[:, sl]
            yg_ref[rs, sl] = ug[rs, sl] * sv
    yield
    yg = yg_ref[...]
    y_ref[:, M_WIDTH:] = (yg * _rms_scale(yg) * gog_ref[...]).astype(BF16)
    o_pre = _dot(h, win_ref[:, M_WIDTH:2 * M_WIDTH])
    yield

    scale = np.float32(HEAD_DIM ** -0.5)
    o_gate = _sigmoid(o_pre)
    units = [(j, hd) for j in range(sb // L) for hd in range(M_HEADS)]
    rows = {u: slice(u[0] * L, (u[0] + 1) * L) for u in units}
    lanes = {u: slice(u[1] * HEAD_DIM, (u[1] + 1) * HEAD_DIM) for u in units}
    tril = (lax.broadcasted_iota(jnp.int32, (L, L), 1) <= lax.broadcasted_iota(jnp.int32, (L, L), 0))
    b_col = {u: bcol[rows[u], M_HEADS + u[1]:M_HEADS + u[1] + 1] for u in units}
    ig_col = {u: gcol[rows[u], u[1]:u[1] + 1] for u in units}
    c_row = {u: brow[M_HEADS + u[1]:M_HEADS + u[1] + 1, rows[u]] - grow[u[1]:u[1] + 1, rows[u]] for u in units}
    b_end = {u: b_col[u][L - 1:L, :] for u in units}
    c_max = {u: jnp.max(-c_row[u], axis=1, keepdims=True) for u in units}
    m_in, m_out = {}, {}
    for hd in range(M_HEADS):
        m = m_ref[hd][:, 0:1]
        for j in range(sb // L):
            m_in[(j, hd)] = m
            m = b_end[(j, hd)] + jnp.maximum(m, c_max[(j, hd)])
            m_out[(j, hd)] = m
        m_ref[hd] = jnp.broadcast_to(m, (1, LANES))
    yield

    d = {u: jnp.where(tril, b_col[u] - c_row[u], NEG) for u in units}
    d_max = {u: jnp.max(d[u], axis=1, keepdims=True) for u in units}
    yield
    w, a, m_t, a_prev, w_s = {}, {}, {}, {}, {}
    for u in units:
        inter = b_col[u] + m_in[u]
        m_t[u] = jnp.maximum(inter, d_max[u])
        w[u] = jnp.exp(d[u] - m_t[u])
        a[u] = jnp.exp(inter - m_t[u])
        a_prev[u] = jnp.exp(b_end[u] + m_in[u] - m_out[u])
        w_s[u] = jnp.exp(b_end[u] - m_out[u] - (b_col[u] - ig_col[u]))
    yield

    qf = {u: qs[u[1]][rows[u]] * scale for u in units}
    qb = {u: qf[u].astype(BF16) for u in units}
    kb = {u: ks[u[1]][rows[u]].astype(BF16) for u in units}
    vb = {u: vs[u[1]][rows[u]].astype(BF16) for u in units}
    qk = {u: _dot_nt(qb[u], kb[u]) for u in units}
    s = {u: qk[u] * w[u] for u in units}
    yield
    sv = {u: _dot(s[u].astype(BF16), vb[u]) for u in units}
    s_sum = {u: jnp.sum(s[u], axis=1, keepdims=True) for u in units}
    yield
    kw = {u: ks[u[1]][rows[u]] * w_s[u] for u in units}
    upd = {u: _dot_tn(kw[u].astype(BF16), vb[u]) for u in units}
    k_sum = {u: jnp.sum(kw[u], axis=0, keepdims=True) for u in units}
    yield
    ct_in, n_in = {}, {}
    for hd in range(M_HEADS):
        ct = ct_ref[hd]
        n = n_ref[hd]
        for j in range(sb // L):
            u = (j, hd)
            ct_in[u] = ct
            n_in[u] = n
            ct = a_prev[u] * ct + upd[u]
            n = a_prev[u] * n + k_sum[u]
        ct_ref[hd] = ct
        n_ref[hd] = n
    qc = {u: _dot(qb[u], ct_in[u].astype(BF16)) for u in units}
    qn = {u: jnp.sum(qf[u] * n_in[u], axis=1, keepdims=True) for u in units}
    yield
    hh = {}
    for u in units:
        num = a[u] * qc[u] + sv[u]
        den = jnp.maximum(jnp.abs(a[u] * qn[u] + s_sum[u]), jnp.exp(-m_t[u]))
        hh[u] = o_gate[rows[u], lanes[u]] * (num / den)
    mu = {u: jnp.mean(hh[u], axis=1, keepdims=True) for u in units}
    yield
    dev = {u: hh[u] - mu[u] for u in units}
    var = {u: jnp.mean(dev[u] * dev[u], axis=1, keepdims=True) for u in units}
    yield
    for u in units:
        y_m = (dev[u] * lax.rsqrt(var[u] + EPS) * mhg_ref[:, lanes[u]]
               + skip_ref[:, lanes[u]] * x_c[rows[u], lanes[u]])
        y_ref[rows[u], lanes[u]] = y_m.astype(BF16)
    yield

    o_ref[...] = x + _dot(y_ref[...], wout_ref[...])


def _const_spec(shape):
    nd = len(shape)
    return pl.BlockSpec(shape, lambda *_: (0,) * nd, pipeline_mode=pl.Buffered(1))


def _chunk_cumsum_matrix(n, chunk):
    i = np.arange(n)[:, None]
    j = np.arange(n)[None, :]
    return ((j <= i) & (i // chunk == j // chunk)).astype(np.float32)


def _mixer(x, g, w_in, conv_w, conv_b, w_q, w_k, w_v, w_if, b_if, skip_m, mh_norm_g,
           gm_v_g, w_sp, b_sp, gm_out_g, w_out):
    B, S, D = x.shape
    sb = min(MIX_BLOCK, S)
    assert S % sb == 0 and sb % MLSTM_CHUNK == 0 and sb % G_BLOCK == 0
    tcol = _chunk_cumsum_matrix(sb, MLSTM_CHUNK)
    wif = jnp.pad(w_if, ((0, 0), (0, LANES - 2 * M_HEADS)))
    bif = jnp.pad(b_if, (0, LANES - 2 * M_HEADS)).reshape(1, LANES)
    wqk = jnp.concatenate([w_q, w_k], axis=2)
    zero = jnp.zeros((HEAD_DIM, HEAD_DIM), w_v.dtype)
    wv2 = jnp.stack([jnp.block([[w_v[2 * p], zero], [zero, w_v[2 * p + 1]]]) for p in range(M_HEADS // 2)])
    pos = np.arange(G_BLOCK)
    chunk_mask = (pos[:, None] // G_CHUNK) >= (pos[None, :] // G_CHUNK)
    wsp = jnp.where(chunk_mask[None], w_sp, 0.0).astype(BF16)
    bsp = jnp.repeat(b_sp.T, G_CH, axis=1)
    row = lambda a: a.reshape(1, -1).astype(F32)
    args = (x, row(g), w_in.astype(BF16), conv_w.astype(F32), row(conv_b),
            wqk.astype(BF16), wv2.astype(BF16),
            wif.astype(BF16), bif.astype(F32),
            jnp.asarray(tcol, BF16),
            row(skip_m), row(mh_norm_g), row(gm_v_g), wsp, bsp.astype(F32), row(gm_out_g),
            w_out.astype(BF16))
    nbm = MIX_BATCH if B % MIX_BATCH == 0 else 1
    in_specs = [pl.BlockSpec((nbm, sb, D), lambda b, s: (b, s, 0))]
    in_specs += [_const_spec(a.shape) for a in args[1:]]
    return pl.pallas_call(
        _mixer_kernel,
        grid=(B // nbm, S // sb),
        in_specs=in_specs,
        out_specs=pl.BlockSpec((nbm, sb, D), lambda b, s: (b, s, 0)),
        out_shape=jax.ShapeDtypeStruct((B, S, D), F32),
        scratch_shapes=[
            pltpu.VMEM((nbm, sb + SUBLANES, M_WIDTH), F32),
            pltpu.VMEM((nbm, M_HEADS, HEAD_DIM, HEAD_DIM), F32),
            pltpu.VMEM((nbm, M_HEADS, 1, HEAD_DIM), F32),
            pltpu.VMEM((nbm, M_HEADS, 1, LANES), F32),
            pltpu.VMEM((nbm, sb, M_WIDTH + G_WIDTH), BF16),
            pltpu.VMEM((nbm, sb, G_WIDTH), F32),
        ],
        compiler_params=pltpu.CompilerParams(
            dimension_semantics=("arbitrary", "arbitrary"), vmem_limit_bytes=VMEM_LIMIT),
        name="mixer",
    )(*args)


def _ffn_kernel(x_ref, g_ref, w1_ref, w3_ref, w2_ref, o_ref):
    x = x_ref[...]
    hn = (x * _rms_scale(x) * g_ref[...]).astype(BF16)
    acc = x
    ff = w1_ref.shape[1]
    for lo in range(0, ff, FFN_CHUNK):
        sl = slice(lo, min(lo + FFN_CHUNK, ff))
        h1 = _dot(hn, w1_ref[:, sl])
        h3 = _dot(hn, w3_ref[:, sl])
        acc = acc + _dot((_silu(h1) * h3).astype(BF16), w2_ref[sl, :])
    o_ref[...] = acc


def _ffn_dense(x2, g, w1, w3, w2):
    T, D = x2.shape
    ff = w1.shape[1]
    assert ff % 64 == 0
    tm = min(FFN_BLOCK, T)
    assert T % tm == 0
    return pl.pallas_call(
        _ffn_kernel,
        grid=(T // tm,),
        in_specs=[pl.BlockSpec((tm, D), lambda i: (i, 0)),
                  _const_spec((1, D)), _const_spec((D, ff)), _const_spec((D, ff)), _const_spec((ff, D))],
        out_specs=pl.BlockSpec((tm, D), lambda i: (i, 0)),
        out_shape=jax.ShapeDtypeStruct((T, D), F32),
        compiler_params=pltpu.CompilerParams(
            dimension_semantics=("arbitrary",), vmem_limit_bytes=VMEM_LIMIT),
        name="ffn_dense",
    )(x2, g.reshape(1, D).astype(F32), w1.astype(BF16), w3.astype(BF16), w2.astype(BF16))


def _router_kernel(x_ref, g_ref, wr_ref, tri_ref, hn_ref, rank_ref, rankt_ref, gatet_ref, cnt_ref,
                   carry_ref):
    @pl.when(pl.program_id(0) == 0)
    def _():
        carry_ref[...] = jnp.zeros(carry_ref.shape, F32)

    x = x_ref[...]
    hn = x * _rms_scale(x) * g_ref[...]
    hi = hn.astype(BF16)
    hn_ref[...] = hi
    tb = x.shape[0]
    lane = lax.broadcasted_iota(jnp.int32, (tb, LANES), 1).astype(F32)
    lo = (hn - hi.astype(F32)).astype(BF16)
    hw = _dot(hi, wr_ref[...])
    logits = hw[:, 0:LANES] + hw[:, LANES:] + _dot(lo, wr_ref[:, 0:LANES])
    logits = jnp.where(lane < float(N_EXPERTS), logits, NEG)
    m1 = jnp.max(logits, axis=1, keepdims=True)
    i1 = jnp.min(jnp.where(logits == m1, lane, float(LANES)), axis=1, keepdims=True)
    rest = jnp.where(lane == i1, NEG, logits)
    m2 = jnp.max(rest, axis=1, keepdims=True)
    i2 = jnp.min(jnp.where(rest == m2, lane, float(LANES)), axis=1, keepdims=True)
    r = jnp.exp(m2 - m1)
    g1 = 1.0 / (1.0 + r)
    g2 = r * g1
    sel1 = lane == i1
    sel2 = lane == i2
    gates = jnp.where(sel1, g1, jnp.where(sel2, g2, 0.0))
    sel = jnp.where(sel1, 1.0, jnp.where(sel2, 1.0, 0.0))
    before = _dot(tri_ref[...], sel.astype(BF16))
    rank = jnp.where(sel > 0.0, carry_ref[...] + before, -1.0)
    rank_ref[...] = rank[:, 0:N_EXPERTS].astype(jnp.int32)
    rank_t = rank.T
    gates_t = gates.T
    gw = rankt_ref.shape[2]
    carry = carry_ref[...]
    for wi in range(tb // gw):
        ws = slice(wi * gw, (wi + 1) * gw)
        rankt_ref[wi] = rank_t[0:N_EXPERTS, ws]
        gatet_ref[wi] = gates_t[0:N_EXPERTS, ws]
        carry = carry + jnp.sum(sel[ws, :], axis=0, keepdims=True)
        cnt_ref[wi] = carry
    carry_ref[...] = carry


def _router(x2, g, w_router):
    T, D = x2.shape
    tb = min(ROUTER_BLOCK, T)
    gw = min(TOKEN_WINDOW, tb)
    assert T % tb == 0 and tb % gw == 0
    nb = T // tb
    nwb = tb // gw
    tri = np.tril(np.ones((tb, tb), np.float32), -1)
    wr = jnp.pad(w_router.astype(F32), ((0, 0), (0, LANES - N_EXPERTS)))
    wh = wr.astype(BF16)
    wl = (wr - wh.astype(F32)).astype(BF16)
    return pl.pallas_call(
        _router_kernel,
        grid=(nb,),
        in_specs=[pl.BlockSpec((tb, D), lambda i: (i, 0)),
                  _const_spec((1, D)), _const_spec((D, 2 * LANES)), _const_spec((tb, tb))],
        out_specs=[pl.BlockSpec((tb, D), lambda i: (i, 0)),
                   pl.BlockSpec((tb, N_EXPERTS), lambda i: (i, 0)),
                   pl.BlockSpec((nwb, N_EXPERTS, gw), lambda i: (i, 0, 0)),
                   pl.BlockSpec((nwb, N_EXPERTS, gw), lambda i: (i, 0, 0)),
                   pl.BlockSpec((nwb, 1, LANES), lambda i: (i, 0, 0))],
        out_shape=[jax.ShapeDtypeStruct((T, D), BF16),
                   jax.ShapeDtypeStruct((T, N_EXPERTS), jnp.int32),
                   jax.ShapeDtypeStruct((T // gw, N_EXPERTS, gw), F32),
                   jax.ShapeDtypeStruct((T // gw, N_EXPERTS, gw), F32),
                   jax.ShapeDtypeStruct((T // gw, 1, LANES), F32)],
        scratch_shapes=[pltpu.VMEM((1, LANES), F32)],
        compiler_params=pltpu.CompilerParams(
            dimension_semantics=("arbitrary",), vmem_limit_bytes=VMEM_LIMIT),
        name="router",
    )(x2, g.reshape(1, D).astype(F32), jnp.concatenate([wh, wl], axis=1), jnp.asarray(tri, BF16))


def _moe_kernel(te_ref, nused_ref, wlo_ref, nwin_ref, rbase_ref, sublo_ref, subhi_ref,
                hn_hbm, rankt_ref, gatet_ref, w1_ref, w3_ref, w2_ref,
                o_ref,
                hbuf, sem, gacc, xs_ref, acc_ref):
    i = pl.program_id(0)
    f = pl.program_id(1)
    used = i < nused_ref[0]
    tm = o_ref.shape[0]
    gw = rankt_ref.shape[2]
    nbuf = MOE_WIN_BUFS
    kwin = MOE_GATHER_WINS
    rblk = MOE_ROW_BLOCK
    nsub = tm // rblk

    def window_copy(tile, k, slot):
        w = wlo_ref[tile] + k
        return pltpu.make_async_copy(hn_hbm.at[pl.ds(pl.multiple_of(w * gw, gw), gw), :],
                                     hbuf.at[pl.ds(slot * gw, gw), :], sem.at[slot])

    def start_windows(tile, k_lo):
        for k in range(nbuf):
            @pl.when(k_lo + k < nwin_ref[tile])
            def _():
                window_copy(tile, k_lo + k, k).start()

    @pl.when(jnp.logical_and(used, f == 0))
    def _dispatch():
        @pl.when(i == 0)
        def _():
            hbuf[...] = jnp.zeros(hbuf.shape, BF16)
            start_windows(0, 0)

        e = te_ref[i]
        rb = rbase_ref[i].astype(F32)
        nwin = nwin_ref[i]
        w0 = wlo_ref[i]
        last_w = rankt_ref.shape[0] - 1
        def run_phase(p, first):
            k_lo = p * nbuf
            if not first:
                start_windows(i, k_lo)
            for k in range(nbuf):
                @pl.when(k_lo + k < nwin)
                def _():
                    window_copy(i, k_lo + k, k).wait()

            def select(s, c):
                k0 = jnp.maximum(sublo_ref[i * nsub + s], k_lo)
                k1 = jnp.minimum(subhi_ref[i * nsub + s], k_lo + nbuf - 1)
                kk = jnp.minimum(k0 + c * kwin, jnp.minimum(k_lo + nbuf, nwin) - 1)
                rel, gate = [], []
                for j in range(kwin):
                    live = jnp.logical_and(kk + j >= k0, kk + j <= k1)
                    w = jnp.minimum(w0 + kk + j, last_w)
                    rel.append(jnp.where(live, rankt_ref[w, pl.ds(e, 1), :] - rb, -1.0))
                    gate.append(gatet_ref[w, pl.ds(e, 1), :])
                rel = jnp.concatenate(rel, axis=1)
                gate = jnp.concatenate(gate, axis=1)
                rows = lax.broadcasted_iota(jnp.int32, (rblk, kwin * gw), 0).astype(F32) + float(s * rblk)
                hit = rel == rows
                gsum = jnp.sum(jnp.where(hit, gate, 0.0), axis=1, keepdims=True)
                src = hbuf[pl.ds(pl.multiple_of((kk - k_lo) * gw, gw), kwin * gw), :]
                return jnp.where(hit, 1.0, 0.0).astype(BF16), src, jnp.broadcast_to(gsum, (rblk, LANES))

            picks = [select(s, 0) for s in range(nsub)]
            prods = [_dot(onehot, src) for onehot, src, _ in picks]
            for s in range(nsub):
                rs = slice(s * rblk, (s + 1) * rblk)
                if first:
                    xs_ref[rs, :] = prods[s].astype(BF16)
                    gacc[rs, :] = picks[s][2]
                else:
                    xs_ref[rs, :] += prods[s].astype(BF16)
                    gacc[rs, :] += picks[s][2]

            for s in range(nsub):
                rs = slice(s * rblk, (s + 1) * rblk)
                k0 = jnp.maximum(sublo_ref[i * nsub + s], k_lo)
                k1 = jnp.minimum(subhi_ref[i * nsub + s], k_lo + nbuf - 1)

                def chunk_body(c, cc, s=s, rs=rs):
                    onehot, src, gsum = select(s, c)
                    xs_ref[rs, :] += _dot(onehot, src).astype(BF16)
                    gacc[rs, :] += gsum
                    return cc

                lax.fori_loop(1, jnp.maximum((k1 - k0 + kwin) // kwin, 0), chunk_body, 0)

        run_phase(0, True)
        lax.fori_loop(1, (nwin + nbuf - 1) // nbuf, lambda p, c: (run_phase(p, False), c)[1], 0)

        @pl.when(i + 1 < nused_ref[0])
        def _():
            start_windows(i + 1, 0)

    @pl.when(used)
    def _ffn():
        x = xs_ref[...]
        part = None
        for c in range(w1_ref.shape[2] // MOE_FF_CHUNK):
            sl = slice(c * MOE_FF_CHUNK, (c + 1) * MOE_FF_CHUNK)
            h1 = _dot(x, w1_ref[0, :, sl])
            h3 = _dot(x, w3_ref[0, :, sl])
            p = _dot((_silu(h1) * h3).astype(BF16), w2_ref[0, sl, :].astype(BF16))
            part = p if part is None else part + p

        @pl.when(f == 0)
        def _():
            acc_ref[...] = part

        @pl.when(f > 0)
        def _():
            acc_ref[...] += part

    @pl.when(f == pl.num_programs(1) - 1)
    def _():
        @pl.when(used)
        def _():
            o_ref[...] = (acc_ref[...] * gacc[:, 0:1]).astype(BF16)

        @pl.when(jnp.logical_not(used))
        def _():
            o_ref[...] = jnp.zeros(o_ref.shape, BF16)


def _moe_grouped(hn, rankt, gatet, tile_expert, n_used, wlo, nwin, rbase, sublo, subhi, nt, w1, w3, w2):
    T, D = hn.shape
    nbp, _, gw = rankt.shape
    E, _, ff = w1.shape
    tm, tf = MOE_TILE, MOE_FF_TILE
    assert ff % tf == 0 and tf % MOE_FF_CHUNK == 0 and tm % MOE_ROW_BLOCK == 0
    nf = ff // tf

    def fsel(i, f, nu):
        return jnp.where(i < nu[0], f, nf - 1)

    grid_spec = pltpu.PrefetchScalarGridSpec(
        num_scalar_prefetch=7,
        grid=(nt, nf),
        in_specs=[pl.BlockSpec(memory_space=pl.ANY),
                  pl.BlockSpec((nbp, N_EXPERTS, gw), lambda i, f, *_: (0, 0, 0), pipeline_mode=pl.Buffered(1)),
                  pl.BlockSpec((nbp, N_EXPERTS, gw), lambda i, f, *_: (0, 0, 0), pipeline_mode=pl.Buffered(1)),
                  pl.BlockSpec((1, D, tf), lambda i, f, te, nu, *_: (te[i], 0, fsel(i, f, nu))),
                  pl.BlockSpec((1, D, tf), lambda i, f, te, nu, *_: (te[i], 0, fsel(i, f, nu))),
                  pl.BlockSpec((1, tf, D), lambda i, f, te, nu, *_: (te[i], fsel(i, f, nu), 0))],
        out_specs=pl.BlockSpec((tm, D), lambda i, f, *_: (i, 0)),
        scratch_shapes=[pltpu.VMEM(((MOE_WIN_BUFS + MOE_GATHER_WINS - 1) * gw, D), BF16),
                        pltpu.SemaphoreType.DMA((MOE_WIN_BUFS,)),
                        pltpu.VMEM((tm, LANES), F32),
                        pltpu.VMEM((tm, D), BF16),
                        pltpu.VMEM((tm, D), F32)],
    )
    return pl.pallas_call(
        _moe_kernel,
        grid_spec=grid_spec,
        out_shape=jax.ShapeDtypeStruct((nt * tm, D), BF16),
        compiler_params=pltpu.CompilerParams(
            dimension_semantics=("arbitrary", "arbitrary"), vmem_limit_bytes=VMEM_LIMIT),
        name="moe_ffn",
    )(tile_expert, n_used, wlo, nwin, rbase, sublo, subhi, hn, rankt, gatet, w1, w3, w2)


def _combine_kernel(ws_ref, nblk_ref, nround_ref,
                    x_ref, rank_ref, start_ref, g_ref, ys_hbm, o_ref, ybuf, sem):
    j = pl.program_id(0)
    nj = pl.num_programs(0)
    tt = x_ref.shape[0]
    wc = COMBINE_WIN
    extra_slot = 2

    def window_copies(tile, rnd, slot):
        cps = []
        for e in range(N_EXPERTS):
            t = tile * N_EXPERTS + e
            s = ws_ref[t] + jnp.where(rnd < nblk_ref[t], rnd, 0) * wc
            cps.append(pltpu.make_async_copy(ys_hbm.at[pl.ds(pl.multiple_of(s, BF16_ROWS), wc), :],
                                             ybuf.at[slot, pl.ds(e * wc, wc), :], sem.at[slot]))
        return cps

    def select(rnd, slot):
        rank = rank_ref[...]
        pos = jnp.where(rank >= 0, rank + start_ref[...], -1)
        lane = lax.broadcasted_iota(jnp.int32, (tt, LANES), 1)
        base = []
        for e in range(N_EXPERTS):
            t = j * N_EXPERTS + e
            base.append(jnp.where(rnd < nblk_ref[t], ws_ref[t] + rnd * wc, -2 * N_EXPERTS * wc))
        parts = []
        for lo in range(0, N_EXPERTS * wc, LANES):
            ea, eb = lo // wc, (lo + LANES - 1) // wc
            one = jnp.where(pos[:, ea:ea + 1] == base[ea] + (lo - ea * wc) + lane, 1.0, 0.0)
            if eb != ea:
                one_b = jnp.where(pos[:, eb:eb + 1] == base[eb] + (lo - eb * wc) + lane, 1.0, 0.0)
                one = jnp.where(lane < eb * wc - lo, one, one_b)
            parts.append(one.astype(BF16))
        return _dot(jnp.concatenate(parts, axis=1), ybuf[slot])

    @pl.when(j == 0)
    def _():
        for cp in window_copies(0, 0, 0):
            cp.start()

    @pl.when(j + 1 < nj)
    def _():
        for cp in window_copies(j + 1, 0, (j + 1) % 2):
            cp.start()

    for cp in window_copies(j, 0, j % 2):
        cp.wait()
    acc = select(0, j % 2)

    def round_body(rnd, acc):
        for cp in window_copies(j, rnd, extra_slot):
            cp.start()
        for cp in window_copies(j, rnd, extra_slot):
            cp.wait()
        return acc + select(rnd, extra_slot)

    acc = lax.fori_loop(1, nround_ref[j], round_body, acc)
    x = x_ref[...] + acc
    o_ref[...] = x * _rms_scale(x) * g_ref[...]


def _combine(x2, rank, start, ys, ws, nblk, nround, g, tt):
    T, D = x2.shape
    grid_spec = pltpu.PrefetchScalarGridSpec(
        num_scalar_prefetch=3,
        grid=(T // tt,),
        in_specs=[pl.BlockSpec((tt, D), lambda j, *_: (j, 0)),
                  pl.BlockSpec((tt, N_EXPERTS), lambda j, *_: (j, 0)),
                  pl.BlockSpec((1, N_EXPERTS), lambda j, *_: (0, 0)),
                  pl.BlockSpec((1, D), lambda j, *_: (0, 0)),
                  pl.BlockSpec(memory_space=pl.ANY)],
        out_specs=pl.BlockSpec((tt, D), lambda j, *_: (j, 0)),
        scratch_shapes=[pltpu.VMEM((3, N_EXPERTS * COMBINE_WIN, D), BF16),
                        pltpu.SemaphoreType.DMA((3,))],
    )
    return pl.pallas_call(
        _combine_kernel,
        grid_spec=grid_spec,
        out_shape=jax.ShapeDtypeStruct((T, D), F32),
        compiler_params=pltpu.CompilerParams(
            dimension_semantics=("arbitrary",), vmem_limit_bytes=VMEM_LIMIT),
        name="combine",
    )(ws, nblk, nround, x2, rank, start.reshape(1, N_EXPERTS).astype(jnp.int32), g.reshape(1, D).astype(F32), ys)


def _moe_layer(x2, g, w_router, w1, w3, w2, final_g):
    T, D = x2.shape
    tm = MOE_TILE
    hn, rank, rankt, gatet, cnt = _router(x2, g, w_router)
    nb, _, gw = rankt.shape
    cb = jnp.concatenate([jnp.zeros((1, N_EXPERTS), jnp.int32),
                          cnt[:, 0, :N_EXPERTS].astype(jnp.int32)], axis=0)
    counts = cb[-1]
    padded = (counts + tm - 1) // tm * tm
    start = jnp.cumsum(padded) - padded
    nt = (2 * T) // tm + N_EXPERTS + 1
    n_used = (jnp.sum(padded) // tm).astype(jnp.int32)
    tile_lo = jnp.arange(nt, dtype=jnp.int32) * tm
    te = jnp.sum((tile_lo[:, None] >= (start + padded)[None, :]).astype(jnp.int32), axis=1)
    te = jnp.minimum(te, N_EXPERTS - 1).astype(jnp.int32)
    rbase = tile_lo - start[te]
    nvalid = jnp.clip(counts[te] - rbase, 0, tm)
    cb_after = cb[1:, :][:, te]
    wlo = jnp.sum((cb_after <= rbase[None, :]).astype(jnp.int32), axis=0)
    whi = jnp.sum((cb_after <= (rbase + nvalid - 1)[None, :]).astype(jnp.int32), axis=0)
    nwin = jnp.where(nvalid > 0, whi - wlo + 1, 0).astype(jnp.int32)
    wlo = jnp.minimum(wlo, nb - 1).astype(jnp.int32)
    nsub = tm // MOE_ROW_BLOCK
    r0 = rbase[:, None] + jnp.arange(nsub, dtype=jnp.int32)[None, :] * MOE_ROW_BLOCK
    nv = jnp.clip(counts[te][:, None] - r0, 0, MOE_ROW_BLOCK)
    sub_a = jnp.sum((cb_after[:, :, None] <= r0[None]).astype(jnp.int32), axis=0)
    sub_b = jnp.sum((cb_after[:, :, None] <= (r0 + nv - 1)[None]).astype(jnp.int32), axis=0)
    sublo = jnp.where(nv > 0, sub_a - wlo[:, None], 0).astype(jnp.int32)
    subhi = jnp.where(nv > 0, sub_b - wlo[:, None], -1).astype(jnp.int32)
    ys = _moe_grouped(hn, rankt, gatet, te, n_used.reshape(1), wlo, nwin, rbase.astype(jnp.int32),
                      sublo.reshape(-1), subhi.reshape(-1), nt,
                      w1.astype(BF16), w3.astype(BF16), w2)
    tt = min(COMBINE_TILE, T)
    assert T % tt == 0 and tt % gw == 0
    assert COMBINE_WIN % BF16_ROWS == 0 and (N_EXPERTS * COMBINE_WIN) % LANES == 0 and COMBINE_WIN >= LANES
    cbt = cb[::tt // gw]
    first = start[None, :] + cbt[:-1, :]
    need = cbt[1:, :] - cbt[:-1, :]
    ws = first // BF16_ROWS * BF16_ROWS
    nblk = jnp.where(need > 0, (first - ws + need + COMBINE_WIN - 1) // COMBINE_WIN, 0)
    nround = jnp.maximum(jnp.max(nblk, axis=1), 1)
    return _combine(x2, rank, start, ys, ws.reshape(-1).astype(jnp.int32), nblk.reshape(-1).astype(jnp.int32),
                    nround.astype(jnp.int32), final_g, tt)


def kernel(x, mix_norm_g, w_in, conv_w, conv_b, w_q, w_k, w_v, w_if, b_if, skip_m, mh_norm_g, gm_v_g, w_sp, b_sp, gm_out_g, w_out, ffn_norm_g, dense_w1, dense_w3, dense_w2, moe_router, moe_w1, moe_w3, moe_w2, final_norm_g):
    B, S, D = x.shape
    depth = w_in.shape[0]
    assert depth == 2 and dense_w1.shape[0] == 1 and moe_w1.shape[0] == 1
    for l in range(depth):
        x = _mixer(x, mix_norm_g[l], w_in[l], conv_w[l], conv_b[l], w_q[l], w_k[l], w_v[l],
                   w_if[l], b_if[l], skip_m[l], mh_norm_g[l], gm_v_g[l], w_sp[l], b_sp[l],
                   gm_out_g[l], w_out[l])
        x2 = x.reshape(B * S, D)
        if l % 2 == 0:
            x = _ffn_dense(x2, ffn_norm_g[l], dense_w1[l // 2], dense_w3[l // 2],
                           dense_w2[l // 2]).reshape(B, S, D)
        else:
            x = _moe_layer(x2, ffn_norm_g[l], moe_router[l // 2], moe_w1[l // 2], moe_w3[l // 2],
                           moe_w2[l // 2], final_norm_g).reshape(B, S, D)
    return x
```

```python
import numpy as np
import jax
import jax.numpy as jnp
from jax import lax
from jax.experimental import pallas as pl
from jax.experimental.pallas import tpu as pltpu

F32 = jnp.float32
BF16 = jnp.bfloat16
EPS = 1e-6
NEG = -1e30

D_MODEL = 1024
M_HEADS = 4
HEAD_DIM = 128
M_WIDTH = M_HEADS * HEAD_DIM
G_GROUPS = 4
G_CH = 128
G_WIDTH = G_GROUPS * G_CH
G_BLOCK = 128
G_CHUNK = 64
CONV_W = 4
N_EXPERTS = 8
LANES = 128
SUBLANES = 8
BF16_ROWS = 16

MLSTM_CHUNK = 128
MIX_BLOCK = 256
MIX_BATCH = 4
MIX_STAGGER = 6
FFN_BLOCK = 512
FFN_CHUNK = 256
ROUTER_BLOCK = 512
TOKEN_WINDOW = 256
MOE_TILE = 512
MOE_FF_TILE = 1792
MOE_FF_CHUNK = 256
MOE_ROW_BLOCK = 128
MOE_GATHER_WINS = 4
MOE_WIN_BUFS = 12
COMBINE_TILE = 512
COMBINE_WIN = 256
VMEM_LIMIT = 56 * 1024 * 1024


def _dot(a, b):
    return jnp.dot(a, b, preferred_element_type=F32)


def _dot_nt(a, b):
    return lax.dot_general(a, b, (((1,), (1,)), ((), ())), preferred_element_type=F32)


def _dot_tn(a, b):
    return lax.dot_general(a, b, (((0,), (0,)), ((), ())), preferred_element_type=F32)


def _rms_scale(x):
    return lax.rsqrt(jnp.mean(x * x, axis=-1, keepdims=True) + EPS)


def _sigmoid(x):
    return 1.0 / (1.0 + jnp.exp(-x))


def _silu(x):
    return x * _sigmoid(x)


def _gelu_tanh(x):
    c = np.float32(np.sqrt(2.0 / np.pi))
    return 0.5 * x * (1.0 + jnp.tanh(c * (x + 0.044715 * (x * x * x))))


def _log_sigmoid(x):
    return jnp.minimum(x, 0.0) - jnp.log1p(jnp.exp(-jnp.abs(x)))


def _split3(x):
    hi = x.astype(BF16)
    r1 = x - hi.astype(F32)
    mid = r1.astype(BF16)
    lo = (r1 - mid.astype(F32)).astype(BF16)
    return hi, mid, lo


def _mixer_kernel(x_ref, *refs):
    weights = refs[:16]
    o_ref = refs[16]
    xm_ext, ct_ref, n_ref, m_ref = refs[17:21]
    scratch = refs[17:]

    @pl.when(pl.program_id(1) == 0)
    def _():
        xm_ext[:, 0:SUBLANES, :] = jnp.zeros((xm_ext.shape[0], SUBLANES, M_WIDTH), F32)
        ct_ref[...] = jnp.zeros(ct_ref.shape, F32)
        n_ref[...] = jnp.zeros(n_ref.shape, F32)
        m_ref[...] = jnp.zeros(m_ref.shape, F32)

    rows = [_mixer_block(x_ref.at[bb], *weights, o_ref.at[bb], *[r.at[bb] for r in scratch])
            for bb in range(x_ref.shape[0])]
    live = list(range(len(rows)))
    tick = 0
    while live:
        for bb in list(live):
            if tick >= bb * MIX_STAGGER and next(rows[bb], "done") == "done":
                live.remove(bb)
        tick += 1


def _mixer_block(x_ref, g_ref, win_ref, convw_ref, convb_ref, wqk_ref, wv_ref,
                 wif_ref, bif_ref, tcol_ref,
                 skip_ref, mhg_ref, gvg_ref, wsp_ref, bsp_ref, gog_ref, wout_ref,
                 o_ref,
                 xm_ext, ct_ref, n_ref, m_ref, y_ref, yg_ref):
    sb = x_ref.shape[0]
    L = MLSTM_CHUNK

    x = x_ref[...]
    h = (x * _rms_scale(x) * g_ref[...]).astype(BF16)
    yield
    x_m = _dot(h, win_ref[:, 0:M_WIDTH])
    yield

    xm_ext[SUBLANES:SUBLANES + sb, :] = x_m
    acc = jnp.zeros((sb, M_WIDTH), F32) + convb_ref[...]
    for j in range(CONV_W):
        off = SUBLANES - (CONV_W - 1) + j
        acc = acc + convw_ref[j:j + 1, :] * xm_ext[off:off + sb, :]
    x_c = _silu(acc)
    xm_ext[0:SUBLANES, :] = xm_ext[sb:sb + SUBLANES, :]
    xc_b = x_c.astype(BF16)
    xm_b = x_m.astype(BF16)
    yield
    gm_v = _dot(h, win_ref[:, 2 * M_WIDTH + G_WIDTH:])
    yield

    qs, ks, vs = [], [], []
    for hd in range(M_HEADS):
        sl = slice(hd * HEAD_DIM, (hd + 1) * HEAD_DIM)
        qk = _dot(xc_b[:, sl], wqk_ref[hd])
        qs.append(qk[:, 0:HEAD_DIM])
        ks.append(qk[:, HEAD_DIM:])
    for pr in range(M_HEADS // 2):
        sl = slice(2 * pr * HEAD_DIM, 2 * (pr + 1) * HEAD_DIM)
        vv = _dot(xm_b[:, sl], wv_ref[pr])
        vs.append(vv[:, 0:HEAD_DIM])
        vs.append(vv[:, HEAD_DIM:])
    yield
    qkv_b = jnp.concatenate([t.astype(BF16) for t in qs + ks + vs], axis=1)

    vg = _gelu_tanh(gm_v)
    mu = jnp.mean(vg, axis=1, keepdims=True)
    dev = vg - mu
    var = jnp.mean(dev * dev, axis=1, keepdims=True)
    vn = (dev * lax.rsqrt(var + EPS) * gvg_ref[...]).astype(BF16)
    yield

    gcol = _dot(qkv_b, wif_ref[...]) + bif_ref[...]
    gm_u = _dot(h, win_ref[:, 2 * M_WIDTH:2 * M_WIDTH + G_WIDTH])
    yield
    lf_col = _log_sigmoid(gcol)
    ch, cm, cl = _split3(lf_col)
    ug = _gelu_tanh(gm_u)
    yield
    bcol3 = _dot(tcol_ref[...], jnp.concatenate([ch, cm, cl], axis=1))
    bcol = bcol3[:, 0:LANES] + bcol3[:, LANES:2 * LANES] + bcol3[:, 2 * LANES:]
    grow = gcol.T
    brow = bcol.T
    yield

    for i in range(sb // G_BLOCK):
        rs = slice(i * G_BLOCK, (i + 1) * G_BLOCK)
        for gi in range(G_GROUPS):
            sl = slice(gi * G_CH, (gi + 1) * G_CH)
            sv = _dot(wsp_ref[gi], vn[rs, sl]) + bsp_ref[:, sl]
            yg_ref[rs, sl] = ug[rs, sl] * sv
    yield
    yg = yg_ref[...]
    y_ref[:, M_WIDTH:] = (yg * _rms_scale(yg) * gog_ref[...]).astype(BF16)
    o_pre = _dot(h, win_ref[:, M_WIDTH:2 * M_WIDTH])
    yield

    scale = np.float32(HEAD_DIM ** -0.5)
    o_gate = _sigmoid(o_pre)
    units = [(j, hd) for j in range(sb // L) for hd in range(M_HEADS)]
    rows = {u: slice(u[0] * L, (u[0] + 1) * L) for u in units}
    lanes = {u: slice(u[1] * HEAD_DIM, (u[1] + 1) * HEAD_DIM) for u in units}
    tril = (lax.broadcasted_iota(jnp.int32, (L, L), 1) <= lax.broadcasted_iota(jnp.int32, (L, L), 0))
    b_col = {u: bcol[rows[u], M_HEADS + u[1]:M_HEADS + u[1] + 1] for u in units}
    ig_col = {u: gcol[rows[u], u[1]:u[1] + 1] for u in units}
    c_row = {u: brow[M_HEADS + u[1]:M_HEADS + u[1] + 1, rows[u]] - grow[u[1]:u[1] + 1, rows[u]] for u in units}
    b_end = {u: b_col[u][L - 1:L, :] for u in units}
    c_max = {u: jnp.max(-c_row[u], axis=1, keepdims=True) for u in units}
    m_in, m_out = {}, {}
    for hd in range(M_HEADS):
        m = m_ref[hd][:, 0:1]
        for j in range(sb // L):
            m_in[(j, hd)] = m
            m = b_end[(j, hd)] + jnp.maximum(m, c_max[(j, hd)])
            m_out[(j, hd)] = m
        m_ref[hd] = jnp.broadcast_to(m, (1, LANES))
    yield

    d = {u: jnp.where(tril, b_col[u] - c_row[u], NEG) for u in units}
    d_max = {u: jnp.max(d[u], axis=1, keepdims=True) for u in units}
    yield
    w, a, m_t, a_prev, w_s = {}, {}, {}, {}, {}
    for u in units:
        inter = b_col[u] + m_in[u]
        m_t[u] = jnp.maximum(inter, d_max[u])
        w[u] = jnp.exp(d[u] - m_t[u])
        a[u] = jnp.exp(inter - m_t[u])
        a_prev[u] = jnp.exp(b_end[u] + m_in[u] - m_out[u])
        w_s[u] = jnp.exp(b_end[u] - m_out[u] - (b_col[u] - ig_col[u]))
    yield

    qf = {u: qs[u[1]][rows[u]] * scale for u in units}
    qb = {u: qf[u].astype(BF16) for u in units}
    kb = {u: ks[u[1]][rows[u]].astype(BF16) for u in units}
    vb = {u: vs[u[1]][rows[u]].astype(BF16) for u in units}
    qk = {u: _dot_nt(qb[u], kb[u]) for u in units}
    s = {u: qk[u] * w[u] for u in units}
    yield
    sv = {u: _dot(s[u].astype(BF16), vb[u]) for u in units}
    s_sum = {u: jnp.sum(s[u], axis=1, keepdims=True) for u in units}
    yield
    kw = {u: ks[u[1]][rows[u]] * w_s[u] for u in units}
    upd = {u: _dot_tn(kw[u].astype(BF16), vb[u]) for u in units}
    k_sum = {u: jnp.sum(kw[u], axis=0, keepdims=True) for u in units}
    yield
    ct_in, n_in = {}, {}
    for hd in range(M_HEADS):
        ct = ct_ref[hd]
        n = n_ref[hd]
        for j in range(sb // L):
            u = (j, hd)
            ct_in[u] = ct
            n_in[u] = n
            ct = a_prev[u] * ct + upd[u]
            n = a_prev[u] * n + k_sum[u]
        ct_ref[hd] = ct
        n_ref[hd] = n
    qc = {u: _dot(qb[u], ct_in[u].astype(BF16)) for u in units}
    qn = {u: jnp.sum(qf[u] * n_in[u], axis=1, keepdims=True) for u in units}
    yield
    hh = {}
    for u in units:
        num = a[u] * qc[u] + sv[u]
        den = jnp.maximum(jnp.abs(a[u] * qn[u] + s_sum[u]), jnp.exp(-m_t[u]))
        hh[u] = o_gate[rows[u], lanes[u]] * (num / den)
    mu = {u: jnp.mean(hh[u], axis=1, keepdims=True) for u in units}
    yield
    dev = {u: hh[u] - mu[u] for u in units}
    var = {u: jnp.mean(dev[u] * dev[u], axis=1, keepdims=True) for u in units}
    yield
    for u in units:
        y_m = (dev[u] * lax.rsqrt(var[u] + EPS) * mhg_ref[:, lanes[u]]
               + skip_ref[:, lanes[u]] * x_c[rows[u], lanes[u]])
        y_ref[rows[u], lanes[u]] = y_m.astype(BF16)
    yield

    o_ref[...] = x + _dot(y_ref[...], wout_ref[...])


def _const_spec(shape):
    nd = len(shape)
    return pl.BlockSpec(shape, lambda *_: (0,) * nd, pipeline_mode=pl.Buffered(1))


def _chunk_cumsum_matrix(n, chunk):
    i = np.arange(n)[:, None]
    j = np.arange(n)[None, :]
    return ((j <= i) & (i // chunk == j // chunk)).astype(np.float32)


def _mixer(x, g, w_in, conv_w, conv_b, w_q, w_k, w_v, w_if, b_if, skip_m, mh_norm_g,
           gm_v_g, w_sp, b_sp, gm_out_g, w_out):
    B, S, D = x.shape
    sb = min(MIX_BLOCK, S)
    assert S % sb == 0 and sb % MLSTM_CHUNK == 0 and sb % G_BLOCK == 0
    tcol = _chunk_cumsum_matrix(sb, MLSTM_CHUNK)
    wif = jnp.pad(w_if, ((0, 0), (0, LANES - 2 * M_HEADS)))
    bif = jnp.pad(b_if, (0, LANES - 2 * M_HEADS)).reshape(1, LANES)
    wqk = jnp.concatenate([w_q, w_k], axis=2)
    zero = jnp.zeros((HEAD_DIM, HEAD_DIM), w_v.dtype)
    wv2 = jnp.stack([jnp.block([[w_v[2 * p], zero], [zero, w_v[2 * p + 1]]]) for p in range(M_HEADS // 2)])
    pos = np.arange(G_BLOCK)
    chunk_mask = (pos[:, None] // G_CHUNK) >= (pos[None, :] // G_CHUNK)
    wsp = jnp.where(chunk_mask[None], w_sp, 0.0).astype(BF16)
    bsp = jnp.repeat(b_sp.T, G_CH, axis=1)
    row = lambda a: a.reshape(1, -1).astype(F32)
    args = (x, row(g), w_in.astype(BF16), conv_w.astype(F32), row(conv_b),
            wqk.astype(BF16), wv2.astype(BF16),
            wif.astype(BF16), bif.astype(F32),
            jnp.asarray(tcol, BF16),
            row(skip_m), row(mh_norm_g), row(gm_v_g), wsp, bsp.astype(F32), row(gm_out_g),
            w_out.astype(BF16))
    nbm = MIX_BATCH if B % MIX_BATCH == 0 else 1
    in_specs = [pl.BlockSpec((nbm, sb, D), lambda b, s: (b, s, 0))]
    in_specs += [_const_spec(a.shape) for a in args[1:]]
    return pl.pallas_call(
        _mixer_kernel,
        grid=(B // nbm, S // sb),
        in_specs=in_specs,
        out_specs=pl.BlockSpec((nbm, sb, D), lambda b, s: (b, s, 0)),
        out_shape=jax.ShapeDtypeStruct((B, S, D), F32),
        scratch_shapes=[
            pltpu.VMEM((nbm, sb + SUBLANES, M_WIDTH), F32),
            pltpu.VMEM((nbm, M_HEADS, HEAD_DIM, HEAD_DIM), F32),
            pltpu.VMEM((nbm, M_HEADS, 1, HEAD_DIM), F32),
            pltpu.VMEM((nbm, M_HEADS, 1, LANES), F32),
            pltpu.VMEM((nbm, sb, M_WIDTH + G_WIDTH), BF16),
            pltpu.VMEM((nbm, sb, G_WIDTH), F32),
        ],
        compiler_params=pltpu.CompilerParams(
            dimension_semantics=("arbitrary", "arbitrary"), vmem_limit_bytes=VMEM_LIMIT),
        name="mixer",
    )(*args)


def _ffn_kernel(x_ref, g_ref, w1_ref, w3_ref, w2_ref, o_ref):
    x = x_ref[...]
    hn = (x * _rms_scale(x) * g_ref[...]).astype(BF16)
    acc = x
    ff = w1_ref.shape[1]
    for lo in range(0, ff, FFN_CHUNK):
        sl = slice(lo, min(lo + FFN_CHUNK, ff))
        h1 = _dot(hn, w1_ref[:, sl])
        h3 = _dot(hn, w3_ref[:, sl])
        acc = acc + _dot((_silu(h1) * h3).astype(BF16), w2_ref[sl, :])
    o_ref[...] = acc


def _ffn_dense(x2, g, w1, w3, w2):
    T, D = x2.shape
    ff = w1.shape[1]
    assert ff % 64 == 0
    tm = min(FFN_BLOCK, T)
    assert T % tm == 0
    return pl.pallas_call(
        _ffn_kernel,
        grid=(T // tm,),
        in_specs=[pl.BlockSpec((tm, D), lambda i: (i, 0)),
                  _const_spec((1, D)), _const_spec((D, ff)), _const_spec((D, ff)), _const_spec((ff, D))],
        out_specs=pl.BlockSpec((tm, D), lambda i: (i, 0)),
        out_shape=jax.ShapeDtypeStruct((T, D), F32),
        compiler_params=pltpu.CompilerParams(
            dimension_semantics=("arbitrary",), vmem_limit_bytes=VMEM_LIMIT),
        name="ffn_dense",
    )(x2, g.reshape(1, D).astype(F32), w1.astype(BF16), w3.astype(BF16), w2.astype(BF16))


def _router_kernel(x_ref, g_ref, wr_ref, tri_ref, hn_ref, rank_ref, rankt_ref, gatet_ref, cnt_ref,
                   carry_ref):
    @pl.when(pl.program_id(0) == 0)
    def _():
        carry_ref[...] = jnp.zeros(carry_ref.shape, F32)

    x = x_ref[...]
    hn = x * _rms_scale(x) * g_ref[...]
    hi = hn.astype(BF16)
    hn_ref[...] = hi
    tb = x.shape[0]
    lane = lax.broadcasted_iota(jnp.int32, (tb, LANES), 1).astype(F32)
    lo = (hn - hi.astype(F32)).astype(BF16)
    hw = _dot(hi, wr_ref[...])
    logits = hw[:, 0:LANES] + hw[:, LANES:] + _dot(lo, wr_ref[:, 0:LANES])
    logits = jnp.where(lane < float(N_EXPERTS), logits, NEG)
    m1 = jnp.max(logits, axis=1, keepdims=True)
    i1 = jnp.min(jnp.where(logits == m1, lane, float(LANES)), axis=1, keepdims=True)
    rest = jnp.where(lane == i1, NEG, logits)
    m2 = jnp.max(rest, axis=1, keepdims=True)
    i2 = jnp.min(jnp.where(rest == m2, lane, float(LANES)), axis=1, keepdims=True)
    r = jnp.exp(m2 - m1)
    g1 = 1.0 / (1.0 + r)
    g2 = r * g1
    sel1 = lane == i1
    sel2 = lane == i2
    gates = jnp.where(sel1, g1, jnp.where(sel2, g2, 0.0))
    sel = jnp.where(sel1, 1.0, jnp.where(sel2, 1.0, 0.0))
    before = _dot(tri_ref[...], sel.astype(BF16))
    rank = jnp.where(sel > 0.0, carry_ref[...] + before, -1.0)
    rank_ref[...] = rank[:, 0:N_EXPERTS].astype(jnp.int32)
    rank_t = rank.T
    gates_t = gates.T
    gw = rankt_ref.shape[2]
    carry = carry_ref[...]
    for wi in range(tb // gw):
        ws = slice(wi * gw, (wi + 1) * gw)
        rankt_ref[wi] = rank_t[0:N_EXPERTS, ws]
        gatet_ref[wi] = gates_t[0:N_EXPERTS, ws]
        carry = carry + jnp.sum(sel[ws, :], axis=0, keepdims=True)
        cnt_ref[wi] = carry
    carry_ref[...] = carry


def _router(x2, g, w_router):
    T, D = x2.shape
    tb = min(ROUTER_BLOCK, T)
    gw = min(TOKEN_WINDOW, tb)
    assert T % tb == 0 and tb % gw == 0
    nb = T // tb
    nwb = tb // gw
    tri = np.tril(np.ones((tb, tb), np.float32), -1)
    wr = jnp.pad(w_router.astype(F32), ((0, 0), (0, LANES - N_EXPERTS)))
    wh = wr.astype(BF16)
    wl = (wr - wh.astype(F32)).astype(BF16)
    return pl.pallas_call(
        _router_kernel,
        grid=(nb,),
        in_specs=[pl.BlockSpec((tb, D), lambda i: (i, 0)),
                  _const_spec((1, D)), _const_spec((D, 2 * LANES)), _const_spec((tb, tb))],
        out_specs=[pl.BlockSpec((tb, D), lambda i: (i, 0)),
                   pl.BlockSpec((tb, N_EXPERTS), lambda i: (i, 0)),
                   pl.BlockSpec((nwb, N_EXPERTS, gw), lambda i: (i, 0, 0)),
                   pl.BlockSpec((nwb, N_EXPERTS, gw), lambda i: (i, 0, 0)),
                   pl.BlockSpec((nwb, 1, LANES), lambda i: (i, 0, 0))],
        out_shape=[jax.ShapeDtypeStruct((T, D), BF16),
                   jax.ShapeDtypeStruct((T, N_EXPERTS), jnp.int32),
                   jax.ShapeDtypeStruct((T // gw, N_EXPERTS, gw), F32),
                   jax.ShapeDtypeStruct((T // gw, N_EXPERTS, gw), F32),
                   jax.ShapeDtypeStruct((T // gw, 1, LANES), F32)],
        scratch_shapes=[pltpu.VMEM((1, LANES), F32)],
        compiler_params=pltpu.CompilerParams(
            dimension_semantics=("arbitrary",), vmem_limit_bytes=VMEM_LIMIT),
        name="router",
    )(x2, g.reshape(1, D).astype(F32), jnp.concatenate([wh, wl], axis=1), jnp.asarray(tri, BF16))


def _moe_kernel(te_ref, nused_ref, wlo_ref, nwin_ref, rbase_ref, sublo_ref, subhi_ref,
                hn_hbm, rankt_ref, gatet_ref, w1_ref, w3_ref, w2_ref,
                o_ref,
                hbuf, sem, gacc, xs_ref, acc_ref):
    i = pl.program_id(0)
    f = pl.program_id(1)
    used = i < nused_ref[0]
    tm = o_ref.shape[0]
    gw = rankt_ref.shape[2]
    nbuf = MOE_WIN_BUFS
    kwin = MOE_GATHER_WINS
    rblk = MOE_ROW_BLOCK
    nsub = tm // rblk

    def window_copy(tile, k, slot):
        w = wlo_ref[tile] + k
        return pltpu.make_async_copy(hn_hbm.at[pl.ds(pl.multiple_of(w * gw, gw), gw), :],
                                     hbuf.at[pl.ds(slot * gw, gw), :], sem.at[slot])

    def start_windows(tile, k_lo):
        for k in range(nbuf):
            @pl.when(k_lo + k < nwin_ref[tile])
            def _():
                window_copy(tile, k_lo + k, k).start()

    @pl.when(jnp.logical_and(used, f == 0))
    def _dispatch():
        @pl.when(i == 0)
        def _():
            hbuf[...] = jnp.zeros(hbuf.shape, BF16)
            start_windows(0, 0)

        e = te_ref[i]
        rb = rbase_ref[i].astype(F32)
        nwin = nwin_ref[i]
        w0 = wlo_ref[i]
        last_w = rankt_ref.shape[0] - 1
        def run_phase(p, first):
            k_lo = p * nbuf
            if not first:
                start_windows(i, k_lo)
            for k in range(nbuf):
                @pl.when(k_lo + k < nwin)
                def _():
                    window_copy(i, k_lo + k, k).wait()

            def select(s, c):
                k0 = jnp.maximum(sublo_ref[i * nsub + s], k_lo)
                k1 = jnp.minimum(subhi_ref[i * nsub + s], k_lo + nbuf - 1)
                kk = jnp.minimum(k0 + c * kwin, jnp.minimum(k_lo + nbuf, nwin) - 1)
                rel, gate = [], []
                for j in range(kwin):
                    live = jnp.logical_and(kk + j >= k0, kk + j <= k1)
                    w = jnp.minimum(w0 + kk + j, last_w)
                    rel.append(jnp.where(live, rankt_ref[w, pl.ds(e, 1), :] - rb, -1.0))
                    gate.append(gatet_ref[w, pl.ds(e, 1), :])
                rel = jnp.concatenate(rel, axis=1)
                gate = jnp.concatenate(gate, axis=1)
                rows = lax.broadcasted_iota(jnp.int32, (rblk, kwin * gw), 0).astype(F32) + float(s * rblk)
                hit = rel == rows
                gsum = jnp.sum(jnp.where(hit, gate, 0.0), axis=1, keepdims=True)
                src = hbuf[pl.ds(pl.multiple_of((kk - k_lo) * gw, gw), kwin * gw), :]
                return jnp.where(hit, 1.0, 0.0).astype(BF16), src, jnp.broadcast_to(gsum, (rblk, LANES))

            picks = [select(s, 0) for s in range(nsub)]
            prods = [_dot(onehot, src) for onehot, src, _ in picks]
            for s in range(nsub):
                rs = slice(s * rblk, (s + 1) * rblk)
                if first:
                    xs_ref[rs, :] = prods[s].astype(BF16)
                    gacc[rs, :] = picks[s][2]
                else:
                    xs_ref[rs, :] += prods[s].astype(BF16)
                    gacc[rs, :] += picks[s][2]

            for s in range(nsub):
                rs = slice(s * rblk, (s + 1) * rblk)
                k0 = jnp.maximum(sublo_ref[i * nsub + s], k_lo)
                k1 = jnp.minimum(subhi_ref[i * nsub + s], k_lo + nbuf - 1)

                def chunk_body(c, cc, s=s, rs=rs):
                    onehot, src, gsum = select(s, c)
                    xs_ref[rs, :] += _dot(onehot, src).astype(BF16)
                    gacc[rs, :] += gsum
                    return cc

                lax.fori_loop(1, jnp.maximum((k1 - k0 + kwin) // kwin, 0), chunk_body, 0)

        run_phase(0, True)
        lax.fori_loop(1, (nwin + nbuf - 1) // nbuf, lambda p, c: (run_phase(p, False), c)[1], 0)

        @pl.when(i + 1 < nused_ref[0])
        def _():
            start_windows(i + 1, 0)

    @pl.when(used)
    def _ffn():
        x = xs_ref[...]
        part = None
        for c in range(w1_ref.shape[2] // MOE_FF_CHUNK):
            sl = slice(c * MOE_FF_CHUNK, (c + 1) * MOE_FF_CHUNK)
            h1 = _dot(x, w1_ref[0, :, sl])
            h3 = _dot(x, w3_ref[0, :, sl])
            p = _dot((_silu(h1) * h3).astype(BF16), w2_ref[0, sl, :].astype(BF16))
            part = p if part is None else part + p

        @pl.when(f == 0)
        def _():
            acc_ref[...] = part

        @pl.when(f > 0)
        def _():
            acc_ref[...] += part

    @pl.when(f == pl.num_programs(1) - 1)
    def _():
        @pl.when(used)
        def _():
            o_ref[...] = (acc_ref[...] * gacc[:, 0:1]).astype(BF16)

        @pl.when(jnp.logical_not(used))
        def _():
            o_ref[...] = jnp.zeros(o_ref.shape, BF16)


def _moe_grouped(hn, rankt, gatet, tile_expert, n_used, wlo, nwin, rbase, sublo, subhi, nt, w1, w3, w2):
    T, D = hn.shape
    nbp, _, gw = rankt.shape
    E, _, ff = w1.shape
    tm, tf = MOE_TILE, MOE_FF_TILE
    assert ff % tf == 0 and tf % MOE_FF_CHUNK == 0 and tm % MOE_ROW_BLOCK == 0
    nf = ff // tf

    def fsel(i, f, nu):
        return jnp.where(i < nu[0], f, nf - 1)

    grid_spec = pltpu.PrefetchScalarGridSpec(
        num_scalar_prefetch=7,
        grid=(nt, nf),
        in_specs=[pl.BlockSpec(memory_space=pl.ANY),
                  pl.BlockSpec((nbp, N_EXPERTS, gw), lambda i, f, *_: (0, 0, 0), pipeline_mode=pl.Buffered(1)),
                  pl.BlockSpec((nbp, N_EXPERTS, gw), lambda i, f, *_: (0, 0, 0), pipeline_mode=pl.Buffered(1)),
                  pl.BlockSpec((1, D, tf), lambda i, f, te, nu, *_: (te[i], 0, fsel(i, f, nu))),
                  pl.BlockSpec((1, D, tf), lambda i, f, te, nu, *_: (te[i], 0, fsel(i, f, nu))),
                  pl.BlockSpec((1, tf, D), lambda i, f, te, nu, *_: (te[i], fsel(i, f, nu), 0))],
        out_specs=pl.BlockSpec((tm, D), lambda i, f, *_: (i, 0)),
        scratch_shapes=[pltpu.VMEM(((MOE_WIN_BUFS + MOE_GATHER_WINS - 1) * gw, D), BF16),
                        pltpu.SemaphoreType.DMA((MOE_WIN_BUFS,)),
                        pltpu.VMEM((tm, LANES), F32),
                        pltpu.VMEM((tm, D), BF16),
                        pltpu.VMEM((tm, D), F32)],
    )
    return pl.pallas_call(
        _moe_kernel,
        grid_spec=grid_spec,
        out_shape=jax.ShapeDtypeStruct((nt * tm, D), BF16),
        compiler_params=pltpu.CompilerParams(
            dimension_semantics=("arbitrary", "arbitrary"), vmem_limit_bytes=VMEM_LIMIT),
        name="moe_ffn",
    )(tile_expert, n_used, wlo, nwin, rbase, sublo, subhi, hn, rankt, gatet, w1, w3, w2)


def _combine_kernel(ws_ref, nblk_ref, nround_ref,
                    x_ref, rank_ref, start_ref, g_ref, ys_hbm, o_ref, ybuf, sem):
    j = pl.program_id(0)
    nj = pl.num_programs(0)
    tt = x_ref.shape[0]
    wc = COMBINE_WIN
    extra_slot = 2

    def window_copies(tile, rnd, slot):
        cps = []
        for e in range(N_EXPERTS):
            t = tile * N_EXPERTS + e
            s = ws_ref[t] + jnp.where(rnd < nblk_ref[t], rnd, 0) * wc
            cps.append(pltpu.make_async_copy(ys_hbm.at[pl.ds(pl.multiple_of(s, BF16_ROWS), wc), :],
                                             ybuf.at[slot, pl.ds(e * wc, wc), :], sem.at[slot]))
        return cps

    def select(rnd, slot):
        rank = rank_ref[...]
        pos = jnp.where(rank >= 0, rank + start_ref[...], -1)
        lane = lax.broadcasted_iota(jnp.int32, (tt, wc), 1)
        parts = []
        for e in range(N_EXPERTS):
            t = j * N_EXPERTS + e
            base = jnp.where(rnd < nblk_ref[t], ws_ref[t] + rnd * wc, -2 * wc)
            parts.append(jnp.where(pos[:, e:e + 1] == base + lane, 1.0, 0.0).astype(BF16))
        return _dot(jnp.concatenate(parts, axis=1), ybuf[slot])

    @pl.when(j == 0)
    def _():
        for cp in window_copies(0, 0, 0):
            cp.start()

    @pl.when(j + 1 < nj)
    def _():
        for cp in window_copies(j + 1, 0, (j + 1) % 2):
            cp.start()

    for cp in window_copies(j, 0, j % 2):
        cp.wait()
    acc = select(0, j % 2)

    def round_body(rnd, acc):
        for cp in window_copies(j, rnd, extra_slot):
            cp.start()
        for cp in window_copies(j, rnd, extra_slot):
            cp.wait()
        return acc + select(rnd, extra_slot)

    acc = lax.fori_loop(1, nround_ref[j], round_body, acc)
    x = x_ref[...] + acc
    o_ref[...] = x * _rms_scale(x) * g_ref[...]


def _combine(x2, rank, start, ys, ws, nblk, nround, g, tt):
    T, D = x2.shape
    grid_spec = pltpu.PrefetchScalarGridSpec(
        num_scalar_prefetch=3,
        grid=(T // tt,),
        in_specs=[pl.BlockSpec((tt, D), lambda j, *_: (j, 0)),
                  pl.BlockSpec((tt, N_EXPERTS), lambda j, *_: (j, 0)),
                  pl.BlockSpec((1, N_EXPERTS), lambda j, *_: (0, 0)),
                  pl.BlockSpec((1, D), lambda j, *_: (0, 0)),
                  pl.BlockSpec(memory_space=pl.ANY)],
        out_specs=pl.BlockSpec((tt, D), lambda j, *_: (j, 0)),
        scratch_shapes=[pltpu.VMEM((3, N_EXPERTS * COMBINE_WIN, D), BF16),
                        pltpu.SemaphoreType.DMA((3,))],
    )
    return pl.pallas_call(
        _combine_kernel,
        grid_spec=grid_spec,
        out_shape=jax.ShapeDtypeStruct((T, D), F32),
        compiler_params=pltpu.CompilerParams(
            dimension_semantics=("arbitrary",), vmem_limit_bytes=VMEM_LIMIT),
        name="combine",
    )(ws, nblk, nround, x2, rank, start.reshape(1, N_EXPERTS).astype(jnp.int32), g.reshape(1, D).astype(F32), ys)


def _moe_layer(x2, g, w_router, w1, w3, w2, final_g):
    T, D = x2.shape
    tm = MOE_TILE
    hn, rank, rankt, gatet, cnt = _router(x2, g, w_router)
    nb, _, gw = rankt.shape
    cb = jnp.concatenate([jnp.zeros((1, N_EXPERTS), jnp.int32),
                          cnt[:, 0, :N_EXPERTS].astype(jnp.int32)], axis=0)
    counts = cb[-1]
    padded = (counts + tm - 1) // tm * tm
    start = jnp.cumsum(padded) - padded
    nt = (2 * T) // tm + N_EXPERTS + 1
    n_used = (jnp.sum(padded) // tm).astype(jnp.int32)
    tile_lo = jnp.arange(nt, dtype=jnp.int32) * tm
    te = jnp.sum((tile_lo[:, None] >= (start + padded)[None, :]).astype(jnp.int32), axis=1)
    te = jnp.minimum(te, N_EXPERTS - 1).astype(jnp.int32)
    rbase = tile_lo - start[te]
    nvalid = jnp.clip(counts[te] - rbase, 0, tm)
    cb_after = cb[1:, :][:, te]
    wlo = jnp.sum((cb_after <= rbase[None, :]).astype(jnp.int32), axis=0)
    whi = jnp.sum((cb_after <= (rbase + nvalid - 1)[None, :]).astype(jnp.int32), axis=0)
    nwin = jnp.where(nvalid > 0, whi - wlo + 1, 0).astype(jnp.int32)
    wlo = jnp.minimum(wlo, nb - 1).astype(jnp.int32)
    nsub = tm // MOE_ROW_BLOCK
    r0 = rbase[:, None] + jnp.arange(nsub, dtype=jnp.int32)[None, :] * MOE_ROW_BLOCK
    nv = jnp.clip(counts[te][:, None] - r0, 0, MOE_ROW_BLOCK)
    sub_a = jnp.sum((cb_after[:, :, None] <= r0[None]).astype(jnp.int32), axis=0)
    sub_b = jnp.sum((cb_after[:, :, None] <= (r0 + nv - 1)[None]).astype(jnp.int32), axis=0)
    sublo = jnp.where(nv > 0, sub_a - wlo[:, None], 0).astype(jnp.int32)
    subhi = jnp.where(nv > 0, sub_b - wlo[:, None], -1).astype(jnp.int32)
    ys = _moe_grouped(hn, rankt, gatet, te, n_used.reshape(1), wlo, nwin, rbase.astype(jnp.int32),
                      sublo.reshape(-1), subhi.reshape(-1), nt,
                      w1.astype(BF16), w3.astype(BF16), w2)
    tt = min(COMBINE_TILE, T)
    assert T % tt == 0 and tt % gw == 0
    assert COMBINE_WIN % LANES == 0
    cbt = cb[::tt // gw]
    first = start[None, :] + cbt[:-1, :]
    need = cbt[1:, :] - cbt[:-1, :]
    ws = first // BF16_ROWS * BF16_ROWS
    nblk = jnp.where(need > 0, (first - ws + need + COMBINE_WIN - 1) // COMBINE_WIN, 0)
    nround = jnp.maximum(jnp.max(nblk, axis=1), 1)
    return _combine(x2, rank, start, ys, ws.reshape(-1).astype(jnp.int32), nblk.reshape(-1).astype(jnp.int32),
                    nround.astype(jnp.int32), final_g, tt)


def kernel(x, mix_norm_g, w_in, conv_w, conv_b, w_q, w_k, w_v, w_if, b_if, skip_m, mh_norm_g, gm_v_g, w_sp, b_sp, gm_out_g, w_out, ffn_norm_g, dense_w1, dense_w3, dense_w2, moe_router, moe_w1, moe_w3, moe_w2, final_norm_g):
    B, S, D = x.shape
    depth = w_in.shape[0]
    assert depth == 2 and dense_w1.shape[0] == 1 and moe_w1.shape[0] == 1
    for l in range(depth):
        x = _mixer(x, mix_norm_g[l], w_in[l], conv_w[l], conv_b[l], w_q[l], w_k[l], w_v[l],
                   w_if[l], b_if[l], skip_m[l], mh_norm_g[l], gm_v_g[l], w_sp[l], b_sp[l],
                   gm_out_g[l], w_out[l])
        x2 = x.reshape(B * S, D)
        if l % 2 == 0:
            x = _ffn_dense(x2, ffn_norm_g[l], dense_w1[l // 2], dense_w3[l // 2],
                           dense_w2[l // 2]).reshape(B, S, D)
        else:
            x = _moe_layer(x2, ffn_norm_g[l], moe_router[l // 2], moe_w1[l // 2], moe_w3[l // 2],
                           moe_w2[l // 2], final_norm_g).reshape(B, S, D)
    return x
```

```python
import numpy as np
import jax
import jax.numpy as jnp
from jax import lax
from jax.experimental import pallas as pl
from jax.experimental.pallas import tpu as pltpu

F32 = jnp.float32
BF16 = jnp.bfloat16
EPS = 1e-6
NEG = -1e30

D_MODEL = 1024
M_HEADS = 4
HEAD_DIM = 128
M_WIDTH = M_HEADS * HEAD_DIM
G_GROUPS = 4
G_CH = 128
G_WIDTH = G_GROUPS * G_CH
G_BLOCK = 128
G_CHUNK = 64
CONV_W = 4
N_EXPERTS = 8
LANES = 128
SUBLANES = 8
BF16_ROWS = 16

MLSTM_CHUNK = 128
MIX_BLOCK = 256
MIX_BATCH = 4
MIX_STAGGER = 6
FFN_BLOCK = 1024
FFN_CHUNK = 256
ROUTER_BLOCK = 1024
TOKEN_WINDOW = 256
MOE_TILE = 512
MOE_FF_TILE = 1792
MOE_FF_CHUNK = 256
MOE_ROW_BLOCK = 128
MOE_GATHER_WINS = 4
MOE_WIN_BUFS = 12
COMBINE_TILE = 512
COMBINE_WIN = 256
VMEM_LIMIT = 56 * 1024 * 1024


def _dot(a, b):
    return jnp.dot(a, b, preferred_element_type=F32)


def _dot_nt(a, b):
    return lax.dot_general(a, b, (((1,), (1,)), ((), ())), preferred_element_type=F32)


def _dot_tn(a, b):
    return lax.dot_general(a, b, (((0,), (0,)), ((), ())), preferred_element_type=F32)


def _rms_scale(x):
    return lax.rsqrt(jnp.mean(x * x, axis=-1, keepdims=True) + EPS)


def _sigmoid(x):
    return 1.0 / (1.0 + jnp.exp(-x))


def _silu(x):
    return x * _sigmoid(x)


def _gelu_tanh(x):
    c = np.float32(np.sqrt(2.0 / np.pi))
    return 0.5 * x * (1.0 + jnp.tanh(c * (x + 0.044715 * (x * x * x))))


def _log_sigmoid(x):
    return jnp.minimum(x, 0.0) - jnp.log1p(jnp.exp(-jnp.abs(x)))


def _split3(x):
    hi = x.astype(BF16)
    r1 = x - hi.astype(F32)
    mid = r1.astype(BF16)
    lo = (r1 - mid.astype(F32)).astype(BF16)
    return hi, mid, lo


def _mixer_kernel(x_ref, *refs):
    weights = refs[:16]
    o_ref = refs[16]
    xm_ext, ct_ref, n_ref, m_ref = refs[17:21]
    scratch = refs[17:]

    @pl.when(pl.program_id(1) == 0)
    def _():
        xm_ext[:, 0:SUBLANES, :] = jnp.zeros((xm_ext.shape[0], SUBLANES, M_WIDTH), F32)
        ct_ref[...] = jnp.zeros(ct_ref.shape, F32)
        n_ref[...] = jnp.zeros(n_ref.shape, F32)
        m_ref[...] = jnp.zeros(m_ref.shape, F32)

    rows = [_mixer_block(x_ref.at[bb], *weights, o_ref.at[bb], *[r.at[bb] for r in scratch])
            for bb in range(x_ref.shape[0])]
    live = list(range(len(rows)))
    tick = 0
    while live:
        for bb in list(live):
            if tick >= bb * MIX_STAGGER and next(rows[bb], "done") == "done":
                live.remove(bb)
        tick += 1


def _mixer_block(x_ref, g_ref, win_ref, convw_ref, convb_ref, wqk_ref, wv_ref,
                 wif_ref, bif_ref, tcol_ref,
                 skip_ref, mhg_ref, gvg_ref, wsp_ref, bsp_ref, gog_ref, wout_ref,
                 o_ref,
                 xm_ext, ct_ref, n_ref, m_ref, y_ref, yg_ref):
    sb = x_ref.shape[0]
    L = MLSTM_CHUNK

    x = x_ref[...]
    h = (x * _rms_scale(x) * g_ref[...]).astype(BF16)
    yield
    x_m = _dot(h, win_ref[:, 0:M_WIDTH])
    yield

    xm_ext[SUBLANES:SUBLANES + sb, :] = x_m
    acc = jnp.zeros((sb, M_WIDTH), F32) + convb_ref[...]
    for j in range(CONV_W):
        off = SUBLANES - (CONV_W - 1) + j
        acc = acc + convw_ref[j:j + 1, :] * xm_ext[off:off + sb, :]
    x_c = _silu(acc)
    xm_ext[0:SUBLANES, :] = xm_ext[sb:sb + SUBLANES, :]
    xc_b = x_c.astype(BF16)
    xm_b = x_m.astype(BF16)
    yield
    gm_v = _dot(h, win_ref[:, 2 * M_WIDTH + G_WIDTH:])
    yield

    qs, ks, vs = [], [], []
    for hd in range(M_HEADS):
        sl = slice(hd * HEAD_DIM, (hd + 1) * HEAD_DIM)
        qk = _dot(xc_b[:, sl], wqk_ref[hd])
        qs.append(qk[:, 0:HEAD_DIM])
        ks.append(qk[:, HEAD_DIM:])
    for pr in range(M_HEADS // 2):
        sl = slice(2 * pr * HEAD_DIM, 2 * (pr + 1) * HEAD_DIM)
        vv = _dot(xm_b[:, sl], wv_ref[pr])
        vs.append(vv[:, 0:HEAD_DIM])
        vs.append(vv[:, HEAD_DIM:])
    yield
    qkv_b = jnp.concatenate([t.astype(BF16) for t in qs + ks + vs], axis=1)

    vg = _gelu_tanh(gm_v)
    mu = jnp.mean(vg, axis=1, keepdims=True)
    dev = vg - mu
    var = jnp.mean(dev * dev, axis=1, keepdims=True)
    vn = (dev * lax.rsqrt(var + EPS) * gvg_ref[...]).astype(BF16)
    yield

    gcol = _dot(qkv_b, wif_ref[...]) + bif_ref[...]
    gm_u = _dot(h, win_ref[:, 2 * M_WIDTH:2 * M_WIDTH + G_WIDTH])
    yield
    lf_col = _log_sigmoid(gcol)
    ch, cm, cl = _split3(lf_col)
    ug = _gelu_tanh(gm_u)
    yield
    bcol3 = _dot(tcol_ref[...], jnp.concatenate([ch, cm, cl], axis=1))
    bcol = bcol3[:, 0:LANES] + bcol3[:, LANES:2 * LANES] + bcol3[:, 2 * LANES:]
    grow = gcol.T
    brow = bcol.T
    yield

    for i in range(sb // G_BLOCK):
        rs = slice(i * G_BLOCK, (i + 1) * G_BLOCK)
        for gi in range(G_GROUPS):
            sl = slice(gi * G_CH, (gi + 1) * G_CH)
            sv = _dot(wsp_ref[gi], vn[rs, sl]) + bsp_ref[:, sl]
            yg_ref[rs, sl] = ug[rs, sl] * sv
    yield
    yg = yg_ref[...]
    y_ref[:, M_WIDTH:] = (yg * _rms_scale(yg) * gog_ref[...]).astype(BF16)
    o_pre = _dot(h, win_ref[:, M_WIDTH:2 * M_WIDTH])
    yield

    scale = np.float32(HEAD_DIM ** -0.5)
    o_gate = _sigmoid(o_pre)
    units = [(j, hd) for j in range(sb // L) for hd in range(M_HEADS)]
    rows = {u: slice(u[0] * L, (u[0] + 1) * L) for u in units}
    lanes = {u: slice(u[1] * HEAD_DIM, (u[1] + 1) * HEAD_DIM) for u in units}
    tril = (lax.broadcasted_iota(jnp.int32, (L, L), 1) <= lax.broadcasted_iota(jnp.int32, (L, L), 0))
    b_col = {u: bcol[rows[u], M_HEADS + u[1]:M_HEADS + u[1] + 1] for u in units}
    ig_col = {u: gcol[rows[u], u[1]:u[1] + 1] for u in units}
    c_row = {u: brow[M_HEADS + u[1]:M_HEADS + u[1] + 1, rows[u]] - grow[u[1]:u[1] + 1, rows[u]] for u in units}
    b_end = {u: b_col[u][L - 1:L, :] for u in units}
    c_max = {u: jnp.max(-c_row[u], axis=1, keepdims=True) for u in units}
    m_in, m_out = {}, {}
    for hd in range(M_HEADS):
        m = m_ref[hd][:, 0:1]
        for j in range(sb // L):
            m_in[(j, hd)] = m
            m = b_end[(j, hd)] + jnp.maximum(m, c_max[(j, hd)])
            m_out[(j, hd)] = m
        m_ref[hd] = jnp.broadcast_to(m, (1, LANES))
    yield

    d = {u: jnp.where(tril, b_col[u] - c_row[u], NEG) for u in units}
    d_max = {u: jnp.max(d[u], axis=1, keepdims=True) for u in units}
    yield
    w, a, m_t, a_prev, w_s = {}, {}, {}, {}, {}
    for u in units:
        inter = b_col[u] + m_in[u]
        m_t[u] = jnp.maximum(inter, d_max[u])
        w[u] = jnp.exp(d[u] - m_t[u])
        a[u] = jnp.exp(inter - m_t[u])
        a_prev[u] = jnp.exp(b_end[u] + m_in[u] - m_out[u])
        w_s[u] = jnp.exp(b_end[u] - m_out[u] - (b_col[u] - ig_col[u]))
    yield

    qf = {u: qs[u[1]][rows[u]] * scale for u in units}
    qb = {u: qf[u].astype(BF16) for u in units}
    kb = {u: ks[u[1]][rows[u]].astype(BF16) for u in units}
    vb = {u: vs[u[1]][rows[u]].astype(BF16) for u in units}
    qk = {u: _dot_nt(qb[u], kb[u]) for u in units}
    s = {u: qk[u] * w[u] for u in units}
    yield
    sv = {u: _dot(s[u].astype(BF16), vb[u]) for u in units}
    s_sum = {u: jnp.sum(s[u], axis=1, keepdims=True) for u in units}
    yield
    kw = {u: ks[u[1]][rows[u]] * w_s[u] for u in units}
    upd = {u: _dot_tn(kw[u].astype(BF16), vb[u]) for u in units}
    k_sum = {u: jnp.sum(kw[u], axis=0, keepdims=True) for u in units}
    yield
    ct_in, n_in = {}, {}
    for hd in range(M_HEADS):
        ct = ct_ref[hd]
        n = n_ref[hd]
        for j in range(sb // L):
            u = (j, hd)
            ct_in[u] = ct
            n_in[u] = n
            ct = a_prev[u] * ct + upd[u]
            n = a_prev[u] * n + k_sum[u]
        ct_ref[hd] = ct
        n_ref[hd] = n
    qc = {u: _dot(qb[u], ct_in[u].astype(BF16)) for u in units}
    qn = {u: jnp.sum(qf[u] * n_in[u], axis=1, keepdims=True) for u in units}
    yield
    hh = {}
    for u in units:
        num = a[u] * qc[u] + sv[u]
        den = jnp.maximum(jnp.abs(a[u] * qn[u] + s_sum[u]), jnp.exp(-m_t[u]))
        hh[u] = o_gate[rows[u], lanes[u]] * (num / den)
    mu = {u: jnp.mean(hh[u], axis=1, keepdims=True) for u in units}
    yield
    dev = {u: hh[u] - mu[u] for u in units}
    var = {u: jnp.mean(dev[u] * dev[u], axis=1, keepdims=True) for u in units}
    yield
    for u in units:
        y_m = (dev[u] * lax.rsqrt(var[u] + EPS) * mhg_ref[:, lanes[u]]
               + skip_ref[:, lanes[u]] * x_c[rows[u], lanes[u]])
        y_ref[rows[u], lanes[u]] = y_m.astype(BF16)
    yield

    o_ref[...] = x + _dot(y_ref[...], wout_ref[...])


def _const_spec(shape):
    nd = len(shape)
    return pl.BlockSpec(shape, lambda *_: (0,) * nd, pipeline_mode=pl.Buffered(1))


def _chunk_cumsum_matrix(n, chunk):
    i = np.arange(n)[:, None]
    j = np.arange(n)[None, :]
    return ((j <= i) & (i // chunk == j // chunk)).astype(np.float32)


def _mixer(x, g, w_in, conv_w, conv_b, w_q, w_k, w_v, w_if, b_if, skip_m, mh_norm_g,
           gm_v_g, w_sp, b_sp, gm_out_g, w_out):
    B, S, D = x.shape
    sb = min(MIX_BLOCK, S)
    assert S % sb == 0 and sb % MLSTM_CHUNK == 0 and sb % G_BLOCK == 0
    tcol = _chunk_cumsum_matrix(sb, MLSTM_CHUNK)
    wif = jnp.pad(w_if, ((0, 0), (0, LANES - 2 * M_HEADS)))
    bif = jnp.pad(b_if, (0, LANES - 2 * M_HEADS)).reshape(1, LANES)
    wqk = jnp.concatenate([w_q, w_k], axis=2)
    zero = jnp.zeros((HEAD_DIM, HEAD_DIM), w_v.dtype)
    wv2 = jnp.stack([jnp.block([[w_v[2 * p], zero], [zero, w_v[2 * p + 1]]]) for p in range(M_HEADS // 2)])
    pos = np.arange(G_BLOCK)
    chunk_mask = (pos[:, None] // G_CHUNK) >= (pos[None, :] // G_CHUNK)
    wsp = jnp.where(chunk_mask[None], w_sp, 0.0).astype(BF16)
    bsp = jnp.repeat(b_sp.T, G_CH, axis=1)
    row = lambda a: a.reshape(1, -1).astype(F32)
    args = (x, row(g), w_in.astype(BF16), conv_w.astype(F32), row(conv_b),
            wqk.astype(BF16), wv2.astype(BF16),
            wif.astype(BF16), bif.astype(F32),
            jnp.asarray(tcol, BF16),
            row(skip_m), row(mh_norm_g), row(gm_v_g), wsp, bsp.astype(F32), row(gm_out_g),
            w_out.astype(BF16))
    nbm = MIX_BATCH if B % MIX_BATCH == 0 else 1
    in_specs = [pl.BlockSpec((nbm, sb, D), lambda b, s: (b, s, 0))]
    in_specs += [_const_spec(a.shape) for a in args[1:]]
    return pl.pallas_call(
        _mixer_kernel,
        grid=(B // nbm, S // sb),
        in_specs=in_specs,
        out_specs=pl.BlockSpec((nbm, sb, D), lambda b, s: (b, s, 0)),
        out_shape=jax.ShapeDtypeStruct((B, S, D), F32),
        scratch_shapes=[
            pltpu.VMEM((nbm, sb + SUBLANES, M_WIDTH), F32),
            pltpu.VMEM((nbm, M_HEADS, HEAD_DIM, HEAD_DIM), F32),
            pltpu.VMEM((nbm, M_HEADS, 1, HEAD_DIM), F32),
            pltpu.VMEM((nbm, M_HEADS, 1, LANES), F32),
            pltpu.VMEM((nbm, sb, M_WIDTH + G_WIDTH), BF16),
            pltpu.VMEM((nbm, sb, G_WIDTH), F32),
        ],
        compiler_params=pltpu.CompilerParams(
            dimension_semantics=("arbitrary", "arbitrary"), vmem_limit_bytes=VMEM_LIMIT),
        name="mixer",
    )(*args)


def _ffn_kernel(x_ref, g_ref, w1_ref, w3_ref, w2_ref, o_ref):
    x = x_ref[...]
    hn = (x * _rms_scale(x) * g_ref[...]).astype(BF16)
    acc = x
    ff = w1_ref.shape[1]
    for lo in range(0, ff, FFN_CHUNK):
        sl = slice(lo, min(lo + FFN_CHUNK, ff))
        h1 = _dot(hn, w1_ref[:, sl])
        h3 = _dot(hn, w3_ref[:, sl])
        acc = acc + _dot((_silu(h1) * h3).astype(BF16), w2_ref[sl, :])
    o_ref[...] = acc


def _ffn_dense(x2, g, w1, w3, w2):
    T, D = x2.shape
    ff = w1.shape[1]
    tm = min(FFN_BLOCK, T)
    assert T % tm == 0
    return pl.pallas_call(
        _ffn_kernel,
        grid=(T // tm,),
        in_specs=[pl.BlockSpec((tm, D), lambda i: (i, 0)),
                  _const_spec((1, D)), _const_spec((D, ff)), _const_spec((D, ff)), _const_spec((ff, D))],
        out_specs=pl.BlockSpec((tm, D), lambda i: (i, 0)),
        out_shape=jax.ShapeDtypeStruct((T, D), F32),
        compiler_params=pltpu.CompilerParams(
            dimension_semantics=("arbitrary",), vmem_limit_bytes=VMEM_LIMIT),
        name="ffn_dense",
    )(x2, g.reshape(1, D).astype(F32), w1.astype(BF16), w3.astype(BF16), w2.astype(BF16))


def _router_kernel(x_ref, g_ref, wr_ref, tri_ref, hn_ref, rank_ref, rankt_ref, gatet_ref, cnt_ref,
                   carry_ref):
    @pl.when(pl.program_id(0) == 0)
    def _():
        carry_ref[...] = jnp.zeros(carry_ref.shape, F32)

    x = x_ref[...]
    hn = x * _rms_scale(x) * g_ref[...]
    hi = hn.astype(BF16)
    hn_ref[...] = hi
    tb = x.shape[0]
    lane = lax.broadcasted_iota(jnp.int32, (tb, LANES), 1).astype(F32)
    lo = (hn - hi.astype(F32)).astype(BF16)
    hw = _dot(hi, wr_ref[...])
    logits = hw[:, 0:LANES] + hw[:, LANES:] + _dot(lo, wr_ref[:, 0:LANES])
    logits = jnp.where(lane < float(N_EXPERTS), logits, NEG)
    m1 = jnp.max(logits, axis=1, keepdims=True)
    i1 = jnp.min(jnp.where(logits == m1, lane, float(LANES)), axis=1, keepdims=True)
    rest = jnp.where(lane == i1, NEG, logits)
    m2 = jnp.max(rest, axis=1, keepdims=True)
    i2 = jnp.min(jnp.where(rest == m2, lane, float(LANES)), axis=1, keepdims=True)
    r = jnp.exp(m2 - m1)
    g1 = 1.0 / (1.0 + r)
    g2 = r * g1
    sel1 = lane == i1
    sel2 = lane == i2
    gates = jnp.where(sel1, g1, jnp.where(sel2, g2, 0.0))
    sel = jnp.where(sel1, 1.0, jnp.where(sel2, 1.0, 0.0))
    before = _dot(tri_ref[...], sel.astype(BF16))
    rank = jnp.where(sel > 0.0, carry_ref[...] + before, -1.0)
    rank_ref[...] = rank[:, 0:N_EXPERTS].astype(jnp.int32)
    rank_t = rank.T
    gates_t = gates.T
    gw = rankt_ref.shape[2]
    carry = carry_ref[...]
    for wi in range(tb // gw):
        ws = slice(wi * gw, (wi + 1) * gw)
        rankt_ref[wi] = rank_t[0:N_EXPERTS, ws]
        gatet_ref[wi] = gates_t[0:N_EXPERTS, ws]
        carry = carry + jnp.sum(sel[ws, :], axis=0, keepdims=True)
        cnt_ref[wi] = carry
    carry_ref[...] = carry


def _router(x2, g, w_router):
    T, D = x2.shape
    tb = min(ROUTER_BLOCK, T)
    gw = min(TOKEN_WINDOW, tb)
    assert T % tb == 0 and tb % gw == 0
    nb = T // tb
    nwb = tb // gw
    tri = np.tril(np.ones((tb, tb), np.float32), -1)
    wr = jnp.pad(w_router.astype(F32), ((0, 0), (0, LANES - N_EXPERTS)))
    wh = wr.astype(BF16)
    wl = (wr - wh.astype(F32)).astype(BF16)
    return pl.pallas_call(
        _router_kernel,
        grid=(nb,),
        in_specs=[pl.BlockSpec((tb, D), lambda i: (i, 0)),
                  _const_spec((1, D)), _const_spec((D, 2 * LANES)), _const_spec((tb, tb))],
        out_specs=[pl.BlockSpec((tb, D), lambda i: (i, 0)),
                   pl.BlockSpec((tb, N_EXPERTS), lambda i: (i, 0)),
                   pl.BlockSpec((nwb, N_EXPERTS, gw), lambda i: (i, 0, 0)),
                   pl.BlockSpec((nwb, N_EXPERTS, gw), lambda i: (i, 0, 0)),
                   pl.BlockSpec((nwb, 1, LANES), lambda i: (i, 0, 0))],
        out_shape=[jax.ShapeDtypeStruct((T, D), BF16),
                   jax.ShapeDtypeStruct((T, N_EXPERTS), jnp.int32),
                   jax.ShapeDtypeStruct((T // gw, N_EXPERTS, gw), F32),
                   jax.ShapeDtypeStruct((T // gw, N_EXPERTS, gw), F32),
                   jax.ShapeDtypeStruct((T // gw, 1, LANES), F32)],
        scratch_shapes=[pltpu.VMEM((1, LANES), F32)],
        compiler_params=pltpu.CompilerParams(
            dimension_semantics=("arbitrary",), vmem_limit_bytes=VMEM_LIMIT),
        name="router",
    )(x2, g.reshape(1, D).astype(F32), jnp.concatenate([wh, wl], axis=1), jnp.asarray(tri, BF16))


def _moe_kernel(te_ref, nused_ref, wlo_ref, nwin_ref, rbase_ref, sublo_ref, subhi_ref,
                hn_hbm, rankt_ref, gatet_ref, w1_ref, w3_ref, w2_ref,
                o_ref,
                hbuf, sem, gacc, xs_ref, acc_ref):
    i = pl.program_id(0)
    f = pl.program_id(1)
    used = i < nused_ref[0]
    tm = o_ref.shape[0]
    gw = rankt_ref.shape[2]
    nbuf = MOE_WIN_BUFS
    kwin = MOE_GATHER_WINS
    rblk = MOE_ROW_BLOCK
    nsub = tm // rblk

    def window_copy(tile, k, slot):
        w = wlo_ref[tile] + k
        return pltpu.make_async_copy(hn_hbm.at[pl.ds(pl.multiple_of(w * gw, gw), gw), :],
                                     hbuf.at[pl.ds(slot * gw, gw), :], sem.at[slot])

    def start_windows(tile, k_lo):
        for k in range(nbuf):
            @pl.when(k_lo + k < nwin_ref[tile])
            def _():
                window_copy(tile, k_lo + k, k).start()

    @pl.when(jnp.logical_and(used, f == 0))
    def _dispatch():
        @pl.when(i == 0)
        def _():
            hbuf[...] = jnp.zeros(hbuf.shape, BF16)
            start_windows(0, 0)

        e = te_ref[i]
        rb = rbase_ref[i].astype(F32)
        nwin = nwin_ref[i]
        w0 = wlo_ref[i]
        last_w = rankt_ref.shape[0] - 1
        def run_phase(p, first):
            k_lo = p * nbuf
            if not first:
                start_windows(i, k_lo)
            for k in range(nbuf):
                @pl.when(k_lo + k < nwin)
                def _():
                    window_copy(i, k_lo + k, k).wait()

            def select(s, c):
                k0 = jnp.maximum(sublo_ref[i * nsub + s], k_lo)
                k1 = jnp.minimum(subhi_ref[i * nsub + s], k_lo + nbuf - 1)
                kk = jnp.minimum(k0 + c * kwin, jnp.minimum(k_lo + nbuf, nwin) - 1)
                rel, gate = [], []
                for j in range(kwin):
                    live = jnp.logical_and(kk + j >= k0, kk + j <= k1)
                    w = jnp.minimum(w0 + kk + j, last_w)
                    rel.append(jnp.where(live, rankt_ref[w, pl.ds(e, 1), :] - rb, -1.0))
                    gate.append(gatet_ref[w, pl.ds(e, 1), :])
                rel = jnp.concatenate(rel, axis=1)
                gate = jnp.concatenate(gate, axis=1)
                rows = lax.broadcasted_iota(jnp.int32, (rblk, kwin * gw), 0).astype(F32) + float(s * rblk)
                hit = rel == rows
                gsum = jnp.sum(jnp.where(hit, gate, 0.0), axis=1, keepdims=True)
                src = hbuf[pl.ds(pl.multiple_of((kk - k_lo) * gw, gw), kwin * gw), :]
                return jnp.where(hit, 1.0, 0.0).astype(BF16), src, jnp.broadcast_to(gsum, (rblk, LANES))

            picks = [select(s, 0) for s in range(nsub)]
            prods = [_dot(onehot, src) for onehot, src, _ in picks]
            for s in range(nsub):
                rs = slice(s * rblk, (s + 1) * rblk)
                if first:
                    xs_ref[rs, :] = prods[s].astype(BF16)
                    gacc[rs, :] = picks[s][2]
                else:
                    xs_ref[rs, :] += prods[s].astype(BF16)
                    gacc[rs, :] += picks[s][2]

            for s in range(nsub):
                rs = slice(s * rblk, (s + 1) * rblk)
                k0 = jnp.maximum(sublo_ref[i * nsub + s], k_lo)
                k1 = jnp.minimum(subhi_ref[i * nsub + s], k_lo + nbuf - 1)

                def chunk_body(c, cc, s=s, rs=rs):
                    onehot, src, gsum = select(s, c)
                    xs_ref[rs, :] += _dot(onehot, src).astype(BF16)
                    gacc[rs, :] += gsum
                    return cc

                lax.fori_loop(1, jnp.maximum((k1 - k0 + kwin) // kwin, 0), chunk_body, 0)

        run_phase(0, True)
        lax.fori_loop(1, (nwin + nbuf - 1) // nbuf, lambda p, c: (run_phase(p, False), c)[1], 0)

        @pl.when(i + 1 < nused_ref[0])
        def _():
            start_windows(i + 1, 0)

    @pl.when(used)
    def _ffn():
        x = xs_ref[...]
        part = None
        for c in range(w1_ref.shape[2] // MOE_FF_CHUNK):
            sl = slice(c * MOE_FF_CHUNK, (c + 1) * MOE_FF_CHUNK)
            h1 = _dot(x, w1_ref[0, :, sl])
            h3 = _dot(x, w3_ref[0, :, sl])
            p = _dot((_silu(h1) * h3).astype(BF16), w2_ref[0, sl, :].astype(BF16))
            part = p if part is None else part + p

        @pl.when(f == 0)
        def _():
            acc_ref[...] = part

        @pl.when(f > 0)
        def _():
            acc_ref[...] += part

    @pl.when(f == pl.num_programs(1) - 1)
    def _():
        @pl.when(used)
        def _():
            o_ref[...] = (acc_ref[...] * gacc[:, 0:1]).astype(BF16)

        @pl.when(jnp.logical_not(used))
        def _():
            o_ref[...] = jnp.zeros(o_ref.shape, BF16)


def _moe_grouped(hn, rankt, gatet, tile_expert, n_used, wlo, nwin, rbase, sublo, subhi, nt, w1, w3, w2):
    T, D = hn.shape
    nbp, _, gw = rankt.shape
    E, _, ff = w1.shape
    tm, tf = MOE_TILE, MOE_FF_TILE
    assert ff % tf == 0 and tf % MOE_FF_CHUNK == 0 and tm % MOE_ROW_BLOCK == 0
    nf = ff // tf

    def fsel(i, f, nu):
        return jnp.where(i < nu[0], f, nf - 1)

    grid_spec = pltpu.PrefetchScalarGridSpec(
        num_scalar_prefetch=7,
        grid=(nt, nf),
        in_specs=[pl.BlockSpec(memory_space=pl.ANY),
                  pl.BlockSpec((nbp, N_EXPERTS, gw), lambda i, f, *_: (0, 0, 0), pipeline_mode=pl.Buffered(1)),
                  pl.BlockSpec((nbp, N_EXPERTS, gw), lambda i, f, *_: (0, 0, 0), pipeline_mode=pl.Buffered(1)),
                  pl.BlockSpec((1, D, tf), lambda i, f, te, nu, *_: (te[i], 0, fsel(i, f, nu))),
                  pl.BlockSpec((1, D, tf), lambda i, f, te, nu, *_: (te[i], 0, fsel(i, f, nu))),
                  pl.BlockSpec((1, tf, D), lambda i, f, te, nu, *_: (te[i], fsel(i, f, nu), 0))],
        out_specs=pl.BlockSpec((tm, D), lambda i, f, *_: (i, 0)),
        scratch_shapes=[pltpu.VMEM(((MOE_WIN_BUFS + MOE_GATHER_WINS - 1) * gw, D), BF16),
                        pltpu.SemaphoreType.DMA((MOE_WIN_BUFS,)),
                        pltpu.VMEM((tm, LANES), F32),
                        pltpu.VMEM((tm, D), BF16),
                        pltpu.VMEM((tm, D), F32)],
    )
    return pl.pallas_call(
        _moe_kernel,
        grid_spec=grid_spec,
        out_shape=jax.ShapeDtypeStruct((nt * tm, D), BF16),
        compiler_params=pltpu.CompilerParams(
            dimension_semantics=("arbitrary", "arbitrary"), vmem_limit_bytes=VMEM_LIMIT),
        name="moe_ffn",
    )(tile_expert, n_used, wlo, nwin, rbase, sublo, subhi, hn, rankt, gatet, w1, w3, w2)


def _combine_kernel(ws_ref, nblk_ref, nround_ref,
                    x_ref, rank_ref, start_ref, g_ref, ys_hbm, o_ref, ybuf, sem):
    j = pl.program_id(0)
    nj = pl.num_programs(0)
    tt = x_ref.shape[0]
    wc = COMBINE_WIN
    extra_slot = 2

    def window_copies(tile, rnd, slot):
        cps = []
        for e in range(N_EXPERTS):
            t = tile * N_EXPERTS + e
            s = ws_ref[t] + jnp.where(rnd < nblk_ref[t], rnd, 0) * wc
            cps.append(pltpu.make_async_copy(ys_hbm.at[pl.ds(pl.multiple_of(s, BF16_ROWS), wc), :],
                                             ybuf.at[slot, pl.ds(e * wc, wc), :], sem.at[slot]))
        return cps

    def select(rnd, slot):
        rank = rank_ref[...]
        pos = jnp.where(rank >= 0, rank + start_ref[...], -1)
        lane = lax.broadcasted_iota(jnp.int32, (tt, wc), 1)
        parts = []
        for e in range(N_EXPERTS):
            t = j * N_EXPERTS + e
            base = jnp.where(rnd < nblk_ref[t], ws_ref[t] + rnd * wc, -2 * wc)
            parts.append(jnp.where(pos[:, e:e + 1] == base + lane, 1.0, 0.0).astype(BF16))
        return _dot(jnp.concatenate(parts, axis=1), ybuf[slot])

    @pl.when(j == 0)
    def _():
        for cp in window_copies(0, 0, 0):
            cp.start()

    @pl.when(j + 1 < nj)
    def _():
        for cp in window_copies(j + 1, 0, (j + 1) % 2):
            cp.start()

    for cp in window_copies(j, 0, j % 2):
        cp.wait()
    acc = select(0, j % 2)

    def round_body(rnd, acc):
        for cp in window_copies(j, rnd, extra_slot):
            cp.start()
        for cp in window_copies(j, rnd, extra_slot):
            cp.wait()
        return acc + select(rnd, extra_slot)

    acc = lax.fori_loop(1, nround_ref[j], round_body, acc)
    x = x_ref[...] + acc
    o_ref[...] = x * _rms_scale(x) * g_ref[...]


def _combine(x2, rank, start, ys, ws, nblk, nround, g, tt):
    T, D = x2.shape
    grid_spec = pltpu.PrefetchScalarGridSpec(
        num_scalar_prefetch=3,
        grid=(T // tt,),
        in_specs=[pl.BlockSpec((tt, D), lambda j, *_: (j, 0)),
                  pl.BlockSpec((tt, N_EXPERTS), lambda j, *_: (j, 0)),
                  pl.BlockSpec((1, N_EXPERTS), lambda j, *_: (0, 0)),
                  pl.BlockSpec((1, D), lambda j, *_: (0, 0)),
                  pl.BlockSpec(memory_space=pl.ANY)],
        out_specs=pl.BlockSpec((tt, D), lambda j, *_: (j, 0)),
        scratch_shapes=[pltpu.VMEM((3, N_EXPERTS * COMBINE_WIN, D), BF16),
                        pltpu.SemaphoreType.DMA((3,))],
    )
    return pl.pallas_call(
        _combine_kernel,
        grid_spec=grid_spec,
        out_shape=jax.ShapeDtypeStruct((T, D), F32),
        compiler_params=pltpu.CompilerParams(
            dimension_semantics=("arbitrary",), vmem_limit_bytes=VMEM_LIMIT),
        name="combine",
    )(ws, nblk, nround, x2, rank, start.reshape(1, N_EXPERTS).astype(jnp.int32), g.reshape(1, D).astype(F32), ys)


def _moe_layer(x2, g, w_router, w1, w3, w2, final_g):
    T, D = x2.shape
    tm = MOE_TILE
    hn, rank, rankt, gatet, cnt = _router(x2, g, w_router)
    nb, _, gw = rankt.shape
    cb = jnp.concatenate([jnp.zeros((1, N_EXPERTS), jnp.int32),
                          cnt[:, 0, :N_EXPERTS].astype(jnp.int32)], axis=0)
    counts = cb[-1]
    padded = (counts + tm - 1) // tm * tm
    start = jnp.cumsum(padded) - padded
    nt = (2 * T) // tm + N_EXPERTS + 1
    n_used = (jnp.sum(padded) // tm).astype(jnp.int32)
    tile_lo = jnp.arange(nt, dtype=jnp.int32) * tm
    te = jnp.sum((tile_lo[:, None] >= (start + padded)[None, :]).astype(jnp.int32), axis=1)
    te = jnp.minimum(te, N_EXPERTS - 1).astype(jnp.int32)
    rbase = tile_lo - start[te]
    nvalid = jnp.clip(counts[te] - rbase, 0, tm)
    cb_after = cb[1:, :][:, te]
    wlo = jnp.sum((cb_after <= rbase[None, :]).astype(jnp.int32), axis=0)
    whi = jnp.sum((cb_after <= (rbase + nvalid - 1)[None, :]).astype(jnp.int32), axis=0)
    nwin = jnp.where(nvalid > 0, whi - wlo + 1, 0).astype(jnp.int32)
    wlo = jnp.minimum(wlo, nb - 1).astype(jnp.int32)
    nsub = tm // MOE_ROW_BLOCK
    r0 = rbase[:, None] + jnp.arange(nsub, dtype=jnp.int32)[None, :] * MOE_ROW_BLOCK
    nv = jnp.clip(counts[te][:, None] - r0, 0, MOE_ROW_BLOCK)
    sub_a = jnp.sum((cb_after[:, :, None] <= r0[None]).astype(jnp.int32), axis=0)
    sub_b = jnp.sum((cb_after[:, :, None] <= (r0 + nv - 1)[None]).astype(jnp.int32), axis=0)
    sublo = jnp.where(nv > 0, sub_a - wlo[:, None], 0).astype(jnp.int32)
    subhi = jnp.where(nv > 0, sub_b - wlo[:, None], -1).astype(jnp.int32)
    ys = _moe_grouped(hn, rankt, gatet, te, n_used.reshape(1), wlo, nwin, rbase.astype(jnp.int32),
                      sublo.reshape(-1), subhi.reshape(-1), nt,
                      w1.astype(BF16), w3.astype(BF16), w2)
    tt = min(COMBINE_TILE, T)
    assert T % tt == 0 and tt % gw == 0
    assert COMBINE_WIN % LANES == 0
    cbt = cb[::tt // gw]
    first = start[None, :] + cbt[:-1, :]
    need = cbt[1:, :] - cbt[:-1, :]
    ws = first // BF16_ROWS * BF16_ROWS
    nblk = jnp.where(need > 0, (first - ws + need + COMBINE_WIN - 1) // COMBINE_WIN, 0)
    nround = jnp.maximum(jnp.max(nblk, axis=1), 1)
    return _combine(x2, rank, start, ys, ws.reshape(-1).astype(jnp.int32), nblk.reshape(-1).astype(jnp.int32),
                    nround.astype(jnp.int32), final_g, tt)


def kernel(x, mix_norm_g, w_in, conv_w, conv_b, w_q, w_k, w_v, w_if, b_if, skip_m, mh_norm_g, gm_v_g, w_sp, b_sp, gm_out_g, w_out, ffn_norm_g, dense_w1, dense_w3, dense_w2, moe_router, moe_w1, moe_w3, moe_w2, final_norm_g):
    B, S, D = x.shape
    depth = w_in.shape[0]
    assert depth == 2 and dense_w1.shape[0] == 1 and moe_w1.shape[0] == 1
    for l in range(depth):
        x = _mixer(x, mix_norm_g[l], w_in[l], conv_w[l], conv_b[l], w_q[l], w_k[l], w_v[l],
                   w_if[l], b_if[l], skip_m[l], mh_norm_g[l], gm_v_g[l], w_sp[l], b_sp[l],
                   gm_out_g[l], w_out[l])
        x2 = x.reshape(B * S, D)
        if l % 2 == 0:
            x = _ffn_dense(x2, ffn_norm_g[l], dense_w1[l // 2], dense_w3[l // 2],
                           dense_w2[l // 2]).reshape(B, S, D)
        else:
            x = _moe_layer(x2, ffn_norm_g[l], moe_router[l // 2], moe_w1[l // 2], moe_w3[l // 2],
                           moe_w2[l // 2], final_norm_g).reshape(B, S, D)
    return x
```

```python
import numpy as np
import jax
import jax.numpy as jnp
from jax import lax
from jax.experimental import pallas as pl
from jax.experimental.pallas import tpu as pltpu

F32 = jnp.float32
BF16 = jnp.bfloat16
EPS = 1e-6
NEG = -1e30

D_MODEL = 1024
M_HEADS = 4
HEAD_DIM = 128
M_WIDTH = M_HEADS * HEAD_DIM
G_GROUPS = 4
G_CH = 128
G_WIDTH = G_GROUPS * G_CH
G_BLOCK = 128
G_CHUNK = 64
CONV_W = 4
N_EXPERTS = 8
LANES = 128
SUBLANES = 8
BF16_ROWS = 16

MLSTM_CHUNK = 128
MIX_BLOCK = 256
MIX_BATCH = 4
MIX_STAGGER = 6
FFN_BLOCK = 1024
FFN_CHUNK = 256
ROUTER_BLOCK = 1024
TOKEN_WINDOW = 256
MOE_TILE = 512
MOE_FF_TILE = 1792
MOE_FF_CHUNK = 256
MOE_ROW_BLOCK = 128
MOE_GATHER_WINS = 4
MOE_WIN_BUFS = 12
COMBINE_TILE = 512
COMBINE_WIN = 256
WINDOW_DMA_PRIORITY = 1
VMEM_LIMIT =56 * 1024 * 1024


def _dot(a, b):
    return jnp.dot(a, b, preferred_element_type=F32)


def _dot_nt(a, b):
    return lax.dot_general(a, b, (((1,), (1,)), ((), ())), preferred_element_type=F32)


def _dot_tn(a, b):
    return lax.dot_general(a, b, (((0,), (0,)), ((), ())), preferred_element_type=F32)


def _rms_scale(x):
    return lax.rsqrt(jnp.mean(x * x, axis=-1, keepdims=True) + EPS)


def _sigmoid(x):
    return 1.0 / (1.0 + jnp.exp(-x))


def _silu(x):
    return x * _sigmoid(x)


def _gelu_tanh(x):
    c = np.float32(np.sqrt(2.0 / np.pi))
    return 0.5 * x * (1.0 + jnp.tanh(c * (x + 0.044715 * (x * x * x))))


def _log_sigmoid(x):
    return jnp.minimum(x, 0.0) - jnp.log1p(jnp.exp(-jnp.abs(x)))


def _split3(x):
    hi = x.astype(BF16)
    r1 = x - hi.astype(F32)
    mid = r1.astype(BF16)
    lo = (r1 - mid.astype(F32)).astype(BF16)
    return hi, mid, lo


def _mixer_kernel(x_ref, *refs):
    weights = refs[:16]
    o_ref = refs[16]
    xm_ext, ct_ref, n_ref, m_ref = refs[17:21]
    scratch = refs[17:]

    @pl.when(pl.program_id(1) == 0)
    def _():
        xm_ext[:, 0:SUBLANES, :] = jnp.zeros((xm_ext.shape[0], SUBLANES, M_WIDTH), F32)
        ct_ref[...] = jnp.zeros(ct_ref.shape, F32)
        n_ref[...] = jnp.zeros(n_ref.shape, F32)
        m_ref[...] = jnp.zeros(m_ref.shape, F32)

    rows = [_mixer_block(x_ref.at[bb], *weights, o_ref.at[bb], *[r.at[bb] for r in scratch])
            for bb in range(x_ref.shape[0])]
    live = list(range(len(rows)))
    tick = 0
    while live:
        for bb in list(live):
            if tick >= bb * MIX_STAGGER and next(rows[bb], "done") == "done":
                live.remove(bb)
        tick += 1


def _mixer_block(x_ref, g_ref, win_ref, convw_ref, convb_ref, wqk_ref, wv_ref,
                 wif_ref, bif_ref, tcol_ref,
                 skip_ref, mhg_ref, gvg_ref, wsp_ref, bsp_ref, gog_ref, wout_ref,
                 o_ref,
                 xm_ext, ct_ref, n_ref, m_ref, y_ref, yg_ref):
    sb = x_ref.shape[0]
    L = MLSTM_CHUNK

    x = x_ref[...]
    h = (x * _rms_scale(x) * g_ref[...]).astype(BF16)
    yield
    x_m = _dot(h, win_ref[:, 0:M_WIDTH])
    yield

    xm_ext[SUBLANES:SUBLANES + sb, :] = x_m
    acc = jnp.zeros((sb, M_WIDTH), F32) + convb_ref[...]
    for j in range(CONV_W):
        off = SUBLANES - (CONV_W - 1) + j
        acc = acc + convw_ref[j:j + 1, :] * xm_ext[off:off + sb, :]
    x_c = _silu(acc)
    xm_ext[0:SUBLANES, :] = xm_ext[sb:sb + SUBLANES, :]
    xc_b = x_c.astype(BF16)
    xm_b = x_m.astype(BF16)
    yield
    gm_v = _dot(h, win_ref[:, 2 * M_WIDTH + G_WIDTH:])
    yield

    qs, ks, vs = [], [], []
    for hd in range(M_HEADS):
        sl = slice(hd * HEAD_DIM, (hd + 1) * HEAD_DIM)
        qk = _dot(xc_b[:, sl], wqk_ref[hd])
        qs.append(qk[:, 0:HEAD_DIM])
        ks.append(qk[:, HEAD_DIM:])
    for pr in range(M_HEADS // 2):
        sl = slice(2 * pr * HEAD_DIM, 2 * (pr + 1) * HEAD_DIM)
        vv = _dot(xm_b[:, sl], wv_ref[pr])
        vs.append(vv[:, 0:HEAD_DIM])
        vs.append(vv[:, HEAD_DIM:])
    yield
    qkv_b = jnp.concatenate([t.astype(BF16) for t in qs + ks + vs], axis=1)

    vg = _gelu_tanh(gm_v)
    mu = jnp.mean(vg, axis=1, keepdims=True)
    dev = vg - mu
    var = jnp.mean(dev * dev, axis=1, keepdims=True)
    vn = (dev * lax.rsqrt(var + EPS) * gvg_ref[...]).astype(BF16)
    yield

    gcol = _dot(qkv_b, wif_ref[...]) + bif_ref[...]
    gm_u = _dot(h, win_ref[:, 2 * M_WIDTH:2 * M_WIDTH + G_WIDTH])
    yield
    lf_col = _log_sigmoid(gcol)
    ch, cm, cl = _split3(lf_col)
    ug = _gelu_tanh(gm_u)
    yield
    bcol3 = _dot(tcol_ref[...], jnp.concatenate([ch, cm, cl], axis=1))
    bcol = bcol3[:, 0:LANES] + bcol3[:, LANES:2 * LANES] + bcol3[:, 2 * LANES:]
    grow = gcol.T
    brow = bcol.T
    yield

    for i in range(sb // G_BLOCK):
        rs = slice(i * G_BLOCK, (i + 1) * G_BLOCK)
        for gi in range(G_GROUPS):
            sl = slice(gi * G_CH, (gi + 1) * G_CH)
            sv = _dot(wsp_ref[gi], vn[rs, sl]) + bsp_ref[:, sl]
            yg_ref[rs, sl] = ug[rs, sl] * sv
    yield
    yg = yg_ref[...]
    y_ref[:, M_WIDTH:] = (yg * _rms_scale(yg) * gog_ref[...]).astype(BF16)
    o_pre = _dot(h, win_ref[:, M_WIDTH:2 * M_WIDTH])
    yield

    scale = np.float32(HEAD_DIM ** -0.5)
    o_gate = _sigmoid(o_pre)
    units = [(j, hd) for j in range(sb // L) for hd in range(M_HEADS)]
    rows = {u: slice(u[0] * L, (u[0] + 1) * L) for u in units}
    lanes = {u: slice(u[1] * HEAD_DIM, (u[1] + 1) * HEAD_DIM) for u in units}
    tril = (lax.broadcasted_iota(jnp.int32, (L, L), 1) <= lax.broadcasted_iota(jnp.int32, (L, L), 0))
    b_col = {u: bcol[rows[u], M_HEADS + u[1]:M_HEADS + u[1] + 1] for u in units}
    ig_col = {u: gcol[rows[u], u[1]:u[1] + 1] for u in units}
    c_row = {u: brow[M_HEADS + u[1]:M_HEADS + u[1] + 1, rows[u]] - grow[u[1]:u[1] + 1, rows[u]] for u in units}
    b_end = {u: b_col[u][L - 1:L, :] for u in units}
    c_max = {u: jnp.max(-c_row[u], axis=1, keepdims=True) for u in units}
    m_in, m_out = {}, {}
    for hd in range(M_HEADS):
        m = m_ref[hd][:, 0:1]
        for j in range(sb // L):
            m_in[(j, hd)] = m
            m = b_end[(j, hd)] + jnp.maximum(m, c_max[(j, hd)])
            m_out[(j, hd)] = m
        m_ref[hd] = jnp.broadcast_to(m, (1, LANES))
    yield

    d = {u: jnp.where(tril, b_col[u] - c_row[u], NEG) for u in units}
    d_max = {u: jnp.max(d[u], axis=1, keepdims=True) for u in units}
    yield
    w, a, m_t, a_prev, w_s = {}, {}, {}, {}, {}
    for u in units:
        inter = b_col[u] + m_in[u]
        m_t[u] = jnp.maximum(inter, d_max[u])
        w[u] = jnp.exp(d[u] - m_t[u])
        a[u] = jnp.exp(inter - m_t[u])
        a_prev[u] = jnp.exp(b_end[u] + m_in[u] - m_out[u])
        w_s[u] = jnp.exp(b_end[u] - m_out[u] - (b_col[u] - ig_col[u]))
    yield

    qf = {u: qs[u[1]][rows[u]] * scale for u in units}
    qb = {u: qf[u].astype(BF16) for u in units}
    kb = {u: ks[u[1]][rows[u]].astype(BF16) for u in units}
    vb = {u: vs[u[1]][rows[u]].astype(BF16) for u in units}
    qk = {u: _dot_nt(qb[u], kb[u]) for u in units}
    s = {u: qk[u] * w[u] for u in units}
    yield
    sv = {u: _dot(s[u].astype(BF16), vb[u]) for u in units}
    s_sum = {u: jnp.sum(s[u], axis=1, keepdims=True) for u in units}
    yield
    kw = {u: ks[u[1]][rows[u]] * w_s[u] for u in units}
    upd = {u: _dot_tn(kw[u].astype(BF16), vb[u]) for u in units}
    k_sum = {u: jnp.sum(kw[u], axis=0, keepdims=True) for u in units}
    yield
    ct_in, n_in = {}, {}
    for hd in range(M_HEADS):
        ct = ct_ref[hd]
        n = n_ref[hd]
        for j in range(sb // L):
            u = (j, hd)
            ct_in[u] = ct
            n_in[u] = n
            ct = a_prev[u] * ct + upd[u]
            n = a_prev[u] * n + k_sum[u]
        ct_ref[hd] = ct
        n_ref[hd] = n
    qc = {u: _dot(qb[u], ct_in[u].astype(BF16)) for u in units}
    qn = {u: jnp.sum(qf[u] * n_in[u], axis=1, keepdims=True) for u in units}
    yield
    hh = {}
    for u in units:
        num = a[u] * qc[u] + sv[u]
        den = jnp.maximum(jnp.abs(a[u] * qn[u] + s_sum[u]), jnp.exp(-m_t[u]))
        hh[u] = o_gate[rows[u], lanes[u]] * (num / den)
    mu = {u: jnp.mean(hh[u], axis=1, keepdims=True) for u in units}
    yield
    dev = {u: hh[u] - mu[u] for u in units}
    var = {u: jnp.mean(dev[u] * dev[u], axis=1, keepdims=True) for u in units}
    yield
    for u in units:
        y_m = (dev[u] * lax.rsqrt(var[u] + EPS) * mhg_ref[:, lanes[u]]
               + skip_ref[:, lanes[u]] * x_c[rows[u], lanes[u]])
        y_ref[rows[u], lanes[u]] = y_m.astype(BF16)
    yield

    o_ref[...] = x + _dot(y_ref[...], wout_ref[...])


def _const_spec(shape):
    nd = len(shape)
    return pl.BlockSpec(shape, lambda *_: (0,) * nd, pipeline_mode=pl.Buffered(1))


def _chunk_cumsum_matrix(n, chunk):
    i = np.arange(n)[:, None]
    j = np.arange(n)[None, :]
    return ((j <= i) & (i // chunk == j // chunk)).astype(np.float32)


def _mixer(x, g, w_in, conv_w, conv_b, w_q, w_k, w_v, w_if, b_if, skip_m, mh_norm_g,
           gm_v_g, w_sp, b_sp, gm_out_g, w_out):
    B, S, D = x.shape
    sb = min(MIX_BLOCK, S)
    assert S % sb == 0 and sb % MLSTM_CHUNK == 0 and sb % G_BLOCK == 0
    tcol = _chunk_cumsum_matrix(sb, MLSTM_CHUNK)
    wif = jnp.pad(w_if, ((0, 0), (0, LANES - 2 * M_HEADS)))
    bif = jnp.pad(b_if, (0, LANES - 2 * M_HEADS)).reshape(1, LANES)
    wqk = jnp.concatenate([w_q, w_k], axis=2)
    zero = jnp.zeros((HEAD_DIM, HEAD_DIM), w_v.dtype)
    wv2 = jnp.stack([jnp.block([[w_v[2 * p], zero], [zero, w_v[2 * p + 1]]]) for p in range(M_HEADS // 2)])
    pos = np.arange(G_BLOCK)
    chunk_mask = (pos[:, None] // G_CHUNK) >= (pos[None, :] // G_CHUNK)
    wsp = jnp.where(chunk_mask[None], w_sp, 0.0).astype(BF16)
    bsp = jnp.repeat(b_sp.T, G_CH, axis=1)
    row = lambda a: a.reshape(1, -1).astype(F32)
    args = (x, row(g), w_in.astype(BF16), conv_w.astype(F32), row(conv_b),
            wqk.astype(BF16), wv2.astype(BF16),
            wif.astype(BF16), bif.astype(F32),
            jnp.asarray(tcol, BF16),
            row(skip_m), row(mh_norm_g), row(gm_v_g), wsp, bsp.astype(F32), row(gm_out_g),
            w_out.astype(BF16))
    nbm = MIX_BATCH if B % MIX_BATCH == 0 else 1
    in_specs = [pl.BlockSpec((nbm, sb, D), lambda b, s: (b, s, 0))]
    in_specs += [_const_spec(a.shape) for a in args[1:]]
    return pl.pallas_call(
        _mixer_kernel,
        grid=(B // nbm, S // sb),
        in_specs=in_specs,
        out_specs=pl.BlockSpec((nbm, sb, D), lambda b, s: (b, s, 0)),
        out_shape=jax.ShapeDtypeStruct((B, S, D), F32),
        scratch_shapes=[
            pltpu.VMEM((nbm, sb + SUBLANES, M_WIDTH), F32),
            pltpu.VMEM((nbm, M_HEADS, HEAD_DIM, HEAD_DIM), F32),
            pltpu.VMEM((nbm, M_HEADS, 1, HEAD_DIM), F32),
            pltpu.VMEM((nbm, M_HEADS, 1, LANES), F32),
            pltpu.VMEM((nbm, sb, M_WIDTH + G_WIDTH), BF16),
            pltpu.VMEM((nbm, sb, G_WIDTH), F32),
        ],
        compiler_params=pltpu.CompilerParams(
            dimension_semantics=("arbitrary", "arbitrary"), vmem_limit_bytes=VMEM_LIMIT),
        name="mixer",
    )(*args)


def _ffn_kernel(x_ref, g_ref, w1_ref, w3_ref, w2_ref, o_ref):
    x = x_ref[...]
    hn = (x * _rms_scale(x) * g_ref[...]).astype(BF16)
    acc = x
    ff = w1_ref.shape[1]
    for lo in range(0, ff, FFN_CHUNK):
        sl = slice(lo, min(lo + FFN_CHUNK, ff))
        h1 = _dot(hn, w1_ref[:, sl])
        h3 = _dot(hn, w3_ref[:, sl])
        acc = acc + _dot((_silu(h1) * h3).astype(BF16), w2_ref[sl, :])
    o_ref[...] = acc


def _ffn_dense(x2, g, w1, w3, w2):
    T, D = x2.shape
    ff = w1.shape[1]
    tm = min(FFN_BLOCK, T)
    assert T % tm == 0
    return pl.pallas_call(
        _ffn_kernel,
        grid=(T // tm,),
        in_specs=[pl.BlockSpec((tm, D), lambda i: (i, 0)),
                  _const_spec((1, D)), _const_spec((D, ff)), _const_spec((D, ff)), _const_spec((ff, D))],
        out_specs=pl.BlockSpec((tm, D), lambda i: (i, 0)),
        out_shape=jax.ShapeDtypeStruct((T, D), F32),
        compiler_params=pltpu.CompilerParams(
            dimension_semantics=("arbitrary",), vmem_limit_bytes=VMEM_LIMIT),
        name="ffn_dense",
    )(x2, g.reshape(1, D).astype(F32), w1.astype(BF16), w3.astype(BF16), w2.astype(BF16))


def _router_kernel(x_ref, g_ref, wr_ref, tri_ref, hn_ref, rank_ref, rankt_ref, gatet_ref, cnt_ref,
                   carry_ref):
    @pl.when(pl.program_id(0) == 0)
    def _():
        carry_ref[...] = jnp.zeros(carry_ref.shape, F32)

    x = x_ref[...]
    hn = x * _rms_scale(x) * g_ref[...]
    hi = hn.astype(BF16)
    hn_ref[...] = hi
    tb = x.shape[0]
    lane = lax.broadcasted_iota(jnp.int32, (tb, LANES), 1).astype(F32)
    lo = (hn - hi.astype(F32)).astype(BF16)
    hw = _dot(hi, wr_ref[...])
    logits = hw[:, 0:LANES] + hw[:, LANES:] + _dot(lo, wr_ref[:, 0:LANES])
    logits = jnp.where(lane < float(N_EXPERTS), logits, NEG)
    m1 = jnp.max(logits, axis=1, keepdims=True)
    i1 = jnp.min(jnp.where(logits == m1, lane, float(LANES)), axis=1, keepdims=True)
    rest = jnp.where(lane == i1, NEG, logits)
    m2 = jnp.max(rest, axis=1, keepdims=True)
    i2 = jnp.min(jnp.where(rest == m2, lane, float(LANES)), axis=1, keepdims=True)
    r = jnp.exp(m2 - m1)
    g1 = 1.0 / (1.0 + r)
    g2 = r * g1
    sel1 = lane == i1
    sel2 = lane == i2
    gates = jnp.where(sel1, g1, jnp.where(sel2, g2, 0.0))
    sel = jnp.where(sel1, 1.0, jnp.where(sel2, 1.0, 0.0))
    before = _dot(tri_ref[...], sel.astype(BF16))
    rank = jnp.where(sel > 0.0, carry_ref[...] + before, -1.0)
    rank_ref[...] = rank[:, 0:N_EXPERTS].astype(jnp.int32)
    rank_t = rank.T
    gates_t = gates.T
    gw = rankt_ref.shape[2]
    carry = carry_ref[...]
    for wi in range(tb // gw):
        ws = slice(wi * gw, (wi + 1) * gw)
        rankt_ref[wi] = rank_t[0:N_EXPERTS, ws]
        gatet_ref[wi] = gates_t[0:N_EXPERTS, ws]
        carry = carry + jnp.sum(sel[ws, :], axis=0, keepdims=True)
        cnt_ref[wi] = carry
    carry_ref[...] = carry


def _router(x2, g, w_router):
    T, D = x2.shape
    tb = min(ROUTER_BLOCK, T)
    gw = min(TOKEN_WINDOW, tb)
    assert T % tb == 0 and tb % gw == 0
    nb = T // tb
    nwb = tb // gw
    tri = np.tril(np.ones((tb, tb), np.float32), -1)
    wr = jnp.pad(w_router.astype(F32), ((0, 0), (0, LANES - N_EXPERTS)))
    wh = wr.astype(BF16)
    wl = (wr - wh.astype(F32)).astype(BF16)
    return pl.pallas_call(
        _router_kernel,
        grid=(nb,),
        in_specs=[pl.BlockSpec((tb, D), lambda i: (i, 0)),
                  _const_spec((1, D)), _const_spec((D, 2 * LANES)), _const_spec((tb, tb))],
        out_specs=[pl.BlockSpec((tb, D), lambda i: (i, 0)),
                   pl.BlockSpec((tb, N_EXPERTS), lambda i: (i, 0)),
                   pl.BlockSpec((nwb, N_EXPERTS, gw), lambda i: (i, 0, 0)),
                   pl.BlockSpec((nwb, N_EXPERTS, gw), lambda i: (i, 0, 0)),
                   pl.BlockSpec((nwb, 1, LANES), lambda i: (i, 0, 0))],
        out_shape=[jax.ShapeDtypeStruct((T, D), BF16),
                   jax.ShapeDtypeStruct((T, N_EXPERTS), jnp.int32),
                   jax.ShapeDtypeStruct((T // gw, N_EXPERTS, gw), F32),
                   jax.ShapeDtypeStruct((T // gw, N_EXPERTS, gw), F32),
                   jax.ShapeDtypeStruct((T // gw, 1, LANES), F32)],
        scratch_shapes=[pltpu.VMEM((1, LANES), F32)],
        compiler_params=pltpu.CompilerParams(
            dimension_semantics=("arbitrary",), vmem_limit_bytes=VMEM_LIMIT),
        name="router",
    )(x2, g.reshape(1, D).astype(F32), jnp.concatenate([wh, wl], axis=1), jnp.asarray(tri, BF16))


def _moe_kernel(te_ref, nused_ref, wlo_ref, nwin_ref, rbase_ref, sublo_ref, subhi_ref,
                hn_hbm, rankt_ref, gatet_ref, w1_ref, w3_ref, w2_ref,
                o_ref,
                hbuf, sem, gacc, xs_ref, acc_ref):
    i = pl.program_id(0)
    f = pl.program_id(1)
    used = i < nused_ref[0]
    tm = o_ref.shape[0]
    gw = rankt_ref.shape[2]
    nbuf = MOE_WIN_BUFS
    kwin = MOE_GATHER_WINS
    rblk = MOE_ROW_BLOCK
    nsub = tm // rblk

    def window_copy(tile, k, slot):
        w = wlo_ref[tile] + k
        return pltpu.make_async_copy(hn_hbm.at[pl.ds(pl.multiple_of(w * gw, gw), gw), :],
                                     hbuf.at[pl.ds(slot * gw, gw), :], sem.at[slot])

    def start_windows(tile, k_lo):
        for k in range(nbuf):
            @pl.when(k_lo + k < nwin_ref[tile])
            def _():
                window_copy(tile, k_lo + k, k).start(priority=WINDOW_DMA_PRIORITY)

    @pl.when(jnp.logical_and(used, f == 0))
    def _dispatch():
        @pl.when(i == 0)
        def _():
            hbuf[...] = jnp.zeros(hbuf.shape, BF16)
            start_windows(0, 0)

        e = te_ref[i]
        rb = rbase_ref[i].astype(F32)
        nwin = nwin_ref[i]
        w0 = wlo_ref[i]
        last_w = rankt_ref.shape[0] - 1
        def run_phase(p, first):
            k_lo = p * nbuf
            if not first:
                start_windows(i, k_lo)
            for k in range(nbuf):
                @pl.when(k_lo + k < nwin)
                def _():
                    window_copy(i, k_lo + k, k).wait()

            def select(s, c):
                k0 = jnp.maximum(sublo_ref[i * nsub + s], k_lo)
                k1 = jnp.minimum(subhi_ref[i * nsub + s], k_lo + nbuf - 1)
                kk = jnp.minimum(k0 + c * kwin, jnp.minimum(k_lo + nbuf, nwin) - 1)
                rel, gate = [], []
                for j in range(kwin):
                    live = jnp.logical_and(kk + j >= k0, kk + j <= k1)
                    w = jnp.minimum(w0 + kk + j, last_w)
                    rel.append(jnp.where(live, rankt_ref[w, pl.ds(e, 1), :] - rb, -1.0))
                    gate.append(gatet_ref[w, pl.ds(e, 1), :])
                rel = jnp.concatenate(rel, axis=1)
                gate = jnp.concatenate(gate, axis=1)
                rows = lax.broadcasted_iota(jnp.int32, (rblk, kwin * gw), 0).astype(F32) + float(s * rblk)
                hit = rel == rows
                gsum = jnp.sum(jnp.where(hit, gate, 0.0), axis=1, keepdims=True)
                src = hbuf[pl.ds(pl.multiple_of((kk - k_lo) * gw, gw), kwin * gw), :]
                return jnp.where(hit, 1.0, 0.0).astype(BF16), src, jnp.broadcast_to(gsum, (rblk, LANES))

            picks = [select(s, 0) for s in range(nsub)]
            prods = [_dot(onehot, src) for onehot, src, _ in picks]
            for s in range(nsub):
                rs = slice(s * rblk, (s + 1) * rblk)
                if first:
                    xs_ref[rs, :] = prods[s].astype(BF16)
                    gacc[rs, :] = picks[s][2]
                else:
                    xs_ref[rs, :] += prods[s].astype(BF16)
                    gacc[rs, :] += picks[s][2]

            for s in range(nsub):
                rs = slice(s * rblk, (s + 1) * rblk)
                k0 = jnp.maximum(sublo_ref[i * nsub + s], k_lo)
                k1 = jnp.minimum(subhi_ref[i * nsub + s], k_lo + nbuf - 1)

                def chunk_body(c, cc, s=s, rs=rs):
                    onehot, src, gsum = select(s, c)
                    xs_ref[rs, :] += _dot(onehot, src).astype(BF16)
                    gacc[rs, :] += gsum
                    return cc

                lax.fori_loop(1, jnp.maximum((k1 - k0 + kwin) // kwin, 0), chunk_body, 0)

        run_phase(0, True)
        lax.fori_loop(1, (nwin + nbuf - 1) // nbuf, lambda p, c: (run_phase(p, False), c)[1], 0)

        @pl.when(i + 1 < nused_ref[0])
        def _():
            start_windows(i + 1, 0)

    @pl.when(used)
    def _ffn():
        x = xs_ref[...]
        part = None
        for c in range(w1_ref.shape[2] // MOE_FF_CHUNK):
            sl = slice(c * MOE_FF_CHUNK, (c + 1) * MOE_FF_CHUNK)
            h1 = _dot(x, w1_ref[0, :, sl])
            h3 = _dot(x, w3_ref[0, :, sl])
            p = _dot((_silu(h1) * h3).astype(BF16), w2_ref[0, sl, :].astype(BF16))
            part = p if part is None else part + p

        @pl.when(f == 0)
        def _():
            acc_ref[...] = part

        @pl.when(f > 0)
        def _():
            acc_ref[...] += part

    @pl.when(f == pl.num_programs(1) - 1)
    def _():
        @pl.when(used)
        def _():
            o_ref[...] = (acc_ref[...] * gacc[:, 0:1]).astype(BF16)

        @pl.when(jnp.logical_not(used))
        def _():
            o_ref[...] = jnp.zeros(o_ref.shape, BF16)


def _moe_grouped(hn, rankt, gatet, tile_expert, n_used, wlo, nwin, rbase, sublo, subhi, nt, w1, w3, w2):
    T, D = hn.shape
    nbp, _, gw = rankt.shape
    E, _, ff = w1.shape
    tm, tf = MOE_TILE, MOE_FF_TILE
    assert ff % tf == 0 and tf % MOE_FF_CHUNK == 0 and tm % MOE_ROW_BLOCK == 0
    nf = ff // tf

    def fsel(i, f, nu):
        return jnp.where(i < nu[0], f, nf - 1)

    grid_spec = pltpu.PrefetchScalarGridSpec(
        num_scalar_prefetch=7,
        grid=(nt, nf),
        in_specs=[pl.BlockSpec(memory_space=pl.ANY),
                  pl.BlockSpec((nbp, N_EXPERTS, gw), lambda i, f, *_: (0, 0, 0), pipeline_mode=pl.Buffered(1)),
                  pl.BlockSpec((nbp, N_EXPERTS, gw), lambda i, f, *_: (0, 0, 0), pipeline_mode=pl.Buffered(1)),
                  pl.BlockSpec((1, D, tf), lambda i, f, te, nu, *_: (te[i], 0, fsel(i, f, nu))),
                  pl.BlockSpec((1, D, tf), lambda i, f, te, nu, *_: (te[i], 0, fsel(i, f, nu))),
                  pl.BlockSpec((1, tf, D), lambda i, f, te, nu, *_: (te[i], fsel(i, f, nu), 0))],
        out_specs=pl.BlockSpec((tm, D), lambda i, f, *_: (i, 0)),
        scratch_shapes=[pltpu.VMEM(((MOE_WIN_BUFS + MOE_GATHER_WINS - 1) * gw, D), BF16),
                        pltpu.SemaphoreType.DMA((MOE_WIN_BUFS,)),
                        pltpu.VMEM((tm, LANES), F32),
                        pltpu.VMEM((tm, D), BF16),
                        pltpu.VMEM((tm, D), F32)],
    )
    return pl.pallas_call(
        _moe_kernel,
        grid_spec=grid_spec,
        out_shape=jax.ShapeDtypeStruct((nt * tm, D), BF16),
        compiler_params=pltpu.CompilerParams(
            dimension_semantics=("arbitrary", "arbitrary"), vmem_limit_bytes=VMEM_LIMIT),
        name="moe_ffn",
    )(tile_expert, n_used, wlo, nwin, rbase, sublo, subhi, hn, rankt, gatet, w1, w3, w2)


def _combine_kernel(ws_ref, nblk_ref, nround_ref,
                    x_ref, rank_ref, start_ref, g_ref, ys_hbm, o_ref, ybuf, sem):
    j = pl.program_id(0)
    nj = pl.num_programs(0)
    tt = x_ref.shape[0]
    wc = COMBINE_WIN
    extra_slot = 2

    def window_copies(tile, rnd, slot):
        cps = []
        for e in range(N_EXPERTS):
            t = tile * N_EXPERTS + e
            s = ws_ref[t] + jnp.where(rnd < nblk_ref[t], rnd, 0) * wc
            cps.append(pltpu.make_async_copy(ys_hbm.at[pl.ds(pl.multiple_of(s, BF16_ROWS), wc), :],
                                             ybuf.at[slot, pl.ds(e * wc, wc), :], sem.at[slot]))
        return cps

    def select(rnd, slot):
        rank = rank_ref[...]
        pos = jnp.where(rank >= 0, rank + start_ref[...], -1)
        lane = lax.broadcasted_iota(jnp.int32, (tt, wc), 1)
        parts = []
        for e in range(N_EXPERTS):
            t = j * N_EXPERTS + e
            base = jnp.where(rnd < nblk_ref[t], ws_ref[t] + rnd * wc, -2 * wc)
            parts.append(jnp.where(pos[:, e:e + 1] == base + lane, 1.0, 0.0).astype(BF16))
        return _dot(jnp.concatenate(parts, axis=1), ybuf[slot])

    @pl.when(j == 0)
    def _():
        for cp in window_copies(0, 0, 0):
            cp.start(priority=WINDOW_DMA_PRIORITY)

    @pl.when(j + 1 < nj)
    def _():
        for cp in window_copies(j + 1, 0, (j + 1) % 2):
            cp.start(priority=WINDOW_DMA_PRIORITY)

    for cp in window_copies(j, 0, j % 2):
        cp.wait()
    acc = select(0, j % 2)

    def round_body(rnd, acc):
        for cp in window_copies(j, rnd, extra_slot):
            cp.start(priority=WINDOW_DMA_PRIORITY)
        for cp in window_copies(j, rnd, extra_slot):
            cp.wait()
        return acc + select(rnd, extra_slot)

    acc = lax.fori_loop(1, nround_ref[j], round_body, acc)
    x = x_ref[...] + acc
    o_ref[...] = x * _rms_scale(x) * g_ref[...]


def _combine(x2, rank, start, ys, ws, nblk, nround, g, tt):
    T, D = x2.shape
    grid_spec = pltpu.PrefetchScalarGridSpec(
        num_scalar_prefetch=3,
        grid=(T // tt,),
        in_specs=[pl.BlockSpec((tt, D), lambda j, *_: (j, 0)),
                  pl.BlockSpec((tt, N_EXPERTS), lambda j, *_: (j, 0)),
                  pl.BlockSpec((1, N_EXPERTS), lambda j, *_: (0, 0)),
                  pl.BlockSpec((1, D), lambda j, *_: (0, 0)),
                  pl.BlockSpec(memory_space=pl.ANY)],
        out_specs=pl.BlockSpec((tt, D), lambda j, *_: (j, 0)),
        scratch_shapes=[pltpu.VMEM((3, N_EXPERTS * COMBINE_WIN, D), BF16),
                        pltpu.SemaphoreType.DMA((3,))],
    )
    return pl.pallas_call(
        _combine_kernel,
        grid_spec=grid_spec,
        out_shape=jax.ShapeDtypeStruct((T, D), F32),
        compiler_params=pltpu.CompilerParams(
            dimension_semantics=("arbitrary",), vmem_limit_bytes=VMEM_LIMIT),
        name="combine",
    )(ws, nblk, nround, x2, rank, start.reshape(1, N_EXPERTS).astype(jnp.int32), g.reshape(1, D).astype(F32), ys)


def _moe_layer(x2, g, w_router, w1, w3, w2, final_g):
    T, D = x2.shape
    tm = MOE_TILE
    hn, rank, rankt, gatet, cnt = _router(x2, g, w_router)
    nb, _, gw = rankt.shape
    cb = jnp.concatenate([jnp.zeros((1, N_EXPERTS), jnp.int32),
                          cnt[:, 0, :N_EXPERTS].astype(jnp.int32)], axis=0)
    counts = cb[-1]
    padded = (counts + tm - 1) // tm * tm
    start = jnp.cumsum(padded) - padded
    nt = (2 * T) // tm + N_EXPERTS + 1
    n_used = (jnp.sum(padded) // tm).astype(jnp.int32)
    tile_lo = jnp.arange(nt, dtype=jnp.int32) * tm
    te = jnp.sum((tile_lo[:, None] >= (start + padded)[None, :]).astype(jnp.int32), axis=1)
    te = jnp.minimum(te, N_EXPERTS - 1).astype(jnp.int32)
    rbase = tile_lo - start[te]
    nvalid = jnp.clip(counts[te] - rbase, 0, tm)
    cb_after = cb[1:, :][:, te]
    wlo = jnp.sum((cb_after <= rbase[None, :]).astype(jnp.int32), axis=0)
    whi = jnp.sum((cb_after <= (rbase + nvalid - 1)[None, :]).astype(jnp.int32), axis=0)
    nwin = jnp.where(nvalid > 0, whi - wlo + 1, 0).astype(jnp.int32)
    wlo = jnp.minimum(wlo, nb - 1).astype(jnp.int32)
    nsub = tm // MOE_ROW_BLOCK
    r0 = rbase[:, None] + jnp.arange(nsub, dtype=jnp.int32)[None, :] * MOE_ROW_BLOCK
    nv = jnp.clip(counts[te][:, None] - r0, 0, MOE_ROW_BLOCK)
    sub_a = jnp.sum((cb_after[:, :, None] <= r0[None]).astype(jnp.int32), axis=0)
    sub_b = jnp.sum((cb_after[:, :, None] <= (r0 + nv - 1)[None]).astype(jnp.int32), axis=0)
    sublo = jnp.where(nv > 0, sub_a - wlo[:, None], 0).astype(jnp.int32)
    subhi = jnp.where(nv > 0, sub_b - wlo[:, None], -1).astype(jnp.int32)
    ys = _moe_grouped(hn, rankt, gatet, te, n_used.reshape(1), wlo, nwin, rbase.astype(jnp.int32),
                      sublo.reshape(-1), subhi.reshape(-1), nt,
                      w1.astype(BF16), w3.astype(BF16), w2)
    tt = min(COMBINE_TILE, T)
    assert T % tt == 0 and tt % gw == 0
    assert COMBINE_WIN % LANES == 0
    cbt = cb[::tt // gw]
    first = start[None, :] + cbt[:-1, :]
    need = cbt[1:, :] - cbt[:-1, :]
    ws = first // BF16_ROWS * BF16_ROWS
    nblk = jnp.where(need > 0, (first - ws + need + COMBINE_WIN - 1) // COMBINE_WIN, 0)
    nround = jnp.maximum(jnp.max(nblk, axis=1), 1)
    return _combine(x2, rank, start, ys, ws.reshape(-1).astype(jnp.int32), nblk.reshape(-1).astype(jnp.int32),
                    nround.astype(jnp.int32), final_g, tt)


def kernel(x, mix_norm_g, w_in, conv_w, conv_b, w_q, w_k, w_v, w_if, b_if, skip_m, mh_norm_g, gm_v_g, w_sp, b_sp, gm_out_g, w_out, ffn_norm_g, dense_w1, dense_w3, dense_w2, moe_router, moe_w1, moe_w3, moe_w2, final_norm_g):
    B, S, D = x.shape
    depth = w_in.shape[0]
    assert depth == 2 and dense_w1.shape[0] == 1 and moe_w1.shape[0] == 1
    for l in range(depth):
        x = _mixer(x, mix_norm_g[l], w_in[l], conv_w[l], conv_b[l], w_q[l], w_k[l], w_v[l],
                   w_if[l], b_if[l], skip_m[l], mh_norm_g[l], gm_v_g[l], w_sp[l], b_sp[l],
                   gm_out_g[l], w_out[l])
        x2 = x.reshape(B * S, D)
        if l % 2 == 0:
            x = _ffn_dense(x2, ffn_norm_g[l], dense_w1[l // 2], dense_w3[l // 2],
                           dense_w2[l // 2]).reshape(B, S, D)
        else:
            x = _moe_layer(x2, ffn_norm_g[l], moe_router[l // 2], moe_w1[l // 2], moe_w3[l // 2],
                           moe_w2[l // 2], final_norm_g).reshape(B, S, D)
    return x
```
